```python
import math
import jax, jax.numpy as jnp
from jax import lax
import numpy as np

D_MODEL = 1024
BATCH = 4
SEQ = 4096
DEPTH = 1

N_META = 16
MIX_WIDTH = D_MODEL
RWKV_WIDTH = MIX_WIDTH // 2
ATT_WIDTH = MIX_WIDTH - RWKV_WIDTH
HEAD_DIM = 64
RWKV_HEADS = RWKV_WIDTH // HEAD_DIM
ATT_HEADS = ATT_WIDTH // HEAD_DIM
DECAY_RANK = 64
ICLR_RANK = 64
GATE_RANK = 128
Q_LORA_RANK = 256
IDX_HEADS = 8
IDX_DIM = 64
IDX_ROPE_DIM = 32
INDEX_TOPK = 256
QBLOCK = 128
ROPE_THETA = 10000.0
N_GROUPS = 4
EXPERTS_PER_GROUP = 8
N_EXPERTS = N_GROUPS * EXPERTS_PER_GROUP
D_EXPERT = 256
TOPK_IN_GROUP = 2
DN_ALPHA = (2.0 * DEPTH) ** 0.25
DN_BETA = (8.0 * DEPTH) ** -0.25
LN_EPS = 1e-5
RMS_EPS = 1e-6
GN_EPS = 64e-5

RWKV_SPLITS = (RWKV_WIDTH, RWKV_WIDTH, RWKV_WIDTH, DECAY_RANK, ICLR_RANK, GATE_RANK)
ATT_SPLITS = (Q_LORA_RANK, ATT_WIDTH, ATT_WIDTH, IDX_DIM, IDX_HEADS)
RWKV_COLS = sum(RWKV_SPLITS)
ATT_COLS = sum(ATT_SPLITS)
IN_COLS = RWKV_COLS + ATT_COLS

kernel_name = "hymba_rwkv7_dsa_hmoe_deepnorm"

F32 = jnp.float32


def split_cols(u, sizes):
    outs, start = [], 0
    for s in sizes:
        outs.append(u[..., start:start + s])
        start += s
    return outs


def layer_norm(x, g, b, eps=LN_EPS):
    xf = x.astype(F32)
    mu = jnp.mean(xf, -1, keepdims=True)
    var = jnp.mean(jnp.square(xf - mu), -1, keepdims=True)
    return ((xf - mu) * lax.rsqrt(var + eps) * g.astype(F32) + b.astype(F32)).astype(x.dtype)


def rms_norm(x, g, eps=RMS_EPS):
    xf = x.astype(F32)
    return (xf * lax.rsqrt(jnp.mean(jnp.square(xf), -1, keepdims=True) + eps) * g.astype(F32)).astype(x.dtype)


def rope(x, positions, rot_dim):
    half = rot_dim // 2
    inv = 1.0 / (ROPE_THETA ** (jnp.arange(half, dtype=F32) / half))
    ang = positions.astype(F32)[:, None] * inv[None, :]
    cos, sin = jnp.cos(ang)[:, None, :], jnp.sin(ang)[:, None, :]
    xr = x[..., :rot_dim].astype(F32)
    x1, x2 = xr[..., :half], xr[..., half:]
    rot = jnp.concatenate([x1 * cos - x2 * sin, x2 * cos + x1 * sin], -1).astype(x.dtype)
    return jnp.concatenate([rot, x[..., rot_dim:]], -1)


def token_shift(u):
    return jnp.pad(u, ((0, 0), (1, 0), (0, 0)))[:, :-1]


def rwkv7_time_mix(u, mu, w0, w2, a0, a2, g2, k_k, k_a, r_k, lnx_g, lnx_b):
    B, L, _ = u.shape
    H, N = RWKV_HEADS, HEAD_DIM
    u = u + (token_shift(u) - u) * mu
    r, k, v, wd, ad, gd = split_cols(u, RWKV_SPLITS)
    w_log = -jax.nn.softplus(-(w0 + jnp.tanh(wd) @ w2).astype(F32)) - 0.5
    decay = jnp.exp(-jnp.exp(w_log))
    a = jax.nn.sigmoid((a0 + ad @ a2).astype(F32))
    g = jax.nn.sigmoid(gd) @ g2
    heads = lambda t: t.astype(F32).reshape(B, L, H, N)
    kk = heads(k * k_k)
    kk = kk / jnp.maximum(jnp.sqrt(jnp.sum(kk * kk, -1, keepdims=True)), 1e-12)
    k_mod = k.astype(F32) * (1.0 + (a - 1.0) * k_a.astype(F32))
    r_h, k_h, v_h, w_h, a_h = heads(r), heads(k_mod), heads(v), heads(decay), heads(a)

    def step(S, inp):
        r_t, w_t, k_t, v_t, kk_t, a_t = inp
        sk = jnp.einsum('bhij,bhj->bhi', S, kk_t)
        S = (S * w_t[:, :, None, :] - sk[..., None] * (kk_t * a_t)[:, :, None, :]
             + v_t[..., None] * k_t[:, :, None, :])
        return S, jnp.einsum('bhij,bhj->bhi', S, r_t)

    tm = lambda t: jnp.moveaxis(t, 1, 0)
    S0 = jnp.zeros((B, H, N, N), F32)
    _, o = lax.scan(step, S0, (tm(r_h), tm(w_h), tm(k_h), tm(v_h), tm(kk), tm(a_h)))
    o = jnp.moveaxis(o, 0, 1)
    mu_o = jnp.mean(o, -1, keepdims=True)
    var_o = jnp.mean(jnp.square(o - mu_o), -1, keepdims=True)
    o = ((o - mu_o) * lax.rsqrt(var_o + GN_EPS)).reshape(B, L, H * N) * lnx_g.astype(F32) + lnx_b.astype(F32)
    bonus = (jnp.sum(r_h * k_h * r_k.astype(F32), -1, keepdims=True) * v_h).reshape(B, L, H * N)
    return ((o + bonus) * g.astype(F32)).astype(u.dtype)


def dsa_attention(u, positions, k_sel, qnorm_g, wuq, idx_wq, idx_kn_g, idx_kn_b):
    B, L, _ = u.shape
    H, Dh, HI, DI = ATT_HEADS, HEAD_DIM, IDX_HEADS, IDX_DIM
    cq, k, v, kidx, widx = split_cols(u, ATT_SPLITS)
    cq = rms_norm(cq, qnorm_g)
    q = rope((cq @ wuq).reshape(B, L, H, Dh), positions, Dh)
    qi = rope((cq @ idx_wq).reshape(B, L, HI, DI), positions, IDX_ROPE_DIM)
    k = rope(k.reshape(B, L, H, Dh), positions, Dh)
    v = v.reshape(B, L, H, Dh)
    kidx = rope(layer_norm(kidx, idx_kn_g, idx_kn_b)[:, :, None, :], positions, IDX_ROPE_DIM)[:, :, 0]
    kidx_f = kidx.astype(F32) * (DI ** -0.5)
    widx = widx * (HI ** -0.5)

    n_blk = -(-L // QBLOCK)
    pad = n_blk * QBLOCK - L
    def to_blocks(t):
        t = jnp.pad(t, ((0, 0), (0, pad)) + ((0, 0),) * (t.ndim - 2))
        return jnp.swapaxes(t.reshape((B, n_blk, QBLOCK) + t.shape[2:]), 0, 1)
    t_blocks = jnp.arange(n_blk * QBLOCK, dtype=jnp.int32).reshape(n_blk, QBLOCK)
    key_pos = jnp.arange(L, dtype=jnp.int32)
    neg = jnp.finfo(F32).min

    def block(args):
        qb, qib, wib, t = args
        s_idx = jnp.einsum('bthd,bsd->bths', qib.astype(F32), kidx_f)
        score = jnp.einsum('bths,bth->bts', jax.nn.relu(s_idx), wib.astype(F32))
        adm = key_pos[None, :] <= t[:, None]
        score = jnp.where(adm[None], score, neg)
        _, sel = lax.top_k(score, k_sel)
        valid = sel <= t[None, :, None]
        kg = jax.vmap(lambda kb, ib: kb[ib])(k, sel)
        vg = jax.vmap(lambda vb, ib: vb[ib])(v, sel)
        logits = jnp.einsum('bthd,btkhd->bhtk', qb.astype(F32), kg.astype(F32)) * (Dh ** -0.5)
        logits = jnp.where(valid[:, None], logits, neg)
        p = jax.nn.softmax(logits, axis=-1)
        return jnp.einsum('bhtk,btkhd->bthd', p, vg.astype(F32)).astype(qb.dtype)

    ob = lax.map(block, (to_blocks(q), to_blocks(qi), to_blocks(widx), t_blocks))
    return jnp.swapaxes(ob, 0, 1).reshape(B, n_blk * QBLOCK, H * Dh)[:, :L]


def hier_moe(h, rgw, rgb, rew, reb, wg, wu, wd):
    B, L, D = h.shape
    t = h.reshape(B * L, D)
    grp_logits = (t @ rgw + rgb).astype(F32)
    grp_prob = jax.nn.softmax(grp_logits, -1)
    gsel = jnp.argmax(grp_logits, -1)
    ggate = jnp.take_along_axis(grp_prob, gsel[:, None], -1)
    exp_logits = (t @ rew + reb).astype(F32).reshape(-1, N_GROUPS, EXPERTS_PER_GROUP)
    in_grp = jnp.take_along_axis(exp_logits, gsel[:, None, None], axis=1)[:, 0]
    top_v, top_i = lax.top_k(in_grp, TOPK_IN_GROUP)
    wts = jax.nn.softmax(top_v, -1) * ggate
    eidx = gsel[:, None] * EXPERTS_PER_GROUP + top_i
    gates = jnp.sum(jax.nn.one_hot(eidx, N_EXPERTS, dtype=F32) * wts[..., None], axis=1)

    def body(acc, xs):
        wg_e, wu_e, wd_e, g_e = xs
        y = (jax.nn.silu(t @ wg_e) * (t @ wu_e)) @ wd_e
        return acc + g_e[:, None].astype(t.dtype) * y, None

    acc, _ = lax.scan(body, jnp.zeros_like(t), (wg, wu, wd, gates.T))
    return acc.reshape(B, L, D)


def setup_inputs(seed: int = 0) -> dict:
    key = jax.random.key(seed)
    ks = iter(jax.random.split(key, 40))
    def nrm(shape, scale):
        return jax.random.normal(next(ks), shape, F32) * scale
    def gain(shape):
        return 1.0 + nrm(shape, 0.02)
    col_scale = jnp.ones((IN_COLS,), F32)
    col_scale = col_scale.at[2 * RWKV_WIDTH:3 * RWKV_WIDTH].set(DN_BETA)
    v_att = RWKV_COLS + Q_LORA_RANK + ATT_WIDTH
    col_scale = col_scale.at[v_att:v_att + ATT_WIDTH].set(DN_BETA)
    return {
        "x": nrm((BATCH, SEQ, D_MODEL), 1.0),
        "meta_tokens": nrm((N_META, D_MODEL), 1.0),
        "ln_emb_g": gain((D_MODEL,)),
        "ln_emb_b": nrm((D_MODEL,), 0.02),
        "w_in": nrm((DEPTH, D_MODEL, IN_COLS), D_MODEL ** -0.5) * col_scale,
        "rw_mu": jax.random.uniform(next(ks), (DEPTH, RWKV_COLS), F32),
        "rw_w0": jax.random.uniform(next(ks), (DEPTH, RWKV_WIDTH), F32, minval=-6.0, maxval=1.0),
        "rw_w2": nrm((DEPTH, DECAY_RANK, RWKV_WIDTH), 0.1 * DECAY_RANK ** -0.5),
        "rw_a0": nrm((DEPTH, RWKV_WIDTH), 0.5),
        "rw_a2": nrm((DEPTH, ICLR_RANK, RWKV_WIDTH), 0.5 * ICLR_RANK ** -0.5),
        "rw_g2": nrm((DEPTH, GATE_RANK, RWKV_WIDTH), GATE_RANK ** -0.5),
        "rw_kk": 0.85 + nrm((DEPTH, RWKV_WIDTH), 0.05),
        "rw_ka": 1.0 + nrm((DEPTH, RWKV_WIDTH), 0.05),
        "rw_rk": nrm((DEPTH, RWKV_HEADS, HEAD_DIM), 0.1),
        "rw_lnx_g": gain((DEPTH, RWKV_WIDTH)),
        "rw_lnx_b": nrm((DEPTH, RWKV_WIDTH), 0.02),
        "att_qnorm_g": gain((DEPTH, Q_LORA_RANK)),
        "att_wuq": nrm((DEPTH, Q_LORA_RANK, ATT_WIDTH), Q_LORA_RANK ** -0.5),
        "idx_wq": nrm((DEPTH, Q_LORA_RANK, IDX_HEADS * IDX_DIM), Q_LORA_RANK ** -0.5),
        "idx_knorm_g": gain((DEPTH, IDX_DIM)),
        "idx_knorm_b": nrm((DEPTH, IDX_DIM), 0.02),
        "w_out": nrm((DEPTH, MIX_WIDTH, D_MODEL), DN_BETA * MIX_WIDTH ** -0.5),
        "ln1_g": gain((DEPTH, D_MODEL)),
        "ln1_b": nrm((DEPTH, D_MODEL), 0.02),
        "rt_grp_w": nrm((DEPTH, D_MODEL, N_GROUPS), D_MODEL ** -0.5),
        "rt_grp_b": nrm((DEPTH, N_GROUPS), 0.01),
        "rt_exp_w": nrm((DEPTH, D_MODEL, N_EXPERTS), D_MODEL ** -0.5),
        "rt_exp_b": nrm((DEPTH, N_EXPERTS), 0.01),
        "ex_w_gate": nrm((DEPTH, N_EXPERTS, D_MODEL, D_EXPERT), D_MODEL ** -0.5),
        "ex_w_up": nrm((DEPTH, N_EXPERTS, D_MODEL, D_EXPERT), D_MODEL ** -0.5),
        "ex_w_down": nrm((DEPTH, N_EXPERTS, D_EXPERT, D_MODEL), DN_BETA * D_EXPERT ** -0.5),
        "ln2_g": gain((DEPTH, D_MODEL)),
        "ln2_b": nrm((DEPTH, D_MODEL), 0.02),
    }


def reference(x, meta_tokens, ln_emb_g, ln_emb_b, w_in, rw_mu, rw_w0, rw_w2, rw_a0, rw_a2, rw_g2,
              rw_kk, rw_ka, rw_rk, rw_lnx_g, rw_lnx_b, att_qnorm_g, att_wuq, idx_wq, idx_knorm_g,
              idx_knorm_b, w_out, ln1_g, ln1_b, rt_grp_w, rt_grp_b, rt_exp_w, rt_exp_b,
              ex_w_gate, ex_w_up, ex_w_down, ln2_g, ln2_b):
    B, S, D = x.shape
    k_sel = min(INDEX_TOPK, S // 4)
    meta = jnp.broadcast_to(meta_tokens[None].astype(x.dtype), (B, N_META, D))
    h = layer_norm(jnp.concatenate([meta, x], axis=1), ln_emb_g, ln_emb_b)
    positions = jnp.arange(N_META + S, dtype=jnp.int32)
    for l in range(DEPTH):
        u = h @ w_in[l]
        y_rwkv = rwkv7_time_mix(u[..., :RWKV_COLS], rw_mu[l], rw_w0[l], rw_w2[l], rw_a0[l], rw_a2[l],
                                rw_g2[l], rw_kk[l], rw_ka[l], rw_rk[l], rw_lnx_g[l], rw_lnx_b[l])
        y_att = dsa_attention(u[..., RWKV_COLS:], positions, k_sel, att_qnorm_g[l], att_wuq[l],
                              idx_wq[l], idx_knorm_g[l], idx_knorm_b[l])
        mix = jnp.concatenate([y_rwkv, y_att], axis=-1) @ w_out[l]
        h = layer_norm(DN_ALPHA * h + mix, ln1_g[l], ln1_b[l])
        ffn = hier_moe(h, rt_grp_w[l], rt_grp_b[l], rt_exp_w[l], rt_exp_b[l],
                       ex_w_gate[l], ex_w_up[l], ex_w_down[l])
        h = layer_norm(DN_ALPHA * h + ffn, ln2_g[l], ln2_b[l])
    return h[:, N_META:]
```

```python
import functools

import jax
import jax.numpy as jnp
from jax import lax
from jax.experimental import pallas as pl
from jax.experimental.pallas import tpu as pltpu

F32 = jnp.float32
BF16 = jnp.bfloat16
I32 = jnp.int32
HIGHEST = lax.Precision.HIGHEST

D_MODEL = 1024
N_META = 16
RWKV_WIDTH = 512
ATT_WIDTH = 512
HEAD_DIM = 64
N_HEADS = 8
DECAY_RANK = 64
ICLR_RANK = 64
GATE_RANK = 128
Q_LORA_RANK = 256
IDX_HEADS = 8
IDX_DIM = 64
IDX_ROPE_DIM = 32
INDEX_TOPK = 256
ROPE_THETA = 10000.0
N_GROUPS = 4
EXPERTS_PER_GROUP = 8
N_EXPERTS = N_GROUPS * EXPERTS_PER_GROUP
D_EXPERT = 256
DN_ALPHA = 2.0 ** 0.25
LN_EPS = 1e-5
RMS_EPS = 1e-6
GN_EPS = 64e-5
RWKV_COLS = 3 * RWKV_WIDTH + DECAY_RANK + ICLR_RANK + GATE_RANK
ATT_COLS = Q_LORA_RANK + 2 * ATT_WIDTH + IDX_DIM + IDX_HEADS
ATT_COLS_PAD = 1408

LANES = 128
PAD_ROWS = 256
OFF = PAD_ROWS - N_META
CHUNK = 64
ROWS_A = 256
ROWS_R = 128
TQ = 128
KC = 256
TM = 1024
NEG = -1e30
INT_MIN = -2147483648
VMEM_LIMIT = 56 * 1024 * 1024


def _mm(a, b, precision=None):
    return jnp.dot(a, b, preferred_element_type=F32, precision=precision)


def _mm_nt(a, b, precision=None):
    return lax.dot_general(a, b, (((1,), (1,)), ((), ())), preferred_element_type=F32, precision=precision)


def _mm_tn(a, b, precision=None):
    return lax.dot_general(a, b, (((0,), (0,)), ((), ())), preferred_element_type=F32, precision=precision)


def _sigmoid(x):
    return 1.0 / (1.0 + jnp.exp(-x))


def _layer_norm(x, g, b):
    mu = jnp.mean(x, -1, keepdims=True)
    xc = x - mu
    var = jnp.mean(xc * xc, -1, keepdims=True)
    return xc * lax.rsqrt(var + LN_EPS) * g + b


def _params(*sem):
    return pltpu.CompilerParams(dimension_semantics=sem, vmem_limit_bytes=VMEM_LIMIT)


def _const_spec(shape):
    nd = len(shape)
    return pl.BlockSpec(shape, lambda *_: (0,) * nd)


def _ln_inproj_body(x_ref, meta_ref, g_ref, b_ref, w_ref, ur_ref, ua_ref):
    blk = pl.program_id(1)
    xin = jnp.where(blk == 0, meta_ref[...], x_ref[0])
    h = _layer_norm(xin, g_ref[...], b_ref[...])
    row = lax.broadcasted_iota(I32, (ROWS_A, 1), 0)
    h = jnp.where((blk > 0) | (row >= OFF), h, 0.0)
    hb = h.astype(BF16)
    step = 256
    for n0 in range(0, RWKV_COLS, step):
        n1 = min(n0 + step, RWKV_COLS)
        ur_ref[0, :, n0:n1] = _mm(hb, w_ref[:, n0:n1])
    for n0 in range(0, ATT_COLS_PAD, step):
        n1 = min(n0 + step, ATT_COLS_PAD)
        ua_ref[0, :, n0:n1] = _mm(hb, w_ref[:, RWKV_COLS + n0:RWKV_COLS + n1])


def _ln_inproj(x, meta_pad, g, b, w):
    B, S, D = x.shape
    nblk = (S + PAD_ROWS) // ROWS_A
    Lp = S + PAD_ROWS
    ncols = RWKV_COLS + ATT_COLS_PAD
    return pl.pallas_call(
        _ln_inproj_body,
        grid=(B, nblk),
        in_specs=[
            pl.BlockSpec((1, ROWS_A, D), lambda b_, i: (b_, jnp.maximum(i - 1, 0), 0)),
            _const_spec((ROWS_A, D)),
            _const_spec((1, D)),
            _const_spec((1, D)),
            _const_spec((D, ncols)),
        ],
        out_specs=[
            pl.BlockSpec((1, ROWS_A, RWKV_COLS), lambda b_, i: (b_, i, 0)),
            pl.BlockSpec((1, ROWS_A, ATT_COLS_PAD), lambda b_, i: (b_, i, 0)),
        ],
        out_shape=[
            jax.ShapeDtypeStruct((B, Lp, RWKV_COLS), F32),
            jax.ShapeDtypeStruct((B, Lp, ATT_COLS_PAD), F32),
        ],
        compiler_params=_params("parallel", "arbitrary"),
        name="ln_inproj",
    )(x, meta_pad, g, b, w)


def _rwkv_prep_body(u_ref, prev_ref, mu_ref, w0_ref, w2_ref, a0_ref, a2_ref, g2_ref, kk_ref, ka_ref, rk_ref,
                    bd_ref, l2_ref, e2_ref,
                    rp_ref, kp_ref, bp_ref, ap_ref, v_ref, ke_ref, be_ref, pc_ref, g_ref, bonus_ref):
    blk = pl.program_id(1)
    u = u_ref[0]
    prev = jnp.where(blk == 0, 0.0, prev_ref[0][7:8, :])
    row = lax.broadcasted_iota(I32, (ROWS_R, 1), 0)
    shifted = jnp.where(row == 0, prev, pltpu.roll(u, 1, 0))
    ul = u + (shifted - u) * mu_ref[...]
    W = RWKV_WIDTH
    r = ul[:, 0:W]
    k = ul[:, W:2 * W]
    v = ul[:, 2 * W:3 * W]
    wa = ul[:, 3 * W:3 * W + 128]
    gd = ul[:, 3 * W + 128:3 * W + 256]
    w = w0_ref[...] + _mm(jnp.tanh(wa).astype(BF16), w2_ref[...])
    softplus_neg_w = jnp.maximum(-w, 0.0) + jnp.log(1.0 + jnp.exp(-jnp.abs(w)))
    logd = -jnp.exp(-softplus_neg_w - 0.5)
    a = _sigmoid(a0_ref[...] + _mm(wa.astype(BF16), a2_ref[...]))
    g = _mm(_sigmoid(gd).astype(BF16), g2_ref[...])
    bd = bd_ref[...]
    kkr = k * kk_ref[...]
    kk = kkr / jnp.maximum(jnp.sqrt(_mm(kkr * kkr, bd, HIGHEST)), 1e-12)
    kmod = k * (1.0 + (a - 1.0) * ka_ref[...])
    bonus_ref[0] = _mm(r * kmod * rk_ref[...], bd, HIGHEST) * v
    g_ref[0] = g
    v_ref[0] = v
    cum = _mm(l2_ref[...], logd, HIGHEST)
    tot = _mm(e2_ref[...], logd, HIGHEST)
    beta = kk * a
    rp_ref[0] = r * jnp.exp(cum)
    einv = jnp.exp(-cum)
    kp_ref[0] = kmod * einv
    bp_ref[0] = beta * einv
    ap_ref[0] = -kk * jnp.exp(cum - logd)
    eend = jnp.exp(tot - cum)
    ke_ref[0] = kmod * eend
    be_ref[0] = beta * eend
    pc = jnp.exp(tot)
    for c in range(ROWS_R // CHUNK):
        pc_ref[0, c] = pc[c * CHUNK:c * CHUNK + 8]


def _rwkv_prep(u_r, mu, w0, w2p, a0, a2p, g2, k_k, k_a, r_k, bd, l2, e2):
    B, Lp, _ = u_r.shape
    nblk = Lp // ROWS_R
    W = RWKV_WIDTH
    row_spec = pl.BlockSpec((1, ROWS_R, W), lambda b_, i: (b_, i, 0))
    row_shape = jax.ShapeDtypeStruct((B, Lp, W), F32)
    cpb = ROWS_R // CHUNK
    return pl.pallas_call(
        _rwkv_prep_body,
        grid=(B, nblk),
        in_specs=[
            pl.BlockSpec((1, ROWS_R, RWKV_COLS), lambda b_, i: (b_, i, 0)),
            pl.BlockSpec((1, 8, RWKV_COLS), lambda b_, i: (b_, jnp.maximum(i * (ROWS_R // 8) - 1, 0), 0)),
            _const_spec((1, RWKV_COLS)),
            _const_spec((1, W)),
            _const_spec((128, W)),
            _const_spec((1, W)),
            _const_spec((128, W)),
            _const_spec((128, W)),
            _const_spec((1, W)),
            _const_spec((1, W)),
            _const_spec((1, W)),
            _const_spec((W, W)),
            _const_spec((ROWS_R, ROWS_R)),
            _const_spec((ROWS_R, ROWS_R)),
        ],
        out_specs=[row_spec] * 7 + [pl.BlockSpec((1, cpb, 8, W), lambda b_, i: (b_, i, 0, 0))] + [row_spec] * 2,
        out_shape=[row_shape] * 7 + [jax.ShapeDtypeStruct((B, Lp // CHUNK, 8, W), F32)] + [row_shape] * 2,
        compiler_params=_params("parallel", "arbitrary"),
        name="rwkv_prep",
    )(u_r, u_r, mu, w0, w2p, a0, a2p, g2, k_k, k_a, r_k, bd, l2, e2)


def _rwkv_scan_body(rp_ref, kp_ref, bp_ref, ap_ref, v_ref, ke_ref, be_ref, pc_ref, o_ref, s_ref):
    c = pl.program_id(1)

    @pl.when(c == 0)
    def _():
        s_ref[...] = jnp.zeros_like(s_ref)

    ri = lax.broadcasted_iota(I32, (CHUNK, CHUNK), 0)
    ci = lax.broadcasted_iota(I32, (CHUNK, CHUNK), 1)
    strict = ri > ci
    incl = ri >= ci
    eye = jnp.where(ri == ci, 1.0, 0.0)
    hp = HIGHEST
    for h in range(N_HEADS):
        sl = slice(h * HEAD_DIM, (h + 1) * HEAD_DIM)
        a_ = ap_ref[0, :, sl]
        b_ = bp_ref[0, :, sl]
        k_ = kp_ref[0, :, sl]
        r_ = rp_ref[0, :, sl]
        v_ = v_ref[0, :, sl]
        ke = ke_ref[0, :, sl]
        be = be_ref[0, :, sl]
        pcv = pc_ref[0, 0, 0:1, sl]
        s0 = s_ref[h]
        a_ab = jnp.where(strict, _mm_nt(a_, b_, hp), 0.0)
        a_ak = jnp.where(strict, _mm_nt(a_, k_, hp), 0.0)
        a_rk = jnp.where(incl, _mm_nt(r_, k_, hp), 0.0)
        a_rb = jnp.where(incl, _mm_nt(r_, b_, hp), 0.0)
        t = eye + a_ab
        p = a_ab
        for _ in range(CHUNK.bit_length() - 2):
            p = _mm(p, p, hp)
            t = t + _mm(t, p, hp)
        u_ = _mm(t, _mm_nt(a_, s0, hp) + _mm(a_ak, v_, hp), hp)
        o_ref[0, :, sl] = _mm_nt(r_, s0, hp) + _mm(a_rk, v_, hp) + _mm(a_rb, u_, hp)
        s_ref[h] = s0 * pcv + _mm_tn(v_, ke, hp) + _mm_tn(u_, be, hp)


def _rwkv_scan(rp, kp, bp, ap, v, ke, be, pc):
    B, Lp, W = rp.shape
    nch = Lp // CHUNK
    row_spec = pl.BlockSpec((1, CHUNK, W), lambda b_, c: (b_, c, 0))
    return pl.pallas_call(
        _rwkv_scan_body,
        grid=(B, nch),
        in_specs=[row_spec] * 7 + [pl.BlockSpec((1, 1, 8, W), lambda b_, c: (b_, c, 0, 0))],
        out_specs=row_spec,
        out_shape=jax.ShapeDtypeStruct((B, Lp, W), F32),
        scratch_shapes=[pltpu.VMEM((N_HEADS, HEAD_DIM, HEAD_DIM), F32)],
        compiler_params=_params("parallel", "arbitrary"),
        name="rwkv_scan",
    )(rp, kp, bp, ap, v, ke, be, pc)


def _rwkv_post_body(o_ref, bonus_ref, g_ref, m_ref, lg_ref, lb_ref, y_ref):
    o = o_ref[0]
    m = m_ref[...]
    oc = o - _mm(o, m, HIGHEST)
    var = _mm(oc * oc, m, HIGHEST)
    y = oc * lax.rsqrt(var + GN_EPS) * lg_ref[...] + lb_ref[...]
    y_ref[0] = (y + bonus_ref[0]) * g_ref[0]


def _rwkv_post(o, bonus, g, m, lg, lb):
    B, Lp, W = o.shape
    row_spec = pl.BlockSpec((1, ROWS_A, W), lambda b_, i: (b_, i, 0))
    return pl.pallas_call(
        _rwkv_post_body,
        grid=(B, Lp // ROWS_A),
        in_specs=[row_spec] * 3 + [_const_spec((W, W)), _const_spec((1, W)), _const_spec((1, W))],
        out_specs=row_spec,
        out_shape=jax.ShapeDtypeStruct((B, Lp, W), F32),
        compiler_params=_params("parallel", "arbitrary"),
        name="rwkv_post",
    )(o, bonus, g, m, lg, lb)


def _rope(x, cos, sin, half, first):
    width = x.shape[1]
    rot = jnp.where(first, pltpu.roll(x, width - half, 1), pltpu.roll(x, half, 1))
    return x * cos + rot * sin


def _dsa_prep_body(u_ref, qg_ref, wuq_ref, wiq_ref, kng_ref, knb_ref, cf_ref, sf_ref, cp_ref, sp_ref,
                   q_ref, k_ref, v_ref, qi_ref, kx_ref, wt_ref):
    u = u_ref[0]
    cq = u[:, 0:Q_LORA_RANK]
    k = u[:, Q_LORA_RANK:Q_LORA_RANK + ATT_WIDTH]
    v = u[:, Q_LORA_RANK + ATT_WIDTH:Q_LORA_RANK + 2 * ATT_WIDTH]
    tail = u[:, Q_LORA_RANK + 2 * ATT_WIDTH:]
    cqn = (cq * lax.rsqrt(jnp.mean(cq * cq, -1, keepdims=True) + RMS_EPS) * qg_ref[...]).astype(BF16)
    reps = ATT_WIDTH // LANES
    cf = jnp.concatenate([cf_ref[...]] * reps, axis=1)
    sf = jnp.concatenate([sf_ref[...]] * reps, axis=1)
    cp = jnp.concatenate([cp_ref[...]] * reps, axis=1)
    sp = jnp.concatenate([sp_ref[...]] * reps, axis=1)
    lane_w = lax.broadcasted_iota(I32, (1, ATT_WIDTH), 1) % HEAD_DIM
    first_f = lane_w < HEAD_DIM // 2
    first_p = lane_w < IDX_ROPE_DIM // 2
    q = _rope(_mm(cqn, wuq_ref[...]), cf, sf, HEAD_DIM // 2, first_f)
    q_ref[0] = (q * (HEAD_DIM ** -0.5)).astype(BF16)
    qi = _rope(_mm(cqn, wiq_ref[...]), cp, sp, IDX_ROPE_DIM // 2, first_p)
    qi_ref[0] = qi.astype(BF16)
    k_ref[0] = _rope(k, cf, sf, HEAD_DIM // 2, first_f).astype(BF16)
    v_ref[0] = v.astype(BF16)
    lane = lax.broadcasted_iota(I32, (1, LANES), 1)
    is_key = lane < IDX_DIM
    mu = jnp.sum(jnp.where(is_key, tail, 0.0), -1, keepdims=True) * (1.0 / IDX_DIM)
    tc = jnp.where(is_key, tail - mu, 0.0)
    var = jnp.sum(tc * tc, -1, keepdims=True) * (1.0 / IDX_DIM)
    kn = tc * lax.rsqrt(var + LN_EPS) * kng_ref[...] + knb_ref[...]
    kn = _rope(kn, cp_ref[...], sp_ref[...], IDX_ROPE_DIM // 2, (lane % HEAD_DIM) < IDX_ROPE_DIM // 2)
    kn = kn * (IDX_DIM ** -0.5)
    kx_ref[0] = jnp.where(is_key, kn, pltpu.roll(kn, IDX_DIM, 1)).astype(BF16)
    wt_ref[0] = tail * (IDX_HEADS ** -0.5)


def _dsa_prep(u_a, qg, wuq, wiq, kng, knb, cf, sf, cp, sp):
    B, Lp, _ = u_a.shape
    W = ATT_WIDTH
    row = lambda width: pl.BlockSpec((1, ROWS_A, width), lambda b_, i: (b_, i, 0))
    tab = pl.BlockSpec((ROWS_A, LANES), lambda b_, i: (i, 0))
    return pl.pallas_call(
        _dsa_prep_body,
        grid=(B, Lp // ROWS_A),
        in_specs=[row(ATT_COLS_PAD), _const_spec((1, Q_LORA_RANK)), _const_spec((Q_LORA_RANK, W)),
                  _const_spec((Q_LORA_RANK, W)), _const_spec((1, LANES)), _const_spec((1, LANES)),
                  tab, tab, tab, tab],
        out_specs=[row(W), row(W), row(W), row(W), row(LANES), row(LANES)],
        out_shape=[jax.ShapeDtypeStruct((B, Lp, W), BF16)] * 4
        + [jax.ShapeDtypeStruct((B, Lp, LANES), BF16), jax.ShapeDtypeStruct((B, Lp, LANES), F32)],
        compiler_params=_params("parallel", "arbitrary"),
        name="dsa_prep",
    )(u_a, qg, wuq, wiq, kng, knb, cf, sf, cp, sp)


def _dsa_attn_body(q_ref, qi_ref, wt_ref, k_ref, v_ref, kx_ref, o_ref, key_ref, acc_ref, m_ref, l_ref, j_ref,
                   *, ksel):
    i = pl.program_id(1)
    nkc = (i * TQ + TQ - 1) // KC + 1
    kf = float(ksel)
    lane = lax.broadcasted_iota(I32, (1, LANES), 1)
    lo_half = lane < HEAD_DIM
    trow = i * TQ + lax.broadcasted_iota(I32, (TQ, 1), 0)
    int_min = jnp.int32(INT_MIN)

    def head_operands(ref):
        x = ref[0]
        out = []
        for h in range(N_HEADS):
            pair = x[:, LANES * (h // 2):LANES * (h // 2 + 1)]
            keep = lo_half if h % 2 == 0 else jnp.logical_not(lo_half)
            out.append(jnp.where(keep, pair, jnp.zeros_like(pair)))
        return out

    qis = head_operands(qi_ref)
    wt = wt_ref[0]
    wcols = [wt[:, IDX_DIM + h:IDX_DIM + h + 1] for h in range(IDX_HEADS)]

    def score_chunk(kc, carry):
        ks = pl.multiple_of(kc * KC, KC)
        kx = kx_ref[0, pl.ds(ks, KC), :]
        sc = jnp.zeros((TQ, KC), F32)
        for h in range(IDX_HEADS):
            sc = sc + jnp.maximum(_mm_nt(qis[h], kx), 0.0) * wcols[h]
        sc = sc + 0.0
        bits = lax.bitcast_convert_type(sc, I32)
        key = jnp.where(bits >= 0, bits, bits ^ jnp.int32(0x7FFFFFFF))
        col = ks + lax.broadcasted_iota(I32, (1, KC), 1)
        key = jnp.where(col >= OFF, key, int_min)
        key_ref[:, pl.ds(ks, KC)] = jnp.where(col <= trow, key, int_min)
        return carry

    lax.fori_loop(0, nkc, score_chunk, 0)

    def count_rows(pred):
        def body(kc, cnt):
            ks = pl.multiple_of(kc * KC, KC)
            col = ks + lax.broadcasted_iota(I32, (1, KC), 1)
            return cnt + pred(key_ref[:, pl.ds(ks, KC)], col)
        cnt = lax.fori_loop(0, nkc, body, jnp.zeros((TQ, KC), F32))
        return jnp.sum(cnt, axis=1, keepdims=True)

    def value_bit(bi, prefix):
        cand = prefix | lax.shift_left(jnp.int32(1), 31 - bi)
        cand_s = cand ^ int_min
        cnt = count_rows(lambda key, col: jnp.where(key >= cand_s, 1.0, 0.0))
        return jnp.where(cnt >= kf, cand, prefix)

    thr = lax.fori_loop(0, 32, value_bit, jnp.zeros((TQ, 1), I32)) ^ int_min
    cnt_gt = count_rows(lambda key, col: jnp.where(key > thr, 1.0, 0.0))
    cnt_eq = count_rows(lambda key, col: jnp.where(key == thr, 1.0, 0.0))
    need = kf - cnt_gt

    j_ref[...] = jnp.full(j_ref.shape, 2 ** 30, I32)

    @pl.when(jnp.max(cnt_eq - need) > 0.0)
    def _():
        def index_bit(bi, prefix):
            cand = prefix | lax.shift_left(jnp.int32(1), 12 - bi)
            before = count_rows(
                lambda key, col: jnp.where(key == thr, jnp.where(col < cand, 1.0, 0.0), 0.0))
            return jnp.where(before < need, cand, prefix)
        jst = lax.fori_loop(0, 13, index_bit, jnp.zeros((TQ, 1), I32))
        j_ref[...] = jnp.broadcast_to(jst, j_ref.shape)

    jstar = j_ref[:, 0:1]

    qs = head_operands(q_ref)
    m_ref[...] = jnp.full(m_ref.shape, NEG, F32)
    l_ref[...] = jnp.zeros_like(l_ref)
    acc_ref[...] = jnp.zeros_like(acc_ref)

    def attend_chunk(kc, carry):
        ks = pl.multiple_of(kc * KC, KC)
        key = key_ref[:, pl.ds(ks, KC)]
        col = ks + lax.broadcasted_iota(I32, (1, KC), 1)
        tie = jnp.where(key == thr, jnp.where(col <= jstar, 0.0, NEG), NEG)
        bias = jnp.where(key > thr, 0.0, tie)
        bias = jnp.where(key == int_min, NEG, bias)
        for p in range(N_HEADS // 2):
            kp = k_ref[0, pl.ds(ks, KC), LANES * p:LANES * (p + 1)]
            vp = v_ref[0, pl.ds(ks, KC), LANES * p:LANES * (p + 1)]
            alphas, pvs = [], []
            for hh in range(2):
                h = 2 * p + hh
                s = _mm_nt(qs[h], kp) + bias
                m_old = m_ref[h][:, 0:1]
                m_new = jnp.maximum(m_old, jnp.max(s, axis=1, keepdims=True))
                alpha = jnp.exp(m_old - m_new)
                pe = jnp.exp(s - m_new)
                l_ref[h] = jnp.broadcast_to(alpha * l_ref[h][:, 0:1] + jnp.sum(pe, axis=1, keepdims=True),
                                            (TQ, LANES))
                m_ref[h] = jnp.broadcast_to(m_new, (TQ, LANES))
                alphas.append(alpha)
                pvs.append(_mm(pe.astype(BF16), vp))
            acc_ref[p] = (acc_ref[p] * jnp.where(lo_half, alphas[0], alphas[1])
                          + jnp.where(lo_half, pvs[0], pvs[1]))
        return carry

    lax.fori_loop(0, nkc, attend_chunk, 0)
    for p in range(N_HEADS // 2):
        lsum = jnp.where(lo_half, l_ref[2 * p][:, 0:1], l_ref[2 * p + 1][:, 0:1])
        o_ref[0, :, LANES * p:LANES * (p + 1)] = acc_ref[p] / lsum


def _dsa_attn(q, k, v, qi, kx, wt, ksel):
    B, Lp, W = q.shape
    assert Lp % KC == 0 and Lp <= 8192
    qrow = lambda width: pl.BlockSpec((1, TQ, width), lambda b_, i: (b_, i, 0))
    full = lambda width: pl.BlockSpec((1, Lp, width), lambda b_, i: (b_, 0, 0))
    return pl.pallas_call(
        functools.partial(_dsa_attn_body, ksel=ksel),
        grid=(B, Lp // TQ),
        in_specs=[qrow(W), qrow(W), qrow(LANES), full(W), full(W), full(LANES)],
        out_specs=qrow(W),
        out_shape=jax.ShapeDtypeStruct((B, Lp, W), F32),
        scratch_shapes=[
            pltpu.VMEM((TQ, Lp), I32),
            pltpu.VMEM((N_HEADS // 2, TQ, LANES), F32),
            pltpu.VMEM((N_HEADS, TQ, LANES), F32),
            pltpu.VMEM((N_HEADS, TQ, LANES), F32),
            pltpu.VMEM((TQ, LANES), I32),
        ],
        compiler_params=_params("parallel", "arbitrary"),
        name="dsa_attn",
    )(q, qi, wt, k, v, kx)


def _outproj_router_body(x_ref, yr_ref, ya_ref, eg_ref, eb_ref, wo_ref, g1_ref, b1_ref, wr_ref, br_ref,
                         h_ref, gates_ref):
    h0 = _layer_norm(x_ref[0], eg_ref[...], eb_ref[...])
    mix = (_mm(yr_ref[0].astype(BF16), wo_ref[0:RWKV_WIDTH, :])
           + _mm(ya_ref[0].astype(BF16), wo_ref[RWKV_WIDTH:, :]))
    h1 = _layer_norm(DN_ALPHA * h0 + mix, g1_ref[...], b1_ref[...])
    h_ref[0] = h1
    logits = _mm(h1, wr_ref[...], HIGHEST) + br_ref[...]
    lg = logits[:, 0:LANES]
    le = logits[:, LANES:2 * LANES]
    lane = lax.broadcasted_iota(I32, (1, LANES), 1)
    lanef = lane.astype(F32)
    low = -3e38
    lgm = jnp.where(lane < N_GROUPS, lg, low)
    gmax = jnp.max(lgm, axis=1, keepdims=True)
    gsel = jnp.min(jnp.where(lgm == gmax, lanef, 1e9), axis=1, keepdims=True)
    gsum = jnp.sum(jnp.where(lane < N_GROUPS, jnp.exp(lgm - gmax), 0.0), axis=1, keepdims=True)
    group_of_lane = (lane // EXPERTS_PER_GROUP).astype(F32)
    lem = jnp.where(group_of_lane == gsel, le, low)
    m1 = jnp.max(lem, axis=1, keepdims=True)
    i1 = jnp.min(jnp.where(lem == m1, lanef, 1e9), axis=1, keepdims=True)
    lem2 = jnp.where(lanef == i1, low, lem)
    m2 = jnp.max(lem2, axis=1, keepdims=True)
    i2 = jnp.min(jnp.where(lem2 == m2, lanef, 1e9), axis=1, keepdims=True)
    e2 = jnp.exp(m2 - m1)
    w1 = 1.0 / (1.0 + e2)
    w2 = e2 / (1.0 + e2)
    gates = jnp.where(lanef == i1, w1, jnp.where(lanef == i2, w2, 0.0)) / gsum
    gates_ref[0] = gates


def _outproj_router(x, y_r, y_a, eg, eb, wo, g1, b1, wr, br):
    B, S, D = x.shape
    skip = PAD_ROWS // ROWS_A
    xrow = pl.BlockSpec((1, ROWS_A, D), lambda b_, i: (b_, i, 0))
    yrow = pl.BlockSpec((1, ROWS_A, RWKV_WIDTH), lambda b_, i: (b_, i + skip, 0))
    vec = _const_spec((1, D))
    return pl.pallas_call(
        _outproj_router_body,
        grid=(B, S // ROWS_A),
        in_specs=[xrow, yrow, yrow, vec, vec, _const_spec((D, D)), vec, vec,
                  _const_spec((D, 2 * LANES)), _const_spec((1, 2 * LANES))],
        out_specs=[xrow, pl.BlockSpec((1, ROWS_A, LANES), lambda b_, i: (b_, i, 0))],
        out_shape=[jax.ShapeDtypeStruct((B, S, D), F32), jax.ShapeDtypeStruct((B, S, LANES), F32)],
        compiler_params=_params("parallel", "arbitrary"),
        name="outproj_router",
    )(x, y_r, y_a, eg, eb, wo, g1, b1, wr, br)


def _moe_body(h_ref, gates_ref, wg_ref, wu_ref, wd_ref, g2_ref, b2_ref, o_ref, acc_ref, hb_ref):
    e = pl.program_id(1)

    @pl.when(e == 0)
    def _():
        acc_ref[...] = jnp.zeros_like(acc_ref)
        hb_ref[...] = h_ref[...].astype(BF16)

    t = hb_ref[...]
    lane = lax.broadcasted_iota(I32, (1, LANES), 1)
    gcol = jnp.sum(jnp.where(lane == e, gates_ref[...], 0.0), axis=1, keepdims=True)
    a = _mm(t, wg_ref[0])
    hid = a * _sigmoid(a) * _mm(t, wu_ref[0]) * gcol
    acc_ref[...] += _mm(hid.astype(BF16), wd_ref[0])

    @pl.when(e == N_EXPERTS - 1)
    def _():
        o_ref[...] = _layer_norm(DN_ALPHA * h_ref[...] + acc_ref[...], g2_ref[...], b2_ref[...])


def _moe(h1, gates, wg, wu, wd, g2, b2):
    T, D = h1.shape
    tile = pl.BlockSpec((TM, D), lambda i, e: (i, 0))
    return pl.pallas_call(
        _moe_body,
        grid=(T // TM, N_EXPERTS),
        in_specs=[tile, pl.BlockSpec((TM, LANES), lambda i, e: (i, 0)),
                  pl.BlockSpec((1, D, D_EXPERT), lambda i, e: (e, 0, 0)),
                  pl.BlockSpec((1, D, D_EXPERT), lambda i, e: (e, 0, 0)),
                  pl.BlockSpec((1, D_EXPERT, D), lambda i, e: (e, 0, 0)),
                  _const_spec((1, D)), _const_spec((1, D))],
        out_specs=tile,
        out_shape=jax.ShapeDtypeStruct((T, D), F32),
        scratch_shapes=[pltpu.VMEM((TM, D), F32), pltpu.VMEM((TM, D), BF16)],
        compiler_params=_params("parallel", "arbitrary"),
        name="moe",
    )(h1, gates, wg, wu, wd, g2, b2)


def _rope_tables(Lp):
    pos = jnp.maximum(jnp.arange(Lp, dtype=I32) - OFF, 0).astype(F32)
    j = jnp.arange(LANES) % HEAD_DIM

    def table(half, rot_dim):
        inv = 1.0 / (ROPE_THETA ** (jnp.arange(half, dtype=F32) / half))
        ang = pos[:, None] * inv[None, :]
        cos, sin = jnp.cos(ang)[:, j % half], jnp.sin(ang)[:, j % half]
        rotated = (j < rot_dim)[None, :]
        sign = jnp.where(j < half, -1.0, 1.0)[None, :]
        return jnp.where(rotated, cos, 1.0), jnp.where(rotated, sin * sign, 0.0)

    cf, sf = table(HEAD_DIM // 2, HEAD_DIM)
    cp, sp = table(IDX_ROPE_DIM // 2, IDX_ROPE_DIM)
    return cf, sf, cp, sp


def _block_ones(n, block):
    idx = jnp.arange(n) // block
    return (idx[:, None] == idx[None, :]).astype(F32)


def kernel(x, meta_tokens, ln_emb_g, ln_emb_b, w_in, rw_mu, rw_w0, rw_w2, rw_a0, rw_a2, rw_g2, rw_kk, rw_ka,
           rw_rk, rw_lnx_g, rw_lnx_b, att_qnorm_g, att_wuq, idx_wq, idx_knorm_g, idx_knorm_b, w_out, ln1_g,
           ln1_b, rt_grp_w, rt_grp_b, rt_exp_w, rt_exp_b, ex_w_gate, ex_w_up, ex_w_down, ln2_g, ln2_b):
    B, S, D = x.shape
    assert w_in.shape[0] == 1 and D == D_MODEL and S % TM == 0
    Lp = S + PAD_ROWS
    ksel = min(INDEX_TOPK, S // 4)
    row = lambda t: t.reshape(1, -1)
    W = RWKV_WIDTH

    meta_pad = jnp.zeros((PAD_ROWS, D), F32).at[OFF:].set(meta_tokens)
    w_in_p = jnp.pad(w_in[0], ((0, 0), (0, ATT_COLS_PAD - ATT_COLS))).astype(BF16)
    u_r, u_a = _ln_inproj(x, meta_pad, row(ln_emb_g), row(ln_emb_b), w_in_p)

    w2p = jnp.concatenate([rw_w2[0], jnp.zeros((ICLR_RANK, W), F32)], 0).astype(BF16)
    a2p = jnp.concatenate([jnp.zeros((DECAY_RANK, W), F32), rw_a2[0]], 0).astype(BF16)
    bd = _block_ones(W, HEAD_DIM)
    tri = (jnp.arange(ROWS_R)[:, None] >= jnp.arange(ROWS_R)[None, :]).astype(F32)
    e2 = _block_ones(ROWS_R, CHUNK)
    rp, kp, bp, ap, v, ke, be, pc, g, bonus = _rwkv_prep(
        u_r, row(rw_mu[0]), row(rw_w0[0]), w2p, row(rw_a0[0]), a2p, rw_g2[0].astype(BF16), row(rw_kk[0]),
        row(rw_ka[0]), row(rw_rk[0]), bd, e2 * tri, e2)
    o = _rwkv_scan(rp, kp, bp, ap, v, ke, be, pc)
    y_r = _rwkv_post(o, bonus, g, bd * (1.0 / HEAD_DIM), row(rw_lnx_g[0]), row(rw_lnx_b[0]))

    pad_lanes = lambda t: jnp.pad(t, (0, LANES - t.shape[0])).reshape(1, LANES)
    cf, sf, cp, sp = _rope_tables(Lp)
    q, k, v_a, qi, kx, wt = _dsa_prep(
        u_a, row(att_qnorm_g[0]), att_wuq[0].astype(BF16), idx_wq[0].astype(BF16),
        pad_lanes(idx_knorm_g[0]), pad_lanes(idx_knorm_b[0]), cf, sf, cp, sp)
    y_a = _dsa_attn(q, k, v_a, qi, kx, wt, ksel)

    wr = jnp.concatenate([jnp.pad(rt_grp_w[0], ((0, 0), (0, LANES - N_GROUPS))),
                          jnp.pad(rt_exp_w[0], ((0, 0), (0, LANES - N_EXPERTS)))], axis=1)
    br = jnp.concatenate([pad_lanes(rt_grp_b[0]), pad_lanes(rt_exp_b[0])], axis=1)
    h1, gates = _outproj_router(x, y_r, y_a, row(ln_emb_g), row(ln_emb_b), w_out[0].astype(BF16),
                                row(ln1_g[0]), row(ln1_b[0]), wr, br)
    out = _moe(h1.reshape(B * S, D), gates.reshape(B * S, LANES), ex_w_gate[0].astype(BF16),
               ex_w_up[0].astype(BF16), ex_w_down[0].astype(BF16), row(ln2_g[0]), row(ln2_b[0]))
    return out.reshape(B, S, D)
```

```python
import functools

import jax
import jax.numpy as jnp
from jax import lax
from jax.experimental import pallas as pl
from jax.experimental.pallas import tpu as pltpu

F32 = jnp.float32
BF16 = jnp.bfloat16
I32 = jnp.int32
HIGHEST = lax.Precision.HIGHEST

D_MODEL = 1024
N_META = 16
RWKV_WIDTH = 512
ATT_WIDTH = 512
HEAD_DIM = 64
N_HEADS = 8
DECAY_RANK = 64
ICLR_RANK = 64
GATE_RANK = 128
Q_LORA_RANK = 256
IDX_HEADS = 8
IDX_DIM = 64
IDX_ROPE_DIM = 32
INDEX_TOPK = 256
ROPE_THETA = 10000.0
N_GROUPS = 4
EXPERTS_PER_GROUP = 8
N_EXPERTS = N_GROUPS * EXPERTS_PER_GROUP
D_EXPERT = 256
DN_ALPHA = 2.0 ** 0.25
LN_EPS = 1e-5
RMS_EPS = 1e-6
GN_EPS = 64e-5
RWKV_COLS = 3 * RWKV_WIDTH + DECAY_RANK + ICLR_RANK + GATE_RANK
ATT_COLS = Q_LORA_RANK + 2 * ATT_WIDTH + IDX_DIM + IDX_HEADS
ATT_COLS_PAD = 1408

LANES = 128
PAD_ROWS = 256
OFF = PAD_ROWS - N_META
CHUNK = 64
ROWS_A = 256
ROWS_R = 128
TQ = 128
KC = 256
TM = 1024
NEG = -1e30
INT_MIN = -2147483648
VMEM_LIMIT = 56 * 1024 * 1024


def _mm(a, b, precision=None):
    return jnp.dot(a, b, preferred_element_type=F32, precision=precision)


def _mm_nt(a, b, precision=None):
    return lax.dot_general(a, b, (((1,), (1,)), ((), ())), preferred_element_type=F32, precision=precision)


def _mm_tn(a, b, precision=None):
    return lax.dot_general(a, b, (((0,), (0,)), ((), ())), preferred_element_type=F32, precision=precision)


def _sigmoid(x):
    return 1.0 / (1.0 + jnp.exp(-x))


def _layer_norm(x, g, b):
    mu = jnp.mean(x, -1, keepdims=True)
    xc = x - mu
    var = jnp.mean(xc * xc, -1, keepdims=True)
    return xc * lax.rsqrt(var + LN_EPS) * g + b


def _params(*sem):
    return pltpu.CompilerParams(dimension_semantics=sem, vmem_limit_bytes=VMEM_LIMIT)


def _const_spec(shape):
    nd = len(shape)
    return pl.BlockSpec(shape, lambda *_: (0,) * nd)


def _ln_inproj_body(x_ref, meta_ref, g_ref, b_ref, w_ref, ur_ref, ua_ref):
    blk = pl.program_id(1)
    xin = jnp.where(blk == 0, meta_ref[...], x_ref[0])
    h = _layer_norm(xin, g_ref[...], b_ref[...])
    row = lax.broadcasted_iota(I32, (ROWS_A, 1), 0)
    h = jnp.where((blk > 0) | (row >= OFF), h, 0.0)
    hb = h.astype(BF16)
    step = 256
    for n0 in range(0, RWKV_COLS, step):
        n1 = min(n0 + step, RWKV_COLS)
        ur_ref[0, :, n0:n1] = _mm(hb, w_ref[:, n0:n1])
    for n0 in range(0, ATT_COLS_PAD, step):
        n1 = min(n0 + step, ATT_COLS_PAD)
        ua_ref[0, :, n0:n1] = _mm(hb, w_ref[:, RWKV_COLS + n0:RWKV_COLS + n1])


def _ln_inproj(x, meta_pad, g, b, w):
    B, S, D = x.shape
    nblk = (S + PAD_ROWS) // ROWS_A
    Lp = S + PAD_ROWS
    ncols = RWKV_COLS + ATT_COLS_PAD
    return pl.pallas_call(
        _ln_inproj_body,
        grid=(B, nblk),
        in_specs=[
            pl.BlockSpec((1, ROWS_A, D), lambda b_, i: (b_, jnp.maximum(i - 1, 0), 0)),
            _const_spec((ROWS_A, D)),
            _const_spec((1, D)),
            _const_spec((1, D)),
            _const_spec((D, ncols)),
        ],
        out_specs=[
            pl.BlockSpec((1, ROWS_A, RWKV_COLS), lambda b_, i: (b_, i, 0)),
            pl.BlockSpec((1, ROWS_A, ATT_COLS_PAD), lambda b_, i: (b_, i, 0)),
        ],
        out_shape=[
            jax.ShapeDtypeStruct((B, Lp, RWKV_COLS), F32),
            jax.ShapeDtypeStruct((B, Lp, ATT_COLS_PAD), F32),
        ],
        compiler_params=_params("parallel", "arbitrary"),
        name="ln_inproj",
    )(x, meta_pad, g, b, w)


def _rwkv_prep_body(u_ref, prev_ref, mu_ref, w0_ref, w2_ref, a0_ref, a2_ref, g2_ref, kk_ref, ka_ref, rk_ref,
                    bd_ref, l2_ref, e2_ref,
                    rp_ref, kp_ref, bp_ref, ap_ref, v_ref, ke_ref, be_ref, pc_ref, g_ref, bonus_ref):
    blk = pl.program_id(1)
    u = u_ref[0]
    prev = jnp.where(blk == 0, 0.0, prev_ref[0][7:8, :])
    row = lax.broadcasted_iota(I32, (ROWS_R, 1), 0)
    shifted = jnp.where(row == 0, prev, pltpu.roll(u, 1, 0))
    ul = u + (shifted - u) * mu_ref[...]
    W = RWKV_WIDTH
    r = ul[:, 0:W]
    k = ul[:, W:2 * W]
    v = ul[:, 2 * W:3 * W]
    wa = ul[:, 3 * W:3 * W + 128]
    gd = ul[:, 3 * W + 128:3 * W + 256]
    w = w0_ref[...] + _mm(jnp.tanh(wa).astype(BF16), w2_ref[...])
    softplus_neg_w = jnp.maximum(-w, 0.0) + jnp.log(1.0 + jnp.exp(-jnp.abs(w)))
    logd = -jnp.exp(-softplus_neg_w - 0.5)
    a = _sigmoid(a0_ref[...] + _mm(wa.astype(BF16), a2_ref[...]))
    g = _mm(_sigmoid(gd).astype(BF16), g2_ref[...])
    bd = bd_ref[...]
    kkr = k * kk_ref[...]
    kk = kkr / jnp.maximum(jnp.sqrt(_mm(kkr * kkr, bd, HIGHEST)), 1e-12)
    kmod = k * (1.0 + (a - 1.0) * ka_ref[...])
    bonus_ref[0] = _mm(r * kmod * rk_ref[...], bd, HIGHEST) * v
    g_ref[0] = g
    v_ref[0] = v
    cum = _mm(l2_ref[...], logd, HIGHEST)
    tot = _mm(e2_ref[...], logd, HIGHEST)
    beta = kk * a
    rp_ref[0] = r * jnp.exp(cum)
    einv = jnp.exp(-cum)
    kp_ref[0] = kmod * einv
    bp_ref[0] = beta * einv
    ap_ref[0] = -kk * jnp.exp(cum - logd)
    eend = jnp.exp(tot - cum)
    ke_ref[0] = kmod * eend
    be_ref[0] = beta * eend
    pc = jnp.exp(tot)
    for c in range(ROWS_R // CHUNK):
        pc_ref[0, c] = pc[c * CHUNK:c * CHUNK + 8]


def _rwkv_prep(u_r, mu, w0, w2p, a0, a2p, g2, k_k, k_a, r_k, bd, l2, e2):
    B, Lp, _ = u_r.shape
    nblk = Lp // ROWS_R
    W = RWKV_WIDTH
    row_spec = pl.BlockSpec((1, ROWS_R, W), lambda b_, i: (b_, i, 0))
    row_shape = jax.ShapeDtypeStruct((B, Lp, W), F32)
    cpb = ROWS_R // CHUNK
    return pl.pallas_call(
        _rwkv_prep_body,
        grid=(B, nblk),
        in_specs=[
            pl.BlockSpec((1, ROWS_R, RWKV_COLS), lambda b_, i: (b_, i, 0)),
            pl.BlockSpec((1, 8, RWKV_COLS), lambda b_, i: (b_, jnp.maximum(i * (ROWS_R // 8) - 1, 0), 0)),
            _const_spec((1, RWKV_COLS)),
            _const_spec((1, W)),
            _const_spec((128, W)),
            _const_spec((1, W)),
            _const_spec((128, W)),
            _const_spec((128, W)),
            _const_spec((1, W)),
            _const_spec((1, W)),
            _const_spec((1, W)),
            _const_spec((W, W)),
            _const_spec((ROWS_R, ROWS_R)),
            _const_spec((ROWS_R, ROWS_R)),
        ],
        out_specs=[row_spec] * 7 + [pl.BlockSpec((1, cpb, 8, W), lambda b_, i: (b_, i, 0, 0))] + [row_spec] * 2,
        out_shape=[row_shape] * 7 + [jax.ShapeDtypeStruct((B, Lp // CHUNK, 8, W), F32)] + [row_shape] * 2,
        compiler_params=_params("parallel", "arbitrary"),
        name="rwkv_prep",
    )(u_r, u_r, mu, w0, w2p, a0, a2p, g2, k_k, k_a, r_k, bd, l2, e2)


def _rwkv_scan_body(rp_ref, kp_ref, bp_ref, ap_ref, v_ref, ke_ref, be_ref, pc_ref, o_ref, s_ref):
    c = pl.program_id(1)

    @pl.when(c == 0)
    def _():
        s_ref[...] = jnp.zeros_like(s_ref)

    ri = lax.broadcasted_iota(I32, (CHUNK, CHUNK), 0)
    ci = lax.broadcasted_iota(I32, (CHUNK, CHUNK), 1)
    strict = ri > ci
    incl = ri >= ci
    eye = jnp.where(ri == ci, 1.0, 0.0)
    hp = HIGHEST
    for h in range(N_HEADS):
        sl = slice(h * HEAD_DIM, (h + 1) * HEAD_DIM)
        a_ = ap_ref[0, :, sl]
        b_ = bp_ref[0, :, sl]
        k_ = kp_ref[0, :, sl]
        r_ = rp_ref[0, :, sl]
        v_ = v_ref[0, :, sl]
        ke = ke_ref[0, :, sl]
        be = be_ref[0, :, sl]
        pcv = pc_ref[0, 0, 0:1, sl]
        s0 = s_ref[h]
        a_ab = jnp.where(strict, _mm_nt(a_, b_, hp), 0.0)
        a_ak = jnp.where(strict, _mm_nt(a_, k_, hp), 0.0)
        a_rk = jnp.where(incl, _mm_nt(r_, k_, hp), 0.0)
        a_rb = jnp.where(incl, _mm_nt(r_, b_, hp), 0.0)
        t = eye + a_ab
        p = a_ab
        for _ in range(CHUNK.bit_length() - 2):
            p = _mm(p, p, hp)
            t = t + _mm(t, p, hp)
        u_ = _mm(t, _mm_nt(a_, s0, hp) + _mm(a_ak, v_, hp), hp)
        o_ref[0, :, sl] = _mm_nt(r_, s0, hp) + _mm(a_rk, v_, hp) + _mm(a_rb, u_, hp)
        s_ref[h] = s0 * pcv + _mm_tn(v_, ke, hp) + _mm_tn(u_, be, hp)


def _rwkv_scan(rp, kp, bp, ap, v, ke, be, pc):
    B, Lp, W = rp.shape
    nch = Lp // CHUNK
    row_spec = pl.BlockSpec((1, CHUNK, W), lambda b_, c: (b_, c, 0))
    return pl.pallas_call(
        _rwkv_scan_body,
        grid=(B, nch),
        in_specs=[row_spec] * 7 + [pl.BlockSpec((1, 1, 8, W), lambda b_, c: (b_, c, 0, 0))],
        out_specs=row_spec,
        out_shape=jax.ShapeDtypeStruct((B, Lp, W), F32),
        scratch_shapes=[pltpu.VMEM((N_HEADS, HEAD_DIM, HEAD_DIM), F32)],
        compiler_params=_params("parallel", "arbitrary"),
        name="rwkv_scan",
    )(rp, kp, bp, ap, v, ke, be, pc)


def _rwkv_post_body(o_ref, bonus_ref, g_ref, m_ref, lg_ref, lb_ref, y_ref):
    o = o_ref[0]
    m = m_ref[...]
    oc = o - _mm(o, m, HIGHEST)
    var = _mm(oc * oc, m, HIGHEST)
    y = oc * lax.rsqrt(var + GN_EPS) * lg_ref[...] + lb_ref[...]
    y_ref[0] = (y + bonus_ref[0]) * g_ref[0]


def _rwkv_post(o, bonus, g, m, lg, lb):
    B, Lp, W = o.shape
    row_spec = pl.BlockSpec((1, ROWS_A, W), lambda b_, i: (b_, i, 0))
    return pl.pallas_call(
        _rwkv_post_body,
        grid=(B, Lp // ROWS_A),
        in_specs=[row_spec] * 3 + [_const_spec((W, W)), _const_spec((1, W)), _const_spec((1, W))],
        out_specs=row_spec,
        out_shape=jax.ShapeDtypeStruct((B, Lp, W), F32),
        compiler_params=_params("parallel", "arbitrary"),
        name="rwkv_post",
    )(o, bonus, g, m, lg, lb)


def _rope(x, cos, sin, half, first):
    width = x.shape[1]
    rot = jnp.where(first, pltpu.roll(x, width - half, 1), pltpu.roll(x, half, 1))
    return x * cos + rot * sin


def _dsa_prep_body(u_ref, qg_ref, wuq_ref, wiq_ref, kng_ref, knb_ref, cf_ref, sf_ref, cp_ref, sp_ref,
                   qt_ref, k_ref, vt_ref, qit_ref, kx_ref, wt_ref):
    u = u_ref[0]
    cq = u[:, 0:Q_LORA_RANK]
    k = u[:, Q_LORA_RANK:Q_LORA_RANK + ATT_WIDTH]
    v = u[:, Q_LORA_RANK + ATT_WIDTH:Q_LORA_RANK + 2 * ATT_WIDTH]
    tail = u[:, Q_LORA_RANK + 2 * ATT_WIDTH:]
    cqn = (cq * lax.rsqrt(jnp.mean(cq * cq, -1, keepdims=True) + RMS_EPS) * qg_ref[...]).astype(BF16)
    reps = ATT_WIDTH // LANES
    cf = jnp.concatenate([cf_ref[...]] * reps, axis=1)
    sf = jnp.concatenate([sf_ref[...]] * reps, axis=1)
    cp = jnp.concatenate([cp_ref[...]] * reps, axis=1)
    sp = jnp.concatenate([sp_ref[...]] * reps, axis=1)
    lane_w = lax.broadcasted_iota(I32, (1, ATT_WIDTH), 1) % HEAD_DIM
    first_f = lane_w < HEAD_DIM // 2
    first_p = lane_w < IDX_ROPE_DIM // 2
    q = _rope(_mm(cqn, wuq_ref[...]), cf, sf, HEAD_DIM // 2, first_f)
    qt_ref[0] = (q * (HEAD_DIM ** -0.5)).T.astype(BF16)
    qi = _rope(_mm(cqn, wiq_ref[...]), cp, sp, IDX_ROPE_DIM // 2, first_p)
    qit_ref[0] = qi.T.astype(BF16)
    k_ref[0] = _rope(k, cf, sf, HEAD_DIM // 2, first_f).astype(BF16)
    vt_ref[0] = v.T.astype(BF16)
    lane = lax.broadcasted_iota(I32, (1, LANES), 1)
    is_key = lane < IDX_DIM
    mu = jnp.sum(jnp.where(is_key, tail, 0.0), -1, keepdims=True) * (1.0 / IDX_DIM)
    tc = jnp.where(is_key, tail - mu, 0.0)
    var = jnp.sum(tc * tc, -1, keepdims=True) * (1.0 / IDX_DIM)
    kn = tc * lax.rsqrt(var + LN_EPS) * kng_ref[...] + knb_ref[...]
    kn = _rope(kn, cp_ref[...], sp_ref[...], IDX_ROPE_DIM // 2, (lane % HEAD_DIM) < IDX_ROPE_DIM // 2)
    kn = kn * (IDX_DIM ** -0.5)
    kx_ref[0] = jnp.where(is_key, kn, pltpu.roll(kn, IDX_DIM, 1)).astype(BF16)
    wt_ref[0] = (tail * (IDX_HEADS ** -0.5)).T


def _dsa_prep(u_a, qg, wuq, wiq, kng, knb, cf, sf, cp, sp):
    B, Lp, _ = u_a.shape
    W = ATT_WIDTH
    row = lambda width: pl.BlockSpec((1, ROWS_A, width), lambda b_, i: (b_, i, 0))
    col = lambda height: pl.BlockSpec((1, height, ROWS_A), lambda b_, i: (b_, 0, i))
    tab = pl.BlockSpec((ROWS_A, LANES), lambda b_, i: (i, 0))
    return pl.pallas_call(
        _dsa_prep_body,
        grid=(B, Lp // ROWS_A),
        in_specs=[row(ATT_COLS_PAD), _const_spec((1, Q_LORA_RANK)), _const_spec((Q_LORA_RANK, W)),
                  _const_spec((Q_LORA_RANK, W)), _const_spec((1, LANES)), _const_spec((1, LANES)),
                  tab, tab, tab, tab],
        out_specs=[col(W), row(W), col(W), col(W), row(LANES), col(LANES)],
        out_shape=[jax.ShapeDtypeStruct((B, W, Lp), BF16), jax.ShapeDtypeStruct((B, Lp, W), BF16),
                   jax.ShapeDtypeStruct((B, W, Lp), BF16), jax.ShapeDtypeStruct((B, W, Lp), BF16),
                   jax.ShapeDtypeStruct((B, Lp, LANES), BF16), jax.ShapeDtypeStruct((B, LANES, Lp), F32)],
        compiler_params=_params("parallel", "arbitrary"),
        name="dsa_prep",
    )(u_a, qg, wuq, wiq, kng, knb, cf, sf, cp, sp)


def _fold_rows(x):
    parts = [x[8 * r:8 * r + 8] for r in range(x.shape[0] // 8)]
    while len(parts) > 1:
        parts = [a + b for a, b in zip(parts[0::2], parts[1::2])] + parts[len(parts) & ~1:]
    return parts[0]


def _dsa_attn_body(qt_ref, qit_ref, wt_ref, k_ref, vt_ref, kx_ref, o_ref, key_ref, acc_ref, j_ref, s_ref, pe_ref,
                   *, ksel):
    i = pl.program_id(1)
    nkc = (i * TQ + TQ - 1) // KC + 1
    kf = float(ksel)
    tcol = i * TQ + lax.broadcasted_iota(I32, (1, TQ), 1)
    int_min = jnp.int32(INT_MIN)
    row_in_pair = lax.broadcasted_iota(I32, (LANES, 1), 0)

    def head_operands(ref):
        out = []
        for h in range(N_HEADS):
            pair = ref[0, LANES * (h // 2):LANES * (h // 2 + 1), :]
            keep = (row_in_pair < HEAD_DIM) if h % 2 == 0 else (row_in_pair >= HEAD_DIM)
            out.append(jnp.where(keep, pair, jnp.zeros_like(pair)))
        return out

    def key_rows(ks):
        return ks + lax.broadcasted_iota(I32, (KC, 1), 0)

    qis = head_operands(qit_ref)
    wrows = [wt_ref[0, IDX_DIM + h:IDX_DIM + h + 1, :] for h in range(IDX_HEADS)]

    def score_chunk(kc, carry):
        ks = pl.multiple_of(kc * KC, KC)
        kx = kx_ref[0, pl.ds(ks, KC), :]
        sc = jnp.zeros((KC, TQ), F32)
        for h in range(IDX_HEADS):
            sc = sc + jnp.maximum(_mm(kx, qis[h]), 0.0) * wrows[h]
        sc = sc + 0.0
        bits = lax.bitcast_convert_type(sc, I32)
        key = jnp.where(bits >= 0, bits, bits ^ jnp.int32(0x7FFFFFFF))
        krow = key_rows(ks)
        key = jnp.where(krow >= OFF, key, int_min)
        key_ref[pl.ds(ks, KC), :] = jnp.where(krow <= tcol, key, int_min)
        return carry

    lax.fori_loop(0, nkc, score_chunk, 0)

    def count_keys(pred):
        def body(kc, cnt):
            ks = pl.multiple_of(kc * KC, KC)
            return cnt + _fold_rows(pred(key_ref[pl.ds(ks, KC), :], ks))
        cnt = lax.fori_loop(0, nkc, body, jnp.zeros((8, TQ), F32))
        return jnp.sum(cnt, axis=0, keepdims=True)

    def value_bit(bi, prefix):
        cand = prefix | lax.shift_left(jnp.int32(1), 31 - bi)
        cand_s = cand ^ int_min
        cnt = count_keys(lambda key, ks: jnp.where(key >= cand_s, 1.0, 0.0))
        return jnp.where(cnt >= kf, cand, prefix)

    thr = lax.fori_loop(0, 32, value_bit, jnp.zeros((1, TQ), I32)) ^ int_min
    cnt_gt = count_keys(lambda key, ks: jnp.where(key > thr, 1.0, 0.0))
    cnt_eq = count_keys(lambda key, ks: jnp.where(key == thr, 1.0, 0.0))
    need = kf - cnt_gt

    j_ref[...] = jnp.full(j_ref.shape, 2 ** 30, I32)

    @pl.when(jnp.max(cnt_eq - need) > 0.0)
    def _():
        def index_bit(bi, prefix):
            cand = prefix | lax.shift_left(jnp.int32(1), 12 - bi)
            before = count_keys(
                lambda key, ks: jnp.where(key == thr, jnp.where(key_rows(ks) < cand, 1.0, 0.0), 0.0))
            return jnp.where(before < need, cand, prefix)
        jst = lax.fori_loop(0, 13, index_bit, jnp.zeros((1, TQ), I32))
        j_ref[...] = jnp.broadcast_to(jst, j_ref.shape)

    jstar = j_ref[0:1, :]

    qs = head_operands(qt_ref)
    acc_ref[...] = jnp.zeros_like(acc_ref)

    def attend_chunk(kc, carry):
        ms, ls = carry
        ks = pl.multiple_of(kc * KC, KC)
        key = key_ref[pl.ds(ks, KC), :]
        tie = jnp.where(key == thr, jnp.where(key_rows(ks) <= jstar, 0.0, NEG), NEG)
        bias = jnp.where(key > thr, 0.0, tie)
        bias = jnp.where(key == int_min, NEG, bias)
        for h in range(N_HEADS):
            p = h // 2
            kp = k_ref[0, pl.ds(ks, KC), LANES * p:LANES * (p + 1)]
            s_ref[h] = _mm(kp, qs[h]) + bias
        new_ms, new_ls, alphas = [], [], []
        for h in range(N_HEADS):
            s = s_ref[h]
            m_new = jnp.maximum(ms[h], jnp.max(s, axis=0, keepdims=True))
            alpha = jnp.exp(ms[h] - m_new)
            pe = jnp.exp(s - m_new)
            pe_ref[h] = pe.astype(BF16)
            new_ls.append(alpha * ls[h] + jnp.sum(pe, axis=0, keepdims=True))
            new_ms.append(m_new)
            alphas.append(alpha)
        for h in range(N_HEADS):
            vt = vt_ref[0, HEAD_DIM * h:HEAD_DIM * (h + 1), pl.ds(ks, KC)]
            rows = slice(HEAD_DIM * h, HEAD_DIM * (h + 1))
            acc_ref[rows, :] = acc_ref[rows, :] * alphas[h] + _mm(vt, pe_ref[h])
        return tuple(new_ms), tuple(new_ls)

    init = (tuple(jnp.full((1, TQ), NEG, F32) for _ in range(N_HEADS)),
            tuple(jnp.zeros((1, TQ), F32) for _ in range(N_HEADS)))
    _, ls = lax.fori_loop(0, nkc, attend_chunk, init)
    for p in range(N_HEADS // 2):
        parts = [acc_ref[HEAD_DIM * h:HEAD_DIM * (h + 1), :] / ls[h] for h in (2 * p, 2 * p + 1)]
        o_ref[0, :, LANES * p:LANES * (p + 1)] = jnp.concatenate(parts, axis=0).T


def _dsa_attn(qt, k, vt, qit, kx, wt, ksel):
    B, Lp, W = k.shape
    assert Lp % KC == 0 and Lp <= 8192
    qcol = lambda height: pl.BlockSpec((1, height, TQ), lambda b_, i: (b_, 0, i))
    full = lambda shape: pl.BlockSpec((1,) + shape, lambda b_, i: (b_, 0, 0))
    return pl.pallas_call(
        functools.partial(_dsa_attn_body, ksel=ksel),
        grid=(B, Lp // TQ),
        in_specs=[qcol(W), qcol(W), qcol(LANES), full((Lp, W)), full((W, Lp)), full((Lp, LANES))],
        out_specs=pl.BlockSpec((1, TQ, W), lambda b_, i: (b_, i, 0)),
        out_shape=jax.ShapeDtypeStruct((B, Lp, W), F32),
        scratch_shapes=[
            pltpu.VMEM((Lp, TQ), I32),
            pltpu.VMEM((W, TQ), F32),
            pltpu.VMEM((8, TQ), I32),
            pltpu.VMEM((N_HEADS, KC, TQ), F32),
            pltpu.VMEM((N_HEADS, KC, TQ), BF16),
        ],
        compiler_params=_params("parallel", "arbitrary"),
        name="dsa_attn",
    )(qt, qit, wt, k, vt, kx)


def _outproj_router_body(x_ref, yr_ref, ya_ref, eg_ref, eb_ref, wo_ref, g1_ref, b1_ref, wr_ref, br_ref,
                         h_ref, gates_ref):
    h0 = _layer_norm(x_ref[0], eg_ref[...], eb_ref[...])
    mix = (_mm(yr_ref[0].astype(BF16), wo_ref[0:RWKV_WIDTH, :])
           + _mm(ya_ref[0].astype(BF16), wo_ref[RWKV_WIDTH:, :]))
    h1 = _layer_norm(DN_ALPHA * h0 + mix, g1_ref[...], b1_ref[...])
    h_ref[0] = h1
    logits = _mm(h1, wr_ref[...], HIGHEST) + br_ref[...]
    lg = logits[:, 0:LANES]
    le = logits[:, LANES:2 * LANES]
    lane = lax.broadcasted_iota(I32, (1, LANES), 1)
    lanef = lane.astype(F32)
    low = -3e38
    lgm = jnp.where(lane < N_GROUPS, lg, low)
    gmax = jnp.max(lgm, axis=1, keepdims=True)
    gsel = jnp.min(jnp.where(lgm == gmax, lanef, 1e9), axis=1, keepdims=True)
    gsum = jnp.sum(jnp.where(lane < N_GROUPS, jnp.exp(lgm - gmax), 0.0), axis=1, keepdims=True)
    group_of_lane = (lane // EXPERTS_PER_GROUP).astype(F32)
    lem = jnp.where(group_of_lane == gsel, le, low)
    m1 = jnp.max(lem, axis=1, keepdims=True)
    i1 = jnp.min(jnp.where(lem == m1, lanef, 1e9), axis=1, keepdims=True)
    lem2 = jnp.where(lanef == i1, low, lem)
    m2 = jnp.max(lem2, axis=1, keepdims=True)
    i2 = jnp.min(jnp.where(lem2 == m2, lanef, 1e9), axis=1, keepdims=True)
    e2 = jnp.exp(m2 - m1)
    w1 = 1.0 / (1.0 + e2)
    w2 = e2 / (1.0 + e2)
    gates = jnp.where(lanef == i1, w1, jnp.where(lanef == i2, w2, 0.0)) / gsum
    gates_ref[0] = gates


def _outproj_router(x, y_r, y_a, eg, eb, wo, g1, b1, wr, br):
    B, S, D = x.shape
    skip = PAD_ROWS // ROWS_A
    xrow = pl.BlockSpec((1, ROWS_A, D), lambda b_, i: (b_, i, 0))
    yrow = pl.BlockSpec((1, ROWS_A, RWKV_WIDTH), lambda b_, i: (b_, i + skip, 0))
    vec = _const_spec((1, D))
    return pl.pallas_call(
        _outproj_router_body,
        grid=(B, S // ROWS_A),
        in_specs=[xrow, yrow, yrow, vec, vec, _const_spec((D, D)), vec, vec,
                  _const_spec((D, 2 * LANES)), _const_spec((1, 2 * LANES))],
        out_specs=[xrow, pl.BlockSpec((1, ROWS_A, LANES), lambda b_, i: (b_, i, 0))],
        out_shape=[jax.ShapeDtypeStruct((B, S, D), F32), jax.ShapeDtypeStruct((B, S, LANES), F32)],
        compiler_params=_params("parallel", "arbitrary"),
        name="outproj_router",
    )(x, y_r, y_a, eg, eb, wo, g1, b1, wr, br)


def _moe_body(h_ref, gates_ref, wg_ref, wu_ref, wd_ref, g2_ref, b2_ref, o_ref, acc_ref, hb_ref):
    e = pl.program_id(1)

    @pl.when(e == 0)
    def _():
        acc_ref[...] = jnp.zeros_like(acc_ref)
        hb_ref[...] = h_ref[...].astype(BF16)

    t = hb_ref[...]
    lane = lax.broadcasted_iota(I32, (1, LANES), 1)
    gcol = jnp.sum(jnp.where(lane == e, gates_ref[...], 0.0), axis=1, keepdims=True)
    a = _mm(t, wg_ref[0])
    hid = a * _sigmoid(a) * _mm(t, wu_ref[0]) * gcol
    acc_ref[...] += _mm(hid.astype(BF16), wd_ref[0])

    @pl.when(e == N_EXPERTS - 1)
    def _():
        o_ref[...] = _layer_norm(DN_ALPHA * h_ref[...] + acc_ref[...], g2_ref[...], b2_ref[...])


def _moe(h1, gates, wg, wu, wd, g2, b2):
    T, D = h1.shape
    tile = pl.BlockSpec((TM, D), lambda i, e: (i, 0))
    return pl.pallas_call(
        _moe_body,
        grid=(T // TM, N_EXPERTS),
        in_specs=[tile, pl.BlockSpec((TM, LANES), lambda i, e: (i, 0)),
                  pl.BlockSpec((1, D, D_EXPERT), lambda i, e: (e, 0, 0)),
                  pl.BlockSpec((1, D, D_EXPERT), lambda i, e: (e, 0, 0)),
                  pl.BlockSpec((1, D_EXPERT, D), lambda i, e: (e, 0, 0)),
                  _const_spec((1, D)), _const_spec((1, D))],
        out_specs=tile,
        out_shape=jax.ShapeDtypeStruct((T, D), F32),
        scratch_shapes=[pltpu.VMEM((TM, D), F32), pltpu.VMEM((TM, D), BF16)],
        compiler_params=_params("parallel", "arbitrary"),
        name="moe",
    )(h1, gates, wg, wu, wd, g2, b2)


def _rope_tables(Lp):
    pos = jnp.maximum(jnp.arange(Lp, dtype=I32) - OFF, 0).astype(F32)
    j = jnp.arange(LANES) % HEAD_DIM

    def table(half, rot_dim):
        inv = 1.0 / (ROPE_THETA ** (jnp.arange(half, dtype=F32) / half))
        ang = pos[:, None] * inv[None, :]
        cos, sin = jnp.cos(ang)[:, j % half], jnp.sin(ang)[:, j % half]
        rotated = (j < rot_dim)[None, :]
        sign = jnp.where(j < half, -1.0, 1.0)[None, :]
        return jnp.where(rotated, cos, 1.0), jnp.where(rotated, sin * sign, 0.0)

    cf, sf = table(HEAD_DIM // 2, HEAD_DIM)
    cp, sp = table(IDX_ROPE_DIM // 2, IDX_ROPE_DIM)
    return cf, sf, cp, sp


def _block_ones(n, block):
    idx = jnp.arange(n) // block
    return (idx[:, None] == idx[None, :]).astype(F32)


def kernel(x, meta_tokens, ln_emb_g, ln_emb_b, w_in, rw_mu, rw_w0, rw_w2, rw_a0, rw_a2, rw_g2, rw_kk, rw_ka,
           rw_rk, rw_lnx_g, rw_lnx_b, att_qnorm_g, att_wuq, idx_wq, idx_knorm_g, idx_knorm_b, w_out, ln1_g,
           ln1_b, rt_grp_w, rt_grp_b, rt_exp_w, rt_exp_b, ex_w_gate, ex_w_up, ex_w_down, ln2_g, ln2_b):
    B, S, D = x.shape
    assert w_in.shape[0] == 1 and D == D_MODEL and S % TM == 0
    Lp = S + PAD_ROWS
    ksel = min(INDEX_TOPK, S // 4)
    row = lambda t: t.reshape(1, -1)
    W = RWKV_WIDTH

    meta_pad = jnp.zeros((PAD_ROWS, D), F32).at[OFF:].set(meta_tokens)
    w_in_p = jnp.pad(w_in[0], ((0, 0), (0, ATT_COLS_PAD - ATT_COLS))).astype(BF16)
    u_r, u_a = _ln_inproj(x, meta_pad, row(ln_emb_g), row(ln_emb_b), w_in_p)

    w2p = jnp.concatenate([rw_w2[0], jnp.zeros((ICLR_RANK, W), F32)], 0).astype(BF16)
    a2p = jnp.concatenate([jnp.zeros((DECAY_RANK, W), F32), rw_a2[0]], 0).astype(BF16)
    bd = _block_ones(W, HEAD_DIM)
    tri = (jnp.arange(ROWS_R)[:, None] >= jnp.arange(ROWS_R)[None, :]).astype(F32)
    e2 = _block_ones(ROWS_R, CHUNK)
    rp, kp, bp, ap, v, ke, be, pc, g, bonus = _rwkv_prep(
        u_r, row(rw_mu[0]), row(rw_w0[0]), w2p, row(rw_a0[0]), a2p, rw_g2[0].astype(BF16), row(rw_kk[0]),
        row(rw_ka[0]), row(rw_rk[0]), bd, e2 * tri, e2)
    o = _rwkv_scan(rp, kp, bp, ap, v, ke, be, pc)
    y_r = _rwkv_post(o, bonus, g, bd * (1.0 / HEAD_DIM), row(rw_lnx_g[0]), row(rw_lnx_b[0]))

    pad_lanes = lambda t: jnp.pad(t, (0, LANES - t.shape[0])).reshape(1, LANES)
    cf, sf, cp, sp = _rope_tables(Lp)
    qt, k, vt, qit, kx, wt = _dsa_prep(
        u_a, row(att_qnorm_g[0]), att_wuq[0].astype(BF16), idx_wq[0].astype(BF16),
        pad_lanes(idx_knorm_g[0]), pad_lanes(idx_knorm_b[0]), cf, sf, cp, sp)
    y_a = _dsa_attn(qt, k, vt, qit, kx, wt, ksel)

    wr = jnp.concatenate([jnp.pad(rt_grp_w[0], ((0, 0), (0, LANES - N_GROUPS))),
                          jnp.pad(rt_exp_w[0], ((0, 0), (0, LANES - N_EXPERTS)))], axis=1)
    br = jnp.concatenate([pad_lanes(rt_grp_b[0]), pad_lanes(rt_exp_b[0])], axis=1)
    h1, gates = _outproj_router(x, y_r, y_a, row(ln_emb_g), row(ln_emb_b), w_out[0].astype(BF16),
                                row(ln1_g[0]), row(ln1_b[0]), wr, br)
    out = _moe(h1.reshape(B * S, D), gates.reshape(B * S, LANES), ex_w_gate[0].astype(BF16),
               ex_w_up[0].astype(BF16), ex_w_down[0].astype(BF16), row(ln2_g[0]), row(ln2_b[0]))
    return out.reshape(B, S, D)
```

```python
import functools

import jax
import jax.numpy as jnp
from jax import lax
from jax.experimental import pallas as pl
from jax.experimental.pallas import tpu as pltpu

F32 = jnp.float32
BF16 = jnp.bfloat16
I32 = jnp.int32
HIGHEST = lax.Precision.HIGHEST

D_MODEL = 1024
N_META = 16
RWKV_WIDTH = 512
ATT_WIDTH = 512
HEAD_DIM = 64
N_HEADS = 8
DECAY_RANK = 64
ICLR_RANK = 64
GATE_RANK = 128
Q_LORA_RANK = 256
IDX_HEADS = 8
IDX_DIM = 64
IDX_ROPE_DIM = 32
INDEX_TOPK = 256
ROPE_THETA = 10000.0
N_GROUPS = 4
EXPERTS_PER_GROUP = 8
N_EXPERTS = N_GROUPS * EXPERTS_PER_GROUP
D_EXPERT = 256
DN_ALPHA = 2.0 ** 0.25
LN_EPS = 1e-5
RMS_EPS = 1e-6
GN_EPS = 64e-5
RWKV_COLS = 3 * RWKV_WIDTH + DECAY_RANK + ICLR_RANK + GATE_RANK
ATT_COLS = Q_LORA_RANK + 2 * ATT_WIDTH + IDX_DIM + IDX_HEADS
ATT_COLS_PAD = 1408

LANES = 128
PAD_ROWS = 256
OFF = PAD_ROWS - N_META
CHUNK = 64
ROWS_A = 256
ROWS_R = 256
TQ = 128
KC = 256
TM = 1024
EXPERT_LANE0 = 64
NEG = -1e30
MASKED = -3e38
BELOW_ALL = -1e38
ABOVE_ALL = 3e38
MAX_REFINE = 400
INT_MIN = -2147483648
VMEM_LIMIT = 56 * 1024 * 1024


def _mm(a, b, precision=None):
    return jnp.dot(a, b, preferred_element_type=F32, precision=precision)


def _mm_nt(a, b, precision=None):
    return lax.dot_general(a, b, (((1,), (1,)), ((), ())), preferred_element_type=F32, precision=precision)


def _mm_tn(a, b, precision=None):
    return lax.dot_general(a, b, (((0,), (0,)), ((), ())), preferred_element_type=F32, precision=precision)


def _sigmoid(x):
    return 1.0 / (1.0 + jnp.exp(-x))


def _layer_norm(x, g, b):
    mu = jnp.mean(x, -1, keepdims=True)
    xc = x - mu
    var = jnp.mean(xc * xc, -1, keepdims=True)
    return xc * lax.rsqrt(var + LN_EPS) * g + b


def _params(*sem):
    return pltpu.CompilerParams(dimension_semantics=sem, vmem_limit_bytes=VMEM_LIMIT)


def _const_spec(shape):
    nd = len(shape)
    return pl.BlockSpec(shape, lambda *_: (0,) * nd)


def _ln_inproj_body(x_ref, meta_ref, g_ref, b_ref, w_ref, ur_ref, ua_ref):
    blk = pl.program_id(1)
    xin = jnp.where(blk == 0, meta_ref[...], x_ref[0])
    h = _layer_norm(xin, g_ref[...], b_ref[...])
    row = lax.broadcasted_iota(I32, (ROWS_A, 1), 0)
    h = jnp.where((blk > 0) | (row >= OFF), h, 0.0)
    hb = h.astype(BF16)
    step = 256
    for n0 in range(0, RWKV_COLS, step):
        n1 = min(n0 + step, RWKV_COLS)
        ur_ref[0, :, n0:n1] = _mm(hb, w_ref[:, n0:n1])
    for n0 in range(0, ATT_COLS_PAD, step):
        n1 = min(n0 + step, ATT_COLS_PAD)
        ua_ref[0, :, n0:n1] = _mm(hb, w_ref[:, RWKV_COLS + n0:RWKV_COLS + n1])


def _ln_inproj(x, meta_pad, g, b, w):
    B, S, D = x.shape
    nblk = (S + PAD_ROWS) // ROWS_A
    Lp = S + PAD_ROWS
    ncols = RWKV_COLS + ATT_COLS_PAD
    return pl.pallas_call(
        _ln_inproj_body,
        grid=(B, nblk),
        in_specs=[
            pl.BlockSpec((1, ROWS_A, D), lambda b_, i: (b_, jnp.maximum(i - 1, 0), 0)),
            _const_spec((ROWS_A, D)),
            _const_spec((1, D)),
            _const_spec((1, D)),
            _const_spec((D, ncols)),
        ],
        out_specs=[
            pl.BlockSpec((1, ROWS_A, RWKV_COLS), lambda b_, i: (b_, i, 0)),
            pl.BlockSpec((1, ROWS_A, ATT_COLS_PAD), lambda b_, i: (b_, i, 0)),
        ],
        out_shape=[
            jax.ShapeDtypeStruct((B, Lp, RWKV_COLS), F32),
            jax.ShapeDtypeStruct((B, Lp, ATT_COLS_PAD), F32),
        ],
        compiler_params=_params("parallel", "arbitrary"),
        name="ln_inproj",
    )(x, meta_pad, g, b, w)


def _split3(x):
    hi = x.astype(BF16)
    r1 = x - hi.astype(F32)
    mid = r1.astype(BF16)
    return hi, mid, (r1 - mid.astype(F32)).astype(BF16)


def _mm_exact_rhs(x, m):
    return sum(_mm(p, m) for p in _split3(x))


def _mm_exact_lhs(m, x):
    return sum(_mm(m, p) for p in _split3(x))


def _rwkv_prep_body(u_ref, prev_ref, mu_ref, w0_ref, w2_ref, a0_ref, a2_ref, g2_ref, kk_ref, ka_ref, rk_ref,
                    hs_ref, hb_ref, tri_ref,
                    rp_ref, kp_ref, bp_ref, ap_ref, v_ref, pc_ref, g_ref, bonus_ref):
    blk = pl.program_id(1)
    u = u_ref[0]
    prev = jnp.where(blk == 0, 0.0, prev_ref[0][7:8, :])
    row = lax.broadcasted_iota(I32, (ROWS_R, 1), 0)
    shifted = jnp.where(row == 0, prev, pltpu.roll(u, 1, 0))
    ul = u + (shifted - u) * mu_ref[...]
    W = RWKV_WIDTH
    r = ul[:, 0:W]
    k = ul[:, W:2 * W]
    v = ul[:, 2 * W:3 * W]
    wa = ul[:, 3 * W:3 * W + 128]
    gd = ul[:, 3 * W + 128:3 * W + 256]
    w = w0_ref[...] + _mm(jnp.tanh(wa).astype(BF16), w2_ref[...])
    softplus_neg_w = jnp.maximum(-w, 0.0) + jnp.log(1.0 + jnp.exp(-jnp.abs(w)))
    logd = -jnp.exp(-softplus_neg_w - 0.5)
    a = _sigmoid(a0_ref[...] + _mm(wa.astype(BF16), a2_ref[...]))
    g_ref[0] = _mm(_sigmoid(gd).astype(BF16), g2_ref[...])
    head_sum = lambda t: _mm_exact_rhs(_mm_exact_rhs(t, hs_ref[...]), hb_ref[...])
    kkr = k * kk_ref[...]
    kk = kkr / jnp.maximum(jnp.sqrt(head_sum(kkr * kkr)), 1e-12)
    kmod = k * (1.0 + (a - 1.0) * ka_ref[...])
    bonus_ref[0] = head_sum(r * kmod * rk_ref[...]) * v
    v_ref[0] = v
    cum = _mm_exact_lhs(tri_ref[...], logd)
    rp_ref[0] = r * jnp.exp(cum)
    einv = jnp.exp(-cum)
    kp_ref[0] = kmod * einv
    bp_ref[0] = kk * a * einv
    ap_ref[0] = -kk * jnp.exp(cum - logd)
    for c in range(ROWS_R // CHUNK):
        last = c * CHUNK + CHUNK - 1
        pc_ref[0, c] = jnp.broadcast_to(jnp.exp(cum[last:last + 1]), (8, W))


def _rwkv_prep(u_r, mu, w0, w2p, a0, a2p, g2, k_k, k_a, r_k, hs, hb, tri):
    B, Lp, _ = u_r.shape
    nblk = Lp // ROWS_R
    W = RWKV_WIDTH
    row_spec = pl.BlockSpec((1, ROWS_R, W), lambda b_, i: (b_, i, 0))
    row_shape = jax.ShapeDtypeStruct((B, Lp, W), F32)
    cpb = ROWS_R // CHUNK
    return pl.pallas_call(
        _rwkv_prep_body,
        grid=(B, nblk),
        in_specs=[
            pl.BlockSpec((1, ROWS_R, RWKV_COLS), lambda b_, i: (b_, i, 0)),
            pl.BlockSpec((1, 8, RWKV_COLS), lambda b_, i: (b_, jnp.maximum(i * (ROWS_R // 8) - 1, 0), 0)),
            _const_spec((1, RWKV_COLS)),
            _const_spec((1, W)),
            _const_spec((128, W)),
            _const_spec((1, W)),
            _const_spec((128, W)),
            _const_spec((128, W)),
            _const_spec((1, W)),
            _const_spec((1, W)),
            _const_spec((1, W)),
            _const_spec((W, LANES)),
            _const_spec((LANES, W)),
            _const_spec((ROWS_R, ROWS_R)),
        ],
        out_specs=[row_spec] * 5 + [pl.BlockSpec((1, cpb, 8, W), lambda b_, i: (b_, i, 0, 0))] + [row_spec] * 2,
        out_shape=[row_shape] * 5 + [jax.ShapeDtypeStruct((B, Lp // CHUNK, 8, W), F32)] + [row_shape] * 2,
        compiler_params=_params("parallel", "arbitrary"),
        name="rwkv_prep",
    )(u_r, u_r, mu, w0, w2p, a0, a2p, g2, k_k, k_a, r_k, hs, hb, tri)


def _rwkv_scan_body(rp_ref, kp_ref, bp_ref, ap_ref, v_ref, pc_ref, g_ref, bonus_ref, lg_ref, lb_ref, o_ref, s_ref):
    c = pl.program_id(1)

    @pl.when(c == 0)
    def _():
        s_ref[...] = jnp.zeros_like(s_ref)

    @pl.when(c < OFF // CHUNK)
    def _():
        o_ref[...] = jnp.zeros_like(o_ref)

    @pl.when(c >= OFF // CHUNK)
    def _():
        _rwkv_chunk(rp_ref, kp_ref, bp_ref, ap_ref, v_ref, pc_ref, g_ref, bonus_ref, lg_ref, lb_ref, o_ref, s_ref)


def _split(x):
    hi = x.astype(BF16)
    return hi, (x - hi.astype(F32)).astype(BF16)


def _dot3(a, b, dims):
    dg = lambda p, q: lax.dot_general(p, q, (dims, ((), ())), preferred_element_type=F32)
    return dg(a[0], b[0]) + dg(a[0], b[1]) + dg(a[1], b[0])


_NN = ((1,), (0,))
_NT = ((1,), (1,))
_TN = ((0,), (0,))


def _rwkv_chunk(rp_ref, kp_ref, bp_ref, ap_ref, v_ref, pc_ref, g_ref, bonus_ref, lg_ref, lb_ref, o_ref, s_ref):
    ri = lax.broadcasted_iota(I32, (CHUNK, CHUNK), 0)
    ci = lax.broadcasted_iota(I32, (CHUNK, CHUNK), 1)
    strict = ri > ci
    incl = ri >= ci
    eye = jnp.where(ri == ci, 1.0, 0.0)
    heads = range(N_HEADS)
    sls = [slice(h * HEAD_DIM, (h + 1) * HEAD_DIM) for h in heads]
    load = lambda ref: [_split(ref[0, :, sl]) for sl in sls]
    a_, b_, k_, r_, v_ = (load(ref) for ref in (ap_ref, bp_ref, kp_ref, rp_ref, v_ref))
    pcs = [pc_ref[0, 0, 0:1, sl] for sl in sls]
    ke = [_split(kp_ref[0, :, sls[h]] * pcs[h]) for h in heads]
    be = [_split(bp_ref[0, :, sls[h]] * pcs[h]) for h in heads]
    s0 = [s_ref[h] for h in heads]
    s0s = [_split(s) for s in s0]
    a_ab = [jnp.where(strict, _dot3(a_[h], b_[h], _NT), 0.0) for h in heads]
    a_ak = [_split(jnp.where(strict, _dot3(a_[h], k_[h], _NT), 0.0)) for h in heads]
    a_rk = [_split(jnp.where(incl, _dot3(r_[h], k_[h], _NT), 0.0)) for h in heads]
    a_rb = [_split(jnp.where(incl, _dot3(r_[h], b_[h], _NT), 0.0)) for h in heads]
    t = [eye + a_ab[h] for h in heads]
    ps = [_split(a_ab[h]) for h in heads]
    for _ in range(CHUNK.bit_length() - 2):
        ps = [_split(_dot3(ps[h], ps[h], _NN)) for h in heads]
        t = [t[h] + _dot3(_split(t[h]), ps[h], _NN) for h in heads]
    x = [_split(_dot3(a_[h], s0s[h], _NT) + _dot3(a_ak[h], v_[h], _NN)) for h in heads]
    u_ = [_split(_dot3(_split(t[h]), x[h], _NN)) for h in heads]
    for h in heads:
        o = _dot3(r_[h], s0s[h], _NT) + _dot3(a_rk[h], v_[h], _NN) + _dot3(a_rb[h], u_[h], _NN)
        oc = o - jnp.mean(o, axis=1, keepdims=True)
        var = jnp.mean(oc * oc, axis=1, keepdims=True)
        y = oc * lax.rsqrt(var + GN_EPS) * lg_ref[:, sls[h]] + lb_ref[:, sls[h]]
        o_ref[0, :, sls[h]] = (y + bonus_ref[0, :, sls[h]]) * g_ref[0, :, sls[h]]
    for h in heads:
        s_ref[h] = s0[h] * pcs[h] + _dot3(v_[h], ke[h], _TN) + _dot3(u_[h], be[h], _TN)


def _rwkv_scan(rp, kp, bp, ap, v, pc, g, bonus, lg, lb):
    B, Lp, W = rp.shape
    nch = Lp // CHUNK
    row_spec = pl.BlockSpec((1, CHUNK, W), lambda b_, c: (b_, c, 0))
    return pl.pallas_call(
        _rwkv_scan_body,
        grid=(B, nch),
        in_specs=[row_spec] * 5 + [pl.BlockSpec((1, 1, 8, W), lambda b_, c: (b_, c, 0, 0))] + [row_spec] * 2
        + [_const_spec((1, W))] * 2,
        out_specs=row_spec,
        out_shape=jax.ShapeDtypeStruct((B, Lp, W), F32),
        scratch_shapes=[pltpu.VMEM((N_HEADS, HEAD_DIM, HEAD_DIM), F32)],
        compiler_params=_params("parallel", "arbitrary"),
        name="rwkv_scan",
    )(rp, kp, bp, ap, v, pc, g, bonus, lg, lb)


def _rope(x, cos, sin, half, first):
    width = x.shape[1]
    rot = jnp.where(first, pltpu.roll(x, width - half, 1), pltpu.roll(x, half, 1))
    return x * cos + rot * sin


def _dsa_prep_body(u_ref, qg_ref, wuq_ref, wiq_ref, kng_ref, knb_ref, cf_ref, sf_ref, cp_ref, sp_ref,
                   qt_ref, k_ref, vt_ref, qit_ref, kx_ref, wt_ref):
    u = u_ref[0]
    cq = u[:, 0:Q_LORA_RANK]
    k = u[:, Q_LORA_RANK:Q_LORA_RANK + ATT_WIDTH]
    v = u[:, Q_LORA_RANK + ATT_WIDTH:Q_LORA_RANK + 2 * ATT_WIDTH]
    tail = u[:, Q_LORA_RANK + 2 * ATT_WIDTH:]
    cqn = (cq * lax.rsqrt(jnp.mean(cq * cq, -1, keepdims=True) + RMS_EPS) * qg_ref[...]).astype(BF16)
    reps = ATT_WIDTH // LANES
    cf = jnp.concatenate([cf_ref[...]] * reps, axis=1)
    sf = jnp.concatenate([sf_ref[...]] * reps, axis=1)
    cp = jnp.concatenate([cp_ref[...]] * reps, axis=1)
    sp = jnp.concatenate([sp_ref[...]] * reps, axis=1)
    lane_w = lax.broadcasted_iota(I32, (1, ATT_WIDTH), 1) % HEAD_DIM
    first_f = lane_w < HEAD_DIM // 2
    first_p = lane_w < IDX_ROPE_DIM // 2
    q = _rope(_mm(cqn, wuq_ref[...]), cf, sf, HEAD_DIM // 2, first_f)
    qt_ref[0] = (q * (HEAD_DIM ** -0.5)).T.astype(BF16)
    qi = _rope(_mm(cqn, wiq_ref[...]), cp, sp, IDX_ROPE_DIM // 2, first_p)
    qit_ref[0] = qi.T.astype(BF16)
    k_ref[0] = _rope(k, cf, sf, HEAD_DIM // 2, first_f).astype(BF16)
    vt_ref[0] = v.T.astype(BF16)
    lane = lax.broadcasted_iota(I32, (1, LANES), 1)
    is_key = lane < IDX_DIM
    mu = jnp.sum(jnp.where(is_key, tail, 0.0), -1, keepdims=True) * (1.0 / IDX_DIM)
    tc = jnp.where(is_key, tail - mu, 0.0)
    var = jnp.sum(tc * tc, -1, keepdims=True) * (1.0 / IDX_DIM)
    kn = tc * lax.rsqrt(var + LN_EPS) * kng_ref[...] + knb_ref[...]
    kn = _rope(kn, cp_ref[...], sp_ref[...], IDX_ROPE_DIM // 2, (lane % HEAD_DIM) < IDX_ROPE_DIM // 2)
    kn = kn * (IDX_DIM ** -0.5)
    kx_ref[0] = jnp.where(is_key, kn, pltpu.roll(kn, IDX_DIM, 1)).astype(BF16)
    wt_ref[0] = (tail * (IDX_HEADS ** -0.5)).T


def _dsa_prep(u_a, qg, wuq, wiq, kng, knb, cf, sf, cp, sp):
    B, Lp, _ = u_a.shape
    W = ATT_WIDTH
    row = lambda width: pl.BlockSpec((1, ROWS_A, width), lambda b_, i: (b_, i, 0))
    col = lambda height: pl.BlockSpec((1, height, ROWS_A), lambda b_, i: (b_, 0, i))
    tab = pl.BlockSpec((ROWS_A, LANES), lambda b_, i: (i, 0))
    return pl.pallas_call(
        _dsa_prep_body,
        grid=(B, Lp // ROWS_A),
        in_specs=[row(ATT_COLS_PAD), _const_spec((1, Q_LORA_RANK)), _const_spec((Q_LORA_RANK, W)),
                  _const_spec((Q_LORA_RANK, W)), _const_spec((1, LANES)), _const_spec((1, LANES)),
                  tab, tab, tab, tab],
        out_specs=[col(W), row(W), col(W), col(W), row(LANES), col(LANES)],
        out_shape=[jax.ShapeDtypeStruct((B, W, Lp), BF16), jax.ShapeDtypeStruct((B, Lp, W), BF16),
                   jax.ShapeDtypeStruct((B, W, Lp), BF16), jax.ShapeDtypeStruct((B, W, Lp), BF16),
                   jax.ShapeDtypeStruct((B, Lp, LANES), BF16), jax.ShapeDtypeStruct((B, LANES, Lp), F32)],
        compiler_params=_params("parallel", "arbitrary"),
        name="dsa_prep",
    )(u_a, qg, wuq, wiq, kng, knb, cf, sf, cp, sp)


def _fold_rows(x, op=jnp.add):
    parts = [x[8 * r:8 * r + 8] for r in range(x.shape[0] // 8)]
    while len(parts) > 1:
        parts = [op(a, b) for a, b in zip(parts[0::2], parts[1::2])] + parts[len(parts) & ~1:]
    return parts[0]


def _dsa_attn_body(qt_ref, qit_ref, wt_ref, k_ref, vt_ref, kx_ref, o_ref, key_ref, sc_ref, acc_ref, j_ref, s_ref,
                   pe_ref, *, ksel):
    i = pl.program_id(1)
    nkc = (i * TQ + TQ - 1) // KC + 1
    kf = float(ksel)
    tcol = i * TQ + lax.broadcasted_iota(I32, (1, TQ), 1)
    int_min = jnp.int32(INT_MIN)
    row_in_pair = lax.broadcasted_iota(I32, (LANES, 1), 0)

    def head_operands(ref):
        out = []
        for h in range(N_HEADS):
            pair = ref[0, LANES * (h // 2):LANES * (h // 2 + 1), :]
            keep = (row_in_pair < HEAD_DIM) if h % 2 == 0 else (row_in_pair >= HEAD_DIM)
            out.append(jnp.where(keep, pair, jnp.zeros_like(pair)))
        return out

    def key_rows(ks):
        return ks + lax.broadcasted_iota(I32, (KC, 1), 0)

    qis = head_operands(qit_ref)
    wrows = [wt_ref[0, IDX_DIM + h:IDX_DIM + h + 1, :] for h in range(IDX_HEADS)]

    def score_chunk(kc, carry):
        lo8, hi8 = carry
        ks = pl.multiple_of(kc * KC, KC)
        kx = kx_ref[0, pl.ds(ks, KC), :]
        sc = jnp.zeros((KC, TQ), F32)
        for h in range(IDX_HEADS):
            sc = sc + jnp.maximum(_mm(kx, qis[h]), 0.0) * wrows[h]
        sc = sc + 0.0
        krow = key_rows(ks)
        sc = jnp.where(krow >= OFF, sc, MASKED)
        sc = jnp.where(krow <= tcol, sc, MASKED)
        sc_ref[pl.ds(ks, KC), :] = sc
        bits = lax.bitcast_convert_type(sc, I32)
        key_ref[pl.ds(ks, KC), :] = jnp.where(bits >= 0, bits, bits ^ jnp.int32(0x7FFFFFFF))
        lo8 = jnp.minimum(lo8, _fold_rows(jnp.where(sc <= MASKED, ABOVE_ALL, sc), jnp.minimum))
        hi8 = jnp.maximum(hi8, _fold_rows(sc, jnp.maximum))
        return lo8, hi8

    lo8, hi8 = lax.fori_loop(0, nkc, score_chunk,
                             (jnp.full((8, TQ), ABOVE_ALL, F32), jnp.full((8, TQ), MASKED, F32)))
    smin = jnp.min(lo8, axis=0, keepdims=True)
    smax = jnp.max(hi8, axis=0, keepdims=True)

    def scan_chunks(fn, init):
        def body(kc, carry):
            ks = pl.multiple_of(kc * KC, KC)
            return fn(carry, ks)
        return lax.fori_loop(0, nkc, body, init)

    zeros8 = jnp.zeros((8, TQ), F32)

    def count_where(ref, pred):
        cnt = scan_chunks(lambda c, ks: c + _fold_rows(pred(ref[pl.ds(ks, KC), :], ks)), zeros8)
        return jnp.sum(cnt, axis=0, keepdims=True)

    def value_bit(bi, prefix):
        cand = prefix | lax.shift_left(jnp.int32(1), 31 - bi)
        cand_s = cand ^ int_min
        cnt = count_where(key_ref, lambda key, ks: jnp.where(key >= cand_s, 1.0, 0.0))
        return jnp.where(cnt >= kf, cand, prefix)

    thr_key = lax.fori_loop(0, 32, value_bit, jnp.zeros((1, TQ), I32)) ^ int_min
    cand0 = lax.bitcast_convert_type(jnp.where(thr_key >= 0, thr_key, thr_key ^ jnp.int32(0x7FFFFFFF)), F32)

    n_adm = jnp.maximum(tcol - (OFF - 1), 0).astype(F32)
    searching = n_adm > kf

    def probe(mid):
        def fn(carry, ks):
            cnt, vmin = carry
            s = sc_ref[pl.ds(ks, KC), :]
            ge = s >= mid
            return (cnt + _fold_rows(jnp.where(ge, 1.0, 0.0)),
                    jnp.minimum(vmin, _fold_rows(jnp.where(ge, s, ABOVE_ALL), jnp.minimum)))
        cnt, vmin = scan_chunks(fn, (zeros8, jnp.full((8, TQ), ABOVE_ALL, F32)))
        return jnp.sum(cnt, axis=0, keepdims=True), jnp.min(vmin, axis=0, keepdims=True)

    def refine(state):
        it, lo, hi, c_lo, c_gt, done, _ = state
        mid = jnp.where(it == 0, jnp.where(searching, cand0, lo), lo + 0.5 * (hi - lo))
        c_mid, v_mid = probe(mid)
        up = c_mid >= kf
        lo_n = jnp.where(up, v_mid, lo)
        hi_n = jnp.where(up, hi, mid)
        c_lo_n = jnp.where(up, c_mid, c_lo)
        c_gt_n = count_where(sc_ref, lambda s, ks: jnp.where(s > lo_n, 1.0, 0.0))
        stalled = jnp.where(it > 0, jnp.where(mid <= lo, 1.0, jnp.where(mid >= hi, 1.0, 0.0)), 0.0)
        fin = jnp.maximum(jnp.where(c_gt_n < kf, 1.0, 0.0), stalled)
        frozen = done > 0.0
        keep = lambda old, new_: jnp.where(frozen, old, new_)
        done_n = jnp.maximum(done, fin)
        return (it + 1, keep(lo, lo_n), keep(hi, hi_n), keep(c_lo, c_lo_n), keep(c_gt, c_gt_n), done_n,
                jnp.max(1.0 - done_n))

    done0 = jnp.where(searching, 0.0, 1.0)
    state0 = (jnp.int32(0), smin, smax + (jnp.abs(smax) + 1.0) * 1e-6, n_adm, n_adm, done0, jnp.max(1.0 - done0))
    state = lax.while_loop(lambda st: jnp.logical_and(st[6] > 0.0, st[0] < MAX_REFINE), refine, state0)
    thr = jnp.where(searching, state[1], BELOW_ALL)
    cnt_gt = jnp.where(searching, state[4], n_adm)
    cnt_eq = jnp.where(searching, state[3] - state[4], 0.0)
    need = kf - cnt_gt

    j_ref[...] = jnp.full(j_ref.shape, 2 ** 30, I32)

    @pl.when(jnp.max(cnt_eq - need) > 0.0)
    def _():
        def index_bit(bi, prefix):
            cand = prefix | lax.shift_left(jnp.int32(1), 12 - bi)
            before = count_where(
                sc_ref, lambda s, ks: jnp.where(s == thr, jnp.where(key_rows(ks) < cand, 1.0, 0.0), 0.0))
            return jnp.where(before < need, cand, prefix)
        jst = lax.fori_loop(0, 13, index_bit, jnp.zeros((1, TQ), I32))
        j_ref[...] = jnp.broadcast_to(jst, j_ref.shape)

    jstar = j_ref[0:1, :]

    qs = head_operands(qt_ref)
    acc_ref[...] = jnp.zeros_like(acc_ref)

    def attend_chunk(kc, carry):
        ms, ls = carry
        ks = pl.multiple_of(kc * KC, KC)
        sc = sc_ref[pl.ds(ks, KC), :]
        tie = jnp.where(sc == thr, jnp.where(key_rows(ks) <= jstar, 0.0, NEG), NEG)
        bias = jnp.where(sc > thr, 0.0, tie)
        chunk_max = []
        for h in range(N_HEADS):
            p = h // 2
            kp = k_ref[0, pl.ds(ks, KC), LANES * p:LANES * (p + 1)]
            s = _mm(kp, qs[h]) + bias
            s_ref[h] = s
            chunk_max.append(jnp.max(s, axis=0, keepdims=True))
        new_ms, new_ls, alphas = [], [], []
        for h in range(N_HEADS):
            m_new = jnp.maximum(ms[h], chunk_max[h])
            alphas.append(jnp.exp(ms[h] - m_new))
            new_ms.append(m_new)
            pe_ref[h] = jnp.exp(s_ref[h] - m_new).astype(BF16)
        ones_rows = jnp.ones((16, KC), BF16)
        for h in range(N_HEADS):
            vt = vt_ref[0, HEAD_DIM * h:HEAD_DIM * (h + 1), pl.ds(ks, KC)]
            pv = _mm(jnp.concatenate([vt, ones_rows], axis=0), pe_ref[h])
            rows = slice(HEAD_DIM * h, HEAD_DIM * (h + 1))
            acc_ref[rows, :] = acc_ref[rows, :] * alphas[h] + pv[0:HEAD_DIM]
            new_ls.append(alphas[h] * ls[h] + pv[HEAD_DIM:HEAD_DIM + 1])
        return tuple(new_ms), tuple(new_ls)

    init = (tuple(jnp.full((1, TQ), NEG, F32) for _ in range(N_HEADS)),
            tuple(jnp.zeros((1, TQ), F32) for _ in range(N_HEADS)))
    _, ls = lax.fori_loop(0, nkc, attend_chunk, init)
    for p in range(N_HEADS // 2):
        parts = [acc_ref[HEAD_DIM * h:HEAD_DIM * (h + 1), :] / ls[h] for h in (2 * p, 2 * p + 1)]
        o_ref[0, :, LANES * p:LANES * (p + 1)] = jnp.concatenate(parts, axis=0).T


def _dsa_attn(qt, k, vt, qit, kx, wt, ksel):
    B, Lp, W = k.shape
    assert Lp % KC == 0 and Lp <= 8192
    qcol = lambda height: pl.BlockSpec((1, height, TQ), lambda b_, i: (b_, 0, i))
    full = lambda shape: pl.BlockSpec((1,) + shape, lambda b_, i: (b_, 0, 0))
    return pl.pallas_call(
        functools.partial(_dsa_attn_body, ksel=ksel),
        grid=(B, Lp // TQ),
        in_specs=[qcol(W), qcol(W), qcol(LANES), full((Lp, W)), full((W, Lp)), full((Lp, LANES))],
        out_specs=pl.BlockSpec((1, TQ, W), lambda b_, i: (b_, i, 0)),
        out_shape=jax.ShapeDtypeStruct((B, Lp, W), F32),
        scratch_shapes=[
            pltpu.VMEM((Lp, TQ), I32),
            pltpu.VMEM((Lp, TQ), F32),
            pltpu.VMEM((W, TQ), F32),
            pltpu.VMEM((8, TQ), I32),
            pltpu.VMEM((N_HEADS, KC, TQ), F32),
            pltpu.VMEM((N_HEADS, KC, TQ), BF16),
        ],
        compiler_params=_params("parallel", "arbitrary"),
        name="dsa_attn",
    )(qt, qit, wt, k, vt, kx)


def _outproj_router_body(x_ref, yr_ref, ya_ref, eg_ref, eb_ref, wo_ref, g1_ref, b1_ref, wrh_ref, wrl_ref, br_ref,
                         h_ref, gates_ref):
    h0 = _layer_norm(x_ref[0], eg_ref[...], eb_ref[...])
    mix = (_mm(yr_ref[0].astype(BF16), wo_ref[0:RWKV_WIDTH, :])
           + _mm(ya_ref[0].astype(BF16), wo_ref[RWKV_WIDTH:, :]))
    h1 = _layer_norm(DN_ALPHA * h0 + mix, g1_ref[...], b1_ref[...])
    h_ref[0] = h1
    logits = _dot3(_split(h1), (wrh_ref[...], wrl_ref[...]), _NN) + br_ref[...]
    lane = lax.broadcasted_iota(I32, (1, LANES), 1)
    lanef = lane.astype(F32)
    low = -3e38
    lgm = jnp.where(lane < N_GROUPS, logits, low)
    gmax = jnp.max(lgm, axis=1, keepdims=True)
    gsel = jnp.min(jnp.where(lgm == gmax, lanef, 1e9), axis=1, keepdims=True)
    gsum = jnp.sum(jnp.where(lane < N_GROUPS, jnp.exp(lgm - gmax), 0.0), axis=1, keepdims=True)
    group_of_lane = ((lane - EXPERT_LANE0) // EXPERTS_PER_GROUP).astype(F32)
    lem = jnp.where(group_of_lane == gsel, logits, low)
    m1 = jnp.max(lem, axis=1, keepdims=True)
    i1 = jnp.min(jnp.where(lem == m1, lanef, 1e9), axis=1, keepdims=True)
    lem2 = jnp.where(lanef == i1, low, lem)
    m2 = jnp.max(lem2, axis=1, keepdims=True)
    i2 = jnp.min(jnp.where(lem2 == m2, lanef, 1e9), axis=1, keepdims=True)
    e2 = jnp.exp(m2 - m1)
    w1 = 1.0 / (1.0 + e2)
    w2 = e2 / (1.0 + e2)
    gates = jnp.where(lanef == i1, w1, jnp.where(lanef == i2, w2, 0.0)) / gsum
    gates_ref[0] = gates


def _outproj_router(x, y_r, y_a, eg, eb, wo, g1, b1, wrh, wrl, br):
    B, S, D = x.shape
    skip = PAD_ROWS // ROWS_A
    xrow = pl.BlockSpec((1, ROWS_A, D), lambda b_, i: (b_, i, 0))
    yrow = pl.BlockSpec((1, ROWS_A, RWKV_WIDTH), lambda b_, i: (b_, i + skip, 0))
    vec = _const_spec((1, D))
    return pl.pallas_call(
        _outproj_router_body,
        grid=(B, S // ROWS_A),
        in_specs=[xrow, yrow, yrow, vec, vec, _const_spec((D, D)), vec, vec,
                  _const_spec((D, LANES)), _const_spec((D, LANES)), _const_spec((1, LANES))],
        out_specs=[xrow, pl.BlockSpec((1, ROWS_A, LANES), lambda b_, i: (b_, i, 0))],
        out_shape=[jax.ShapeDtypeStruct((B, S, D), F32), jax.ShapeDtypeStruct((B, S, LANES), F32)],
        compiler_params=_params("parallel", "arbitrary"),
        name="outproj_router",
    )(x, y_r, y_a, eg, eb, wo, g1, b1, wrh, wrl, br)


def _moe_body(h_ref, gates_ref, wg_ref, wu_ref, wd_ref, g2_ref, b2_ref, o_ref, acc_ref, hb_ref):
    e = pl.program_id(1)

    @pl.when(e == 0)
    def _():
        acc_ref[...] = jnp.zeros_like(acc_ref)
        hb_ref[...] = h_ref[...].astype(BF16)

    t = hb_ref[...]
    lane = lax.broadcasted_iota(I32, (1, LANES), 1)
    gcol = jnp.sum(jnp.where(lane == e + EXPERT_LANE0, gates_ref[...], 0.0), axis=1, keepdims=True)
    a = _mm(t, wg_ref[0])
    hid = a * _sigmoid(a) * _mm(t, wu_ref[0]) * gcol
    acc_ref[...] += _mm(hid.astype(BF16), wd_ref[0])

    @pl.when(e == N_EXPERTS - 1)
    def _():
        o_ref[...] = _layer_norm(DN_ALPHA * h_ref[...] + acc_ref[...], g2_ref[...], b2_ref[...])


def _moe(h1, gates, wg, wu, wd, g2, b2):
    T, D = h1.shape
    tile = pl.BlockSpec((TM, D), lambda i, e: (i, 0))
    return pl.pallas_call(
        _moe_body,
        grid=(T // TM, N_EXPERTS),
        in_specs=[tile, pl.BlockSpec((TM, LANES), lambda i, e: (i, 0)),
                  pl.BlockSpec((1, D, D_EXPERT), lambda i, e: (e, 0, 0)),
                  pl.BlockSpec((1, D, D_EXPERT), lambda i, e: (e, 0, 0)),
                  pl.BlockSpec((1, D_EXPERT, D), lambda i, e: (e, 0, 0)),
                  _const_spec((1, D)), _const_spec((1, D))],
        out_specs=tile,
        out_shape=jax.ShapeDtypeStruct((T, D), F32),
        scratch_shapes=[pltpu.VMEM((TM, D), F32), pltpu.VMEM((TM, D), BF16)],
        compiler_params=_params("parallel", "arbitrary"),
        name="moe",
    )(h1, gates, wg, wu, wd, g2, b2)


def _rope_tables(Lp):
    pos = jnp.maximum(jnp.arange(Lp, dtype=I32) - OFF, 0).astype(F32)
    j = jnp.arange(LANES) % HEAD_DIM

    def table(half, rot_dim):
        inv = 1.0 / (ROPE_THETA ** (jnp.arange(half, dtype=F32) / half))
        ang = pos[:, None] * inv[None, :]
        cos, sin = jnp.cos(ang)[:, j % half], jnp.sin(ang)[:, j % half]
        rotated = (j < rot_dim)[None, :]
        sign = jnp.where(j < half, -1.0, 1.0)[None, :]
        return jnp.where(rotated, cos, 1.0), jnp.where(rotated, sin * sign, 0.0)

    cf, sf = table(HEAD_DIM // 2, HEAD_DIM)
    cp, sp = table(IDX_ROPE_DIM // 2, IDX_ROPE_DIM)
    return cf, sf, cp, sp


def _block_ones(n, block):
    idx = jnp.arange(n) // block
    return (idx[:, None] == idx[None, :]).astype(F32)


def kernel(x, meta_tokens, ln_emb_g, ln_emb_b, w_in, rw_mu, rw_w0, rw_w2, rw_a0, rw_a2, rw_g2, rw_kk, rw_ka,
           rw_rk, rw_lnx_g, rw_lnx_b, att_qnorm_g, att_wuq, idx_wq, idx_knorm_g, idx_knorm_b, w_out, ln1_g,
           ln1_b, rt_grp_w, rt_grp_b, rt_exp_w, rt_exp_b, ex_w_gate, ex_w_up, ex_w_down, ln2_g, ln2_b):
    B, S, D = x.shape
    assert w_in.shape[0] == 1 and D == D_MODEL and S % TM == 0
    Lp = S + PAD_ROWS
    ksel = min(INDEX_TOPK, S // 4)
    row = lambda t: t.reshape(1, -1)
    W = RWKV_WIDTH

    meta_pad = jnp.zeros((PAD_ROWS, D), F32).at[OFF:].set(meta_tokens)
    w_in_p = jnp.pad(w_in[0], ((0, 0), (0, ATT_COLS_PAD - ATT_COLS))).astype(BF16)
    u_r, u_a = _ln_inproj(x, meta_pad, row(ln_emb_g), row(ln_emb_b), w_in_p)

    w2p = jnp.concatenate([rw_w2[0], jnp.zeros((ICLR_RANK, W), F32)], 0).astype(BF16)
    a2p = jnp.concatenate([jnp.zeros((DECAY_RANK, W), F32), rw_a2[0]], 0).astype(BF16)
    head_of_lane = jnp.arange(W) // HEAD_DIM
    hs = (head_of_lane[:, None] == jnp.arange(LANES)[None, :]).astype(BF16)
    tri = (jnp.arange(ROWS_R)[:, None] >= jnp.arange(ROWS_R)[None, :]).astype(F32)
    tri = (tri * _block_ones(ROWS_R, CHUNK)).astype(BF16)
    rp, kp, bp, ap, v, pc, g, bonus = _rwkv_prep(
        u_r, row(rw_mu[0]), row(rw_w0[0]), w2p, row(rw_a0[0]), a2p, rw_g2[0].astype(BF16), row(rw_kk[0]),
        row(rw_ka[0]), row(rw_rk[0]), hs, hs.T, tri)
    y_r = _rwkv_scan(rp, kp, bp, ap, v, pc, g, bonus, row(rw_lnx_g[0]), row(rw_lnx_b[0]))

    pad_lanes = lambda t: jnp.pad(t, (0, LANES - t.shape[0])).reshape(1, LANES)
    cf, sf, cp, sp = _rope_tables(Lp)
    qt, k, vt, qit, kx, wt = _dsa_prep(
        u_a, row(att_qnorm_g[0]), att_wuq[0].astype(BF16), idx_wq[0].astype(BF16),
        pad_lanes(idx_knorm_g[0]), pad_lanes(idx_knorm_b[0]), cf, sf, cp, sp)
    y_a = _dsa_attn(qt, k, vt, qit, kx, wt, ksel)

    wr = jnp.zeros((D, LANES), F32).at[:, :N_GROUPS].set(rt_grp_w[0])
    wr = wr.at[:, EXPERT_LANE0:EXPERT_LANE0 + N_EXPERTS].set(rt_exp_w[0])
    br = jnp.zeros((1, LANES), F32).at[0, :N_GROUPS].set(rt_grp_b[0])
    br = br.at[0, EXPERT_LANE0:EXPERT_LANE0 + N_EXPERTS].set(rt_exp_b[0])
    wrh = wr.astype(BF16)
    wrl = (wr - wrh.astype(F32)).astype(BF16)
    h1, gates = _outproj_router(x, y_r, y_a, row(ln_emb_g), row(ln_emb_b), w_out[0].astype(BF16),
                                row(ln1_g[0]), row(ln1_b[0]), wrh, wrl, br)
    out = _moe(h1.reshape(B * S, D), gates.reshape(B * S, LANES), ex_w_gate[0].astype(BF16),
               ex_w_up[0].astype(BF16), ex_w_down[0].astype(BF16), row(ln2_g[0]), row(ln2_b[0]))
    return out.reshape(B, S, D)
```

```python
import functools

import jax
import jax.numpy as jnp
from jax import lax
from jax.experimental import pallas as pl
from jax.experimental.pallas import tpu as pltpu

F32 = jnp.float32
BF16 = jnp.bfloat16
I32 = jnp.int32
I16 = jnp.int16
HIGHEST = lax.Precision.HIGHEST

D_MODEL = 1024
N_META = 16
RWKV_WIDTH = 512
ATT_WIDTH = 512
HEAD_DIM = 64
N_HEADS = 8
DECAY_RANK = 64
ICLR_RANK = 64
GATE_RANK = 128
Q_LORA_RANK = 256
IDX_HEADS = 8
IDX_DIM = 64
IDX_ROPE_DIM = 32
INDEX_TOPK = 256
ROPE_THETA = 10000.0
N_GROUPS = 4
EXPERTS_PER_GROUP = 8
N_EXPERTS = N_GROUPS * EXPERTS_PER_GROUP
D_EXPERT = 256
DN_ALPHA = 2.0 ** 0.25
LN_EPS = 1e-5
RMS_EPS = 1e-6
GN_EPS = 64e-5
RWKV_COLS = 3 * RWKV_WIDTH + DECAY_RANK + ICLR_RANK + GATE_RANK
ATT_COLS = Q_LORA_RANK + 2 * ATT_WIDTH + IDX_DIM + IDX_HEADS
ATT_COLS_PAD = 1408

LANES = 128
PAD_ROWS = 256
OFF = PAD_ROWS - N_META
CHUNK = 64
ROWS_A = 256
ROWS_R = 256
TQ = 256
KC = 256
TM = 1024
EXPERT_LANE0 = 64
NEG = -1e30
MASKED = -3e38
BELOW_ALL = -1e38
ABOVE_ALL = 3e38
MAX_REFINE = 400
INT_MIN = -2147483648
HALF16 = 32768
VMEM_LIMIT = 56 * 1024 * 1024


def _mm(a, b, precision=None):
    return jnp.dot(a, b, preferred_element_type=F32, precision=precision)


def _mm_nt(a, b, precision=None):
    return lax.dot_general(a, b, (((1,), (1,)), ((), ())), preferred_element_type=F32, precision=precision)


def _mm_tn(a, b, precision=None):
    return lax.dot_general(a, b, (((0,), (0,)), ((), ())), preferred_element_type=F32, precision=precision)


def _sigmoid(x):
    return 1.0 / (1.0 + jnp.exp(-x))


def _layer_norm(x, g, b):
    mu = jnp.mean(x, -1, keepdims=True)
    xc = x - mu
    var = jnp.mean(xc * xc, -1, keepdims=True)
    return xc * lax.rsqrt(var + LN_EPS) * g + b


def _params(*sem):
    return pltpu.CompilerParams(dimension_semantics=sem, vmem_limit_bytes=VMEM_LIMIT)


def _const_spec(shape):
    nd = len(shape)
    return pl.BlockSpec(shape, lambda *_: (0,) * nd)


def _ln_inproj_body(x_ref, meta_ref, g_ref, b_ref, w_ref, ur_ref, ua_ref):
    blk = pl.program_id(1)
    xin = jnp.where(blk == 0, meta_ref[...], x_ref[0])
    h = _layer_norm(xin, g_ref[...], b_ref[...])
    row = lax.broadcasted_iota(I32, (ROWS_A, 1), 0)
    h = jnp.where((blk > 0) | (row >= OFF), h, 0.0)
    hb = h.astype(BF16)
    step = 256
    for n0 in range(0, RWKV_COLS, step):
        n1 = min(n0 + step, RWKV_COLS)
        ur_ref[0, :, n0:n1] = _mm(hb, w_ref[:, n0:n1])
    for n0 in range(0, ATT_COLS_PAD, step):
        n1 = min(n0 + step, ATT_COLS_PAD)
        ua_ref[0, :, n0:n1] = _mm(hb, w_ref[:, RWKV_COLS + n0:RWKV_COLS + n1])


def _ln_inproj(x, meta_pad, g, b, w):
    B, S, D = x.shape
    nblk = (S + PAD_ROWS) // ROWS_A
    Lp = S + PAD_ROWS
    ncols = RWKV_COLS + ATT_COLS_PAD
    return pl.pallas_call(
        _ln_inproj_body,
        grid=(B, nblk),
        in_specs=[
            pl.BlockSpec((1, ROWS_A, D), lambda b_, i: (b_, jnp.maximum(i - 1, 0), 0)),
            _const_spec((ROWS_A, D)),
            _const_spec((1, D)),
            _const_spec((1, D)),
            _const_spec((D, ncols)),
        ],
        out_specs=[
            pl.BlockSpec((1, ROWS_A, RWKV_COLS), lambda b_, i: (b_, i, 0)),
            pl.BlockSpec((1, ROWS_A, ATT_COLS_PAD), lambda b_, i: (b_, i, 0)),
        ],
        out_shape=[
            jax.ShapeDtypeStruct((B, Lp, RWKV_COLS), F32),
            jax.ShapeDtypeStruct((B, Lp, ATT_COLS_PAD), F32),
        ],
        compiler_params=_params("parallel", "arbitrary"),
        name="ln_inproj",
    )(x, meta_pad, g, b, w)


def _split3(x):
    hi = x.astype(BF16)
    r1 = x - hi.astype(F32)
    mid = r1.astype(BF16)
    return hi, mid, (r1 - mid.astype(F32)).astype(BF16)


def _mm_exact_rhs(x, m):
    return sum(_mm(p, m) for p in _split3(x))


def _mm_exact_lhs(m, x):
    return sum(_mm(m, p) for p in _split3(x))


def _rwkv_prep_body(u_ref, prev_ref, mu_ref, w0_ref, w2_ref, a0_ref, a2_ref, g2_ref, kk_ref, ka_ref, rk_ref,
                    hs_ref, hb_ref, tri_ref,
                    rp_ref, kp_ref, bp_ref, ap_ref, v_ref, pc_ref, g_ref, bonus_ref):
    blk = pl.program_id(1)
    u = u_ref[0]
    prev = jnp.where(blk == 0, 0.0, prev_ref[0][7:8, :])
    row = lax.broadcasted_iota(I32, (ROWS_R, 1), 0)
    shifted = jnp.where(row == 0, prev, pltpu.roll(u, 1, 0))
    ul = u + (shifted - u) * mu_ref[...]
    W = RWKV_WIDTH
    r = ul[:, 0:W]
    k = ul[:, W:2 * W]
    v = ul[:, 2 * W:3 * W]
    wa = ul[:, 3 * W:3 * W + 128]
    gd = ul[:, 3 * W + 128:3 * W + 256]
    w = w0_ref[...] + _mm(jnp.tanh(wa).astype(BF16), w2_ref[...])
    softplus_neg_w = jnp.maximum(-w, 0.0) + jnp.log(1.0 + jnp.exp(-jnp.abs(w)))
    logd = -jnp.exp(-softplus_neg_w - 0.5)
    a = _sigmoid(a0_ref[...] + _mm(wa.astype(BF16), a2_ref[...]))
    g_ref[0] = _mm(_sigmoid(gd).astype(BF16), g2_ref[...])
    head_sum = lambda t: _mm_exact_rhs(_mm_exact_rhs(t, hs_ref[...]), hb_ref[...])
    kkr = k * kk_ref[...]
    kk = kkr / jnp.maximum(jnp.sqrt(head_sum(kkr * kkr)), 1e-12)
    kmod = k * (1.0 + (a - 1.0) * ka_ref[...])
    bonus_ref[0] = head_sum(r * kmod * rk_ref[...]) * v
    v_ref[0] = v
    cum = _mm_exact_lhs(tri_ref[...], logd)
    rp_ref[0] = r * jnp.exp(cum)
    einv = jnp.exp(-cum)
    kp_ref[0] = kmod * einv
    bp_ref[0] = kk * a * einv
    ap_ref[0] = -kk * jnp.exp(cum - logd)
    for c in range(ROWS_R // CHUNK):
        last = c * CHUNK + CHUNK - 1
        pc_ref[0, c] = jnp.broadcast_to(jnp.exp(cum[last:last + 1]), (8, W))


def _rwkv_prep(u_r, mu, w0, w2p, a0, a2p, g2, k_k, k_a, r_k, hs, hb, tri):
    B, Lp, _ = u_r.shape
    nblk = Lp // ROWS_R
    W = RWKV_WIDTH
    row_spec = pl.BlockSpec((1, ROWS_R, W), lambda b_, i: (b_, i, 0))
    row_shape = jax.ShapeDtypeStruct((B, Lp, W), F32)
    cpb = ROWS_R // CHUNK
    return pl.pallas_call(
        _rwkv_prep_body,
        grid=(B, nblk),
        in_specs=[
            pl.BlockSpec((1, ROWS_R, RWKV_COLS), lambda b_, i: (b_, i, 0)),
            pl.BlockSpec((1, 8, RWKV_COLS), lambda b_, i: (b_, jnp.maximum(i * (ROWS_R // 8) - 1, 0), 0)),
            _const_spec((1, RWKV_COLS)),
            _const_spec((1, W)),
            _const_spec((128, W)),
            _const_spec((1, W)),
            _const_spec((128, W)),
            _const_spec((128, W)),
            _const_spec((1, W)),
            _const_spec((1, W)),
            _const_spec((1, W)),
            _const_spec((W, LANES)),
            _const_spec((LANES, W)),
            _const_spec((ROWS_R, ROWS_R)),
        ],
        out_specs=[row_spec] * 5 + [pl.BlockSpec((1, cpb, 8, W), lambda b_, i: (b_, i, 0, 0))] + [row_spec] * 2,
        out_shape=[row_shape] * 5 + [jax.ShapeDtypeStruct((B, Lp // CHUNK, 8, W), F32)] + [row_shape] * 2,
        compiler_params=_params("parallel", "arbitrary"),
        name="rwkv_prep",
    )(u_r, u_r, mu, w0, w2p, a0, a2p, g2, k_k, k_a, r_k, hs, hb, tri)


def _rwkv_scan_body(rp_ref, kp_ref, bp_ref, ap_ref, v_ref, pc_ref, g_ref, bonus_ref, lg_ref, lb_ref, o_ref, s_ref):
    c = pl.program_id(1)

    @pl.when(c == 0)
    def _():
        s_ref[...] = jnp.zeros_like(s_ref)

    @pl.when(c < OFF // CHUNK)
    def _():
        o_ref[...] = jnp.zeros_like(o_ref)

    @pl.when(c >= OFF // CHUNK)
    def _():
        _rwkv_chunk(rp_ref, kp_ref, bp_ref, ap_ref, v_ref, pc_ref, g_ref, bonus_ref, lg_ref, lb_ref, o_ref, s_ref)


def _split(x):
    hi = x.astype(BF16)
    return hi, (x - hi.astype(F32)).astype(BF16)


def _dot3(a, b, dims):
    dg = lambda p, q: lax.dot_general(p, q, (dims, ((), ())), preferred_element_type=F32)
    return dg(a[0], b[0]) + dg(a[0], b[1]) + dg(a[1], b[0])


_NN = ((1,), (0,))
_NT = ((1,), (1,))
_TN = ((0,), (0,))


def _rwkv_chunk(rp_ref, kp_ref, bp_ref, ap_ref, v_ref, pc_ref, g_ref, bonus_ref, lg_ref, lb_ref, o_ref, s_ref):
    ri = lax.broadcasted_iota(I32, (CHUNK, CHUNK), 0)
    ci = lax.broadcasted_iota(I32, (CHUNK, CHUNK), 1)
    strict = ri > ci
    incl = ri >= ci
    eye = jnp.where(ri == ci, 1.0, 0.0)
    heads = range(N_HEADS)
    sls = [slice(h * HEAD_DIM, (h + 1) * HEAD_DIM) for h in heads]
    load = lambda ref: [_split(ref[0, :, sl]) for sl in sls]
    a_, b_, k_, r_, v_ = (load(ref) for ref in (ap_ref, bp_ref, kp_ref, rp_ref, v_ref))
    pcs = [pc_ref[0, 0, 0:1, sl] for sl in sls]
    ke = [_split(kp_ref[0, :, sls[h]] * pcs[h]) for h in heads]
    be = [_split(bp_ref[0, :, sls[h]] * pcs[h]) for h in heads]
    s0 = [s_ref[h] for h in heads]
    s0s = [_split(s) for s in s0]
    a_ab = [jnp.where(strict, _dot3(a_[h], b_[h], _NT), 0.0) for h in heads]
    a_ak = [_split(jnp.where(strict, _dot3(a_[h], k_[h], _NT), 0.0)) for h in heads]
    a_rk = [_split(jnp.where(incl, _dot3(r_[h], k_[h], _NT), 0.0)) for h in heads]
    a_rb = [_split(jnp.where(incl, _dot3(r_[h], b_[h], _NT), 0.0)) for h in heads]
    t = [eye + a_ab[h] for h in heads]
    ps = [_split(a_ab[h]) for h in heads]
    for _ in range(CHUNK.bit_length() - 2):
        ps = [_split(_dot3(ps[h], ps[h], _NN)) for h in heads]
        t = [t[h] + _dot3(_split(t[h]), ps[h], _NN) for h in heads]
    x = [_split(_dot3(a_[h], s0s[h], _NT) + _dot3(a_ak[h], v_[h], _NN)) for h in heads]
    u_ = [_split(_dot3(_split(t[h]), x[h], _NN)) for h in heads]
    for h in heads:
        o = _dot3(r_[h], s0s[h], _NT) + _dot3(a_rk[h], v_[h], _NN) + _dot3(a_rb[h], u_[h], _NN)
        oc = o - jnp.mean(o, axis=1, keepdims=True)
        var = jnp.mean(oc * oc, axis=1, keepdims=True)
        y = oc * lax.rsqrt(var + GN_EPS) * lg_ref[:, sls[h]] + lb_ref[:, sls[h]]
        o_ref[0, :, sls[h]] = (y + bonus_ref[0, :, sls[h]]) * g_ref[0, :, sls[h]]
    for h in heads:
        s_ref[h] = s0[h] * pcs[h] + _dot3(v_[h], ke[h], _TN) + _dot3(u_[h], be[h], _TN)


def _rwkv_scan(rp, kp, bp, ap, v, pc, g, bonus, lg, lb):
    B, Lp, W = rp.shape
    nch = Lp // CHUNK
    row_spec = pl.BlockSpec((1, CHUNK, W), lambda b_, c: (b_, c, 0))
    return pl.pallas_call(
        _rwkv_scan_body,
        grid=(B, nch),
        in_specs=[row_spec] * 5 + [pl.BlockSpec((1, 1, 8, W), lambda b_, c: (b_, c, 0, 0))] + [row_spec] * 2
        + [_const_spec((1, W))] * 2,
        out_specs=row_spec,
        out_shape=jax.ShapeDtypeStruct((B, Lp, W), F32),
        scratch_shapes=[pltpu.VMEM((N_HEADS, HEAD_DIM, HEAD_DIM), F32)],
        compiler_params=_params("parallel", "arbitrary"),
        name="rwkv_scan",
    )(rp, kp, bp, ap, v, pc, g, bonus, lg, lb)


def _rope(x, cos, sin, half, first):
    width = x.shape[1]
    rot = jnp.where(first, pltpu.roll(x, width - half, 1), pltpu.roll(x, half, 1))
    return x * cos + rot * sin


def _dsa_prep_body(u_ref, qg_ref, wuq_ref, wiq_ref, kng_ref, knb_ref, cf_ref, sf_ref, cp_ref, sp_ref,
                   qt_ref, k_ref, vt_ref, qit_ref, kx_ref, wt_ref):
    u = u_ref[0]
    cq = u[:, 0:Q_LORA_RANK]
    k = u[:, Q_LORA_RANK:Q_LORA_RANK + ATT_WIDTH]
    v = u[:, Q_LORA_RANK + ATT_WIDTH:Q_LORA_RANK + 2 * ATT_WIDTH]
    tail = u[:, Q_LORA_RANK + 2 * ATT_WIDTH:]
    cqn = (cq * lax.rsqrt(jnp.mean(cq * cq, -1, keepdims=True) + RMS_EPS) * qg_ref[...]).astype(BF16)
    reps = ATT_WIDTH // LANES
    cf = jnp.concatenate([cf_ref[...]] * reps, axis=1)
    sf = jnp.concatenate([sf_ref[...]] * reps, axis=1)
    cp = jnp.concatenate([cp_ref[...]] * reps, axis=1)
    sp = jnp.concatenate([sp_ref[...]] * reps, axis=1)
    lane_w = lax.broadcasted_iota(I32, (1, ATT_WIDTH), 1) % HEAD_DIM
    first_f = lane_w < HEAD_DIM // 2
    first_p = lane_w < IDX_ROPE_DIM // 2
    q = _rope(_mm(cqn, wuq_ref[...]), cf, sf, HEAD_DIM // 2, first_f)
    qt_ref[0] = (q * (HEAD_DIM ** -0.5)).T.astype(BF16)
    qi = _rope(_mm(cqn, wiq_ref[...]), cp, sp, IDX_ROPE_DIM // 2, first_p)
    qit_ref[0] = qi.T.astype(BF16)
    k_ref[0] = _rope(k, cf, sf, HEAD_DIM // 2, first_f).astype(BF16)
    vt_ref[0] = v.T.astype(BF16)
    lane = lax.broadcasted_iota(I32, (1, LANES), 1)
    is_key = lane < IDX_DIM
    mu = jnp.sum(jnp.where(is_key, tail, 0.0), -1, keepdims=True) * (1.0 / IDX_DIM)
    tc = jnp.where(is_key, tail - mu, 0.0)
    var = jnp.sum(tc * tc, -1, keepdims=True) * (1.0 / IDX_DIM)
    kn = tc * lax.rsqrt(var + LN_EPS) * kng_ref[...] + knb_ref[...]
    kn = _rope(kn, cp_ref[...], sp_ref[...], IDX_ROPE_DIM // 2, (lane % HEAD_DIM) < IDX_ROPE_DIM // 2)
    kn = kn * (IDX_DIM ** -0.5)
    kx_ref[0] = jnp.where(is_key, kn, pltpu.roll(kn, IDX_DIM, 1)).astype(BF16)
    wt_ref[0] = (tail * (IDX_HEADS ** -0.5)).T


def _dsa_prep(u_a, qg, wuq, wiq, kng, knb, cf, sf, cp, sp):
    B, Lp, _ = u_a.shape
    W = ATT_WIDTH
    row = lambda width: pl.BlockSpec((1, ROWS_A, width), lambda b_, i: (b_, i, 0))
    col = lambda height: pl.BlockSpec((1, height, ROWS_A), lambda b_, i: (b_, 0, i))
    tab = pl.BlockSpec((ROWS_A, LANES), lambda b_, i: (i, 0))
    return pl.pallas_call(
        _dsa_prep_body,
        grid=(B, Lp // ROWS_A),
        in_specs=[row(ATT_COLS_PAD), _const_spec((1, Q_LORA_RANK)), _const_spec((Q_LORA_RANK, W)),
                  _const_spec((Q_LORA_RANK, W)), _const_spec((1, LANES)), _const_spec((1, LANES)),
                  tab, tab, tab, tab],
        out_specs=[col(W), row(W), col(W), col(W), row(LANES), col(LANES)],
        out_shape=[jax.ShapeDtypeStruct((B, W, Lp), BF16), jax.ShapeDtypeStruct((B, Lp, W), BF16),
                   jax.ShapeDtypeStruct((B, W, Lp), BF16), jax.ShapeDtypeStruct((B, W, Lp), BF16),
                   jax.ShapeDtypeStruct((B, Lp, LANES), BF16), jax.ShapeDtypeStruct((B, LANES, Lp), F32)],
        compiler_params=_params("parallel", "arbitrary"),
        name="dsa_prep",
    )(u_a, qg, wuq, wiq, kng, knb, cf, sf, cp, sp)


def _fold_rows(x, op=jnp.add):
    parts = [x[8 * r:8 * r + 8] for r in range(x.shape[0] // 8)]
    while len(parts) > 1:
        parts = [op(a, b) for a, b in zip(parts[0::2], parts[1::2])] + parts[len(parts) & ~1:]
    return parts[0]


def _dsa_attn_body(qt_ref, qit_ref, wt_ref, k_ref, vt_ref, kx_ref, o_ref, khi_ref, klo_ref, sc_ref, acc_ref,
                   j_ref, s_ref, pe_ref, *, ksel):
    i = pl.program_id(1)
    nkc = (i * TQ + TQ - 1) // KC + 1
    kf = float(ksel)
    tcol = i * TQ + lax.broadcasted_iota(I32, (1, TQ), 1)
    int_min = jnp.int32(INT_MIN)
    row_in_pair = lax.broadcasted_iota(I32, (LANES, 1), 0)

    def head_operands(ref):
        out = []
        for h in range(N_HEADS):
            pair = ref[0, LANES * (h // 2):LANES * (h // 2 + 1), :]
            keep = (row_in_pair < HEAD_DIM) if h % 2 == 0 else (row_in_pair >= HEAD_DIM)
            out.append(jnp.where(keep, pair, jnp.zeros_like(pair)))
        return out

    def key_rows(ks):
        return ks + lax.broadcasted_iota(I32, (KC, 1), 0)

    qis = head_operands(qit_ref)
    wrows = [wt_ref[0, IDX_DIM + h:IDX_DIM + h + 1, :] for h in range(IDX_HEADS)]

    def score_chunk(kc, carry):
        lo8, hi8 = carry
        ks = pl.multiple_of(kc * KC, KC)
        kx = kx_ref[0, pl.ds(ks, KC), :]
        sc = jnp.zeros((KC, TQ), F32)
        for h in range(IDX_HEADS):
            sc = sc + jnp.maximum(_mm(kx, qis[h]), 0.0) * wrows[h]
        sc = sc + 0.0
        krow = key_rows(ks)
        sc = jnp.where(krow >= OFF, sc, MASKED)
        sc = jnp.where(krow <= tcol, sc, MASKED)
        sc_ref[pl.ds(ks, KC), :] = sc
        bits = lax.bitcast_convert_type(sc, I32)
        key = jnp.where(bits >= 0, bits, bits ^ jnp.int32(0x7FFFFFFF))
        khi_ref[pl.ds(ks, KC), :] = lax.shift_right_arithmetic(key, 16).astype(I16)
        klo_ref[pl.ds(ks, KC), :] = ((key & jnp.int32(0xFFFF)) - HALF16).astype(I16)
        lo8 = jnp.minimum(lo8, _fold_rows(jnp.where(sc <= MASKED, ABOVE_ALL, sc), jnp.minimum))
        hi8 = jnp.maximum(hi8, _fold_rows(sc, jnp.maximum))
        return lo8, hi8

    lo8, hi8 = lax.fori_loop(0, nkc, score_chunk,
                             (jnp.full((8, TQ), ABOVE_ALL, F32), jnp.full((8, TQ), MASKED, F32)))
    smin = jnp.min(lo8, axis=0, keepdims=True)
    smax = jnp.max(hi8, axis=0, keepdims=True)

    def scan_chunks(fn, init):
        def body(kc, carry):
            ks = pl.multiple_of(kc * KC, KC)
            return fn(carry, ks)
        return lax.fori_loop(0, nkc, body, init)

    zeros8 = jnp.zeros((8, TQ), F32)

    def count_where(ref, pred):
        cnt = scan_chunks(lambda c, ks: c + _fold_rows(pred(ref[pl.ds(ks, KC), :], ks)), zeros8)
        return jnp.sum(cnt, axis=0, keepdims=True)

    def count16(ref, pred):
        def fn(cnt, ks):
            m = pred(ref[pl.ds(ks, KC), :])
            parts = [m[16 * r:16 * r + 16] for r in range(KC // 16)]
            while len(parts) > 1:
                parts = [a + b for a, b in zip(parts[0::2], parts[1::2])]
            return cnt + parts[0]
        cnt = scan_chunks(fn, jnp.zeros((16, TQ), I16))
        return jnp.sum(cnt.astype(I32), axis=0, keepdims=True)

    one16, zero16 = jnp.int16(1), jnp.int16(0)

    def radix16(ref, target):
        def bit(bi, prefix):
            cand = prefix | lax.shift_left(jnp.int32(1), 15 - bi)
            cand16 = (cand - HALF16).astype(I16)
            cnt = count16(ref, lambda x: jnp.where(x >= cand16, one16, zero16))
            return jnp.where(cnt >= target, cand, prefix)
        return lax.fori_loop(0, 16, bit, jnp.zeros((1, TQ), I32))

    k_int = jnp.full((1, TQ), ksel, I32)
    thr_hi = radix16(khi_ref, k_int) - HALF16
    thr_hi16 = thr_hi.astype(I16)
    above = count16(khi_ref, lambda x: jnp.where(x > thr_hi16, one16, zero16))

    def keep_low_of_ties(carry, ks):
        rows = pl.ds(ks, KC)
        klo_ref[rows, :] = jnp.where(khi_ref[rows, :] == thr_hi16, klo_ref[rows, :], jnp.int16(-HALF16))
        return carry

    scan_chunks(keep_low_of_ties, 0)
    thr_lo = radix16(klo_ref, k_int - above)
    thr_key = lax.shift_left(thr_hi, 16) | thr_lo
    cand0 = lax.bitcast_convert_type(jnp.where(thr_key >= 0, thr_key, thr_key ^ jnp.int32(0x7FFFFFFF)), F32)

    n_adm = jnp.maximum(tcol - (OFF - 1), 0).astype(F32)
    searching = n_adm > kf

    def probe(mid):
        def fn(carry, ks):
            cnt, vmin = carry
            s = sc_ref[pl.ds(ks, KC), :]
            ge = s >= mid
            return (cnt + _fold_rows(jnp.where(ge, 1.0, 0.0)),
                    jnp.minimum(vmin, _fold_rows(jnp.where(ge, s, ABOVE_ALL), jnp.minimum)))
        cnt, vmin = scan_chunks(fn, (zeros8, jnp.full((8, TQ), ABOVE_ALL, F32)))
        return jnp.sum(cnt, axis=0, keepdims=True), jnp.min(vmin, axis=0, keepdims=True)

    def refine(state):
        it, lo, hi, c_lo, c_gt, done, _ = state
        mid = jnp.where(it == 0, jnp.where(searching, cand0, lo), lo + 0.5 * (hi - lo))
        c_mid, v_mid = probe(mid)
        up = c_mid >= kf
        lo_n = jnp.where(up, v_mid, lo)
        hi_n = jnp.where(up, hi, mid)
        c_lo_n = jnp.where(up, c_mid, c_lo)
        c_gt_n = count_where(sc_ref, lambda s, ks: jnp.where(s > lo_n, 1.0, 0.0))
        stalled = jnp.where(it > 0, jnp.where(mid <= lo, 1.0, jnp.where(mid >= hi, 1.0, 0.0)), 0.0)
        fin = jnp.maximum(jnp.where(c_gt_n < kf, 1.0, 0.0), stalled)
        frozen = done > 0.0
        keep = lambda old, new_: jnp.where(frozen, old, new_)
        done_n = jnp.maximum(done, fin)
        return (it + 1, keep(lo, lo_n), keep(hi, hi_n), keep(c_lo, c_lo_n), keep(c_gt, c_gt_n), done_n,
                jnp.max(1.0 - done_n))

    done0 = jnp.where(searching, 0.0, 1.0)
    state0 = (jnp.int32(0), smin, smax + (jnp.abs(smax) + 1.0) * 1e-6, n_adm, n_adm, done0, jnp.max(1.0 - done0))
    state = lax.while_loop(lambda st: jnp.logical_and(st[6] > 0.0, st[0] < MAX_REFINE), refine, state0)
    thr = jnp.where(searching, state[1], BELOW_ALL)
    cnt_gt = jnp.where(searching, state[4], n_adm)
    cnt_eq = jnp.where(searching, state[3] - state[4], 0.0)
    need = kf - cnt_gt

    j_ref[...] = jnp.full(j_ref.shape, 2 ** 30, I32)

    @pl.when(jnp.max(cnt_eq - need) > 0.0)
    def _():
        def index_bit(bi, prefix):
            cand = prefix | lax.shift_left(jnp.int32(1), 12 - bi)
            before = count_where(
                sc_ref, lambda s, ks: jnp.where(s == thr, jnp.where(key_rows(ks) < cand, 1.0, 0.0), 0.0))
            return jnp.where(before < need, cand, prefix)
        jst = lax.fori_loop(0, 13, index_bit, jnp.zeros((1, TQ), I32))
        j_ref[...] = jnp.broadcast_to(jst, j_ref.shape)

    jstar = j_ref[0:1, :]

    qs = head_operands(qt_ref)
    acc_ref[...] = jnp.zeros_like(acc_ref)

    ones_rows = jnp.ones((16, KC), BF16)

    def attend_chunk(kc, carry):
        ms, ls = carry
        ks = pl.multiple_of(kc * KC, KC)
        sc = sc_ref[pl.ds(ks, KC), :]
        tie = jnp.where(sc == thr, jnp.where(key_rows(ks) <= jstar, 0.0, NEG), NEG)
        bias = jnp.where(sc > thr, 0.0, tie)
        chunk_max = []
        for h in range(N_HEADS):
            p = h // 2
            kp = k_ref[0, pl.ds(ks, KC), LANES * p:LANES * (p + 1)]
            s = _mm(kp, qs[h]) + bias
            s_ref[h] = s
            chunk_max.append(jnp.max(s, axis=0, keepdims=True))
        new_ms, new_ls, alphas = [], [], []
        for h in range(N_HEADS):
            m_new = jnp.maximum(ms[h], chunk_max[h])
            alphas.append(jnp.exp(ms[h] - m_new))
            new_ms.append(m_new)
            pe_ref[h] = jnp.exp(s_ref[h] - m_new).astype(BF16)
        for h in range(N_HEADS):
            vt = vt_ref[0, HEAD_DIM * h:HEAD_DIM * (h + 1), pl.ds(ks, KC)]
            pv = _mm(jnp.concatenate([vt, ones_rows], axis=0), pe_ref[h])
            rows = slice(HEAD_DIM * h, HEAD_DIM * (h + 1))
            acc_ref[rows, :] = acc_ref[rows, :] * alphas[h] + pv[0:HEAD_DIM]
            new_ls.append(alphas[h] * ls[h] + pv[HEAD_DIM:HEAD_DIM + 1])
        return tuple(new_ms), tuple(new_ls)

    init = (tuple(jnp.full((1, TQ), NEG, F32) for _ in range(N_HEADS)),
            tuple(jnp.zeros((1, TQ), F32) for _ in range(N_HEADS)))
    _, ls = lax.fori_loop(0, nkc, attend_chunk, init)
    for p in range(N_HEADS // 2):
        parts = [acc_ref[HEAD_DIM * h:HEAD_DIM * (h + 1), :] / ls[h] for h in (2 * p, 2 * p + 1)]
        o_ref[0, :, LANES * p:LANES * (p + 1)] = jnp.concatenate(parts, axis=0).T


def _dsa_attn(qt, k, vt, qit, kx, wt, ksel):
    B, Lp, W = k.shape
    assert Lp % KC == 0 and Lp <= 8192
    qcol = lambda height: pl.BlockSpec((1, height, TQ), lambda b_, i: (b_, 0, i))
    full = lambda shape: pl.BlockSpec((1,) + shape, lambda b_, i: (b_, 0, 0))
    return pl.pallas_call(
        functools.partial(_dsa_attn_body, ksel=ksel),
        grid=(B, Lp // TQ),
        in_specs=[qcol(W), qcol(W), qcol(LANES), full((Lp, W)), full((W, Lp)), full((Lp, LANES))],
        out_specs=pl.BlockSpec((1, TQ, W), lambda b_, i: (b_, i, 0)),
        out_shape=jax.ShapeDtypeStruct((B, Lp, W), F32),
        scratch_shapes=[
            pltpu.VMEM((Lp, TQ), I16),
            pltpu.VMEM((Lp, TQ), I16),
            pltpu.VMEM((Lp, TQ), F32),
            pltpu.VMEM((W, TQ), F32),
            pltpu.VMEM((8, TQ), I32),
            pltpu.VMEM((N_HEADS, KC, TQ), F32),
            pltpu.VMEM((N_HEADS, KC, TQ), BF16),
        ],
        compiler_params=_params("parallel", "arbitrary"),
        name="dsa_attn",
    )(qt, qit, wt, k, vt, kx)


def _outproj_router_body(x_ref, yr_ref, ya_ref, eg_ref, eb_ref, wo_ref, g1_ref, b1_ref, wrh_ref, wrl_ref, br_ref,
                         h_ref, gates_ref):
    h0 = _layer_norm(x_ref[0], eg_ref[...], eb_ref[...])
    mix = (_mm(yr_ref[0].astype(BF16), wo_ref[0:RWKV_WIDTH, :])
           + _mm(ya_ref[0].astype(BF16), wo_ref[RWKV_WIDTH:, :]))
    h1 = _layer_norm(DN_ALPHA * h0 + mix, g1_ref[...], b1_ref[...])
    h_ref[0] = h1
    logits = _dot3(_split(h1), (wrh_ref[...], wrl_ref[...]), _NN) + br_ref[...]
    lane = lax.broadcasted_iota(I32, (1, LANES), 1)
    lanef = lane.astype(F32)
    low = -3e38
    lgm = jnp.where(lane < N_GROUPS, logits, low)
    gmax = jnp.max(lgm, axis=1, keepdims=True)
    gsel = jnp.min(jnp.where(lgm == gmax, lanef, 1e9), axis=1, keepdims=True)
    gsum = jnp.sum(jnp.where(lane < N_GROUPS, jnp.exp(lgm - gmax), 0.0), axis=1, keepdims=True)
    group_of_lane = ((lane - EXPERT_LANE0) // EXPERTS_PER_GROUP).astype(F32)
    lem = jnp.where(group_of_lane == gsel, logits, low)
    m1 = jnp.max(lem, axis=1, keepdims=True)
    i1 = jnp.min(jnp.where(lem == m1, lanef, 1e9), axis=1, keepdims=True)
    lem2 = jnp.where(lanef == i1, low, lem)
    m2 = jnp.max(lem2, axis=1, keepdims=True)
    i2 = jnp.min(jnp.where(lem2 == m2, lanef, 1e9), axis=1, keepdims=True)
    e2 = jnp.exp(m2 - m1)
    w1 = 1.0 / (1.0 + e2)
    w2 = e2 / (1.0 + e2)
    gates = jnp.where(lanef == i1, w1, jnp.where(lanef == i2, w2, 0.0)) / gsum
    gates_ref[0] = gates


def _outproj_router(x, y_r, y_a, eg, eb, wo, g1, b1, wrh, wrl, br):
    B, S, D = x.shape
    skip = PAD_ROWS // ROWS_A
    xrow = pl.BlockSpec((1, ROWS_A, D), lambda b_, i: (b_, i, 0))
    yrow = pl.BlockSpec((1, ROWS_A, RWKV_WIDTH), lambda b_, i: (b_, i + skip, 0))
    vec = _const_spec((1, D))
    return pl.pallas_call(
        _outproj_router_body,
        grid=(B, S // ROWS_A),
        in_specs=[xrow, yrow, yrow, vec, vec, _const_spec((D, D)), vec, vec,
                  _const_spec((D, LANES)), _const_spec((D, LANES)), _const_spec((1, LANES))],
        out_specs=[xrow, pl.BlockSpec((1, ROWS_A, LANES), lambda b_, i: (b_, i, 0))],
        out_shape=[jax.ShapeDtypeStruct((B, S, D), F32), jax.ShapeDtypeStruct((B, S, LANES), F32)],
        compiler_params=_params("parallel", "arbitrary"),
        name="outproj_router",
    )(x, y_r, y_a, eg, eb, wo, g1, b1, wrh, wrl, br)


def _moe_body(h_ref, gates_ref, wg_ref, wu_ref, wd_ref, g2_ref, b2_ref, o_ref, acc_ref, hb_ref):
    e = pl.program_id(1)

    @pl.when(e == 0)
    def _():
        acc_ref[...] = jnp.zeros_like(acc_ref)
        hb_ref[...] = h_ref[...].astype(BF16)

    t = hb_ref[...]
    lane = lax.broadcasted_iota(I32, (1, LANES), 1)
    gcol = jnp.sum(jnp.where(lane == e + EXPERT_LANE0, gates_ref[...], 0.0), axis=1, keepdims=True)
    a = _mm(t, wg_ref[0])
    hid = a * _sigmoid(a) * _mm(t, wu_ref[0]) * gcol
    acc_ref[...] += _mm(hid.astype(BF16), wd_ref[0])

    @pl.when(e == N_EXPERTS - 1)
    def _():
        o_ref[...] = _layer_norm(DN_ALPHA * h_ref[...] + acc_ref[...], g2_ref[...], b2_ref[...])


def _moe(h1, gates, wg, wu, wd, g2, b2):
    T, D = h1.shape
    tile = pl.BlockSpec((TM, D), lambda i, e: (i, 0))
    return pl.pallas_call(
        _moe_body,
        grid=(T // TM, N_EXPERTS),
        in_specs=[tile, pl.BlockSpec((TM, LANES), lambda i, e: (i, 0)),
                  pl.BlockSpec((1, D, D_EXPERT), lambda i, e: (e, 0, 0)),
                  pl.BlockSpec((1, D, D_EXPERT), lambda i, e: (e, 0, 0)),
                  pl.BlockSpec((1, D_EXPERT, D), lambda i, e: (e, 0, 0)),
                  _const_spec((1, D)), _const_spec((1, D))],
        out_specs=tile,
        out_shape=jax.ShapeDtypeStruct((T, D), F32),
        scratch_shapes=[pltpu.VMEM((TM, D), F32), pltpu.VMEM((TM, D), BF16)],
        compiler_params=_params("parallel", "arbitrary"),
        name="moe",
    )(h1, gates, wg, wu, wd, g2, b2)


def _rope_tables(Lp):
    pos = jnp.maximum(jnp.arange(Lp, dtype=I32) - OFF, 0).astype(F32)
    j = jnp.arange(LANES) % HEAD_DIM

    def table(half, rot_dim):
        inv = 1.0 / (ROPE_THETA ** (jnp.arange(half, dtype=F32) / half))
        ang = pos[:, None] * inv[None, :]
        cos, sin = jnp.cos(ang)[:, j % half], jnp.sin(ang)[:, j % half]
        rotated = (j < rot_dim)[None, :]
        sign = jnp.where(j < half, -1.0, 1.0)[None, :]
        return jnp.where(rotated, cos, 1.0), jnp.where(rotated, sin * sign, 0.0)

    cf, sf = table(HEAD_DIM // 2, HEAD_DIM)
    cp, sp = table(IDX_ROPE_DIM // 2, IDX_ROPE_DIM)
    return cf, sf, cp, sp


def _block_ones(n, block):
    idx = jnp.arange(n) // block
    return (idx[:, None] == idx[None, :]).astype(F32)


def kernel(x, meta_tokens, ln_emb_g, ln_emb_b, w_in, rw_mu, rw_w0, rw_w2, rw_a0, rw_a2, rw_g2, rw_kk, rw_ka,
           rw_rk, rw_lnx_g, rw_lnx_b, att_qnorm_g, att_wuq, idx_wq, idx_knorm_g, idx_knorm_b, w_out, ln1_g,
           ln1_b, rt_grp_w, rt_grp_b, rt_exp_w, rt_exp_b, ex_w_gate, ex_w_up, ex_w_down, ln2_g, ln2_b):
    B, S, D = x.shape
    assert w_in.shape[0] == 1 and D == D_MODEL and S % TM == 0
    Lp = S + PAD_ROWS
    ksel = min(INDEX_TOPK, S // 4)
    row = lambda t: t.reshape(1, -1)
    W = RWKV_WIDTH

    meta_pad = jnp.zeros((PAD_ROWS, D), F32).at[OFF:].set(meta_tokens)
    w_in_p = jnp.pad(w_in[0], ((0, 0), (0, ATT_COLS_PAD - ATT_COLS))).astype(BF16)
    u_r, u_a = _ln_inproj(x, meta_pad, row(ln_emb_g), row(ln_emb_b), w_in_p)

    w2p = jnp.concatenate([rw_w2[0], jnp.zeros((ICLR_RANK, W), F32)], 0).astype(BF16)
    a2p = jnp.concatenate([jnp.zeros((DECAY_RANK, W), F32), rw_a2[0]], 0).astype(BF16)
    head_of_lane = jnp.arange(W) // HEAD_DIM
    hs = (head_of_lane[:, None] == jnp.arange(LANES)[None, :]).astype(BF16)
    tri = (jnp.arange(ROWS_R)[:, None] >= jnp.arange(ROWS_R)[None, :]).astype(F32)
    tri = (tri * _block_ones(ROWS_R, CHUNK)).astype(BF16)
    rp, kp, bp, ap, v, pc, g, bonus = _rwkv_prep(
        u_r, row(rw_mu[0]), row(rw_w0[0]), w2p, row(rw_a0[0]), a2p, rw_g2[0].astype(BF16), row(rw_kk[0]),
        row(rw_ka[0]), row(rw_rk[0]), hs, hs.T, tri)
    y_r = _rwkv_scan(rp, kp, bp, ap, v, pc, g, bonus, row(rw_lnx_g[0]), row(rw_lnx_b[0]))

    pad_lanes = lambda t: jnp.pad(t, (0, LANES - t.shape[0])).reshape(1, LANES)
    cf, sf, cp, sp = _rope_tables(Lp)
    qt, k, vt, qit, kx, wt = _dsa_prep(
        u_a, row(att_qnorm_g[0]), att_wuq[0].astype(BF16), idx_wq[0].astype(BF16),
        pad_lanes(idx_knorm_g[0]), pad_lanes(idx_knorm_b[0]), cf, sf, cp, sp)
    y_a = _dsa_attn(qt, k, vt, qit, kx, wt, ksel)

    wr = jnp.zeros((D, LANES), F32).at[:, :N_GROUPS].set(rt_grp_w[0])
    wr = wr.at[:, EXPERT_LANE0:EXPERT_LANE0 + N_EXPERTS].set(rt_exp_w[0])
    br = jnp.zeros((1, LANES), F32).at[0, :N_GROUPS].set(rt_grp_b[0])
    br = br.at[0, EXPERT_LANE0:EXPERT_LANE0 + N_EXPERTS].set(rt_exp_b[0])
    wrh = wr.astype(BF16)
    wrl = (wr - wrh.astype(F32)).astype(BF16)
    h1, gates = _outproj_router(x, y_r, y_a, row(ln_emb_g), row(ln_emb_b), w_out[0].astype(BF16),
                                row(ln1_g[0]), row(ln1_b[0]), wrh, wrl, br)
    out = _moe(h1.reshape(B * S, D), gates.reshape(B * S, LANES), ex_w_gate[0].astype(BF16),
               ex_w_up[0].astype(BF16), ex_w_down[0].astype(BF16), row(ln2_g[0]), row(ln2_b[0]))
    return out.reshape(B, S, D)
```

```python
import functools

import jax
import jax.numpy as jnp
from jax import lax
from jax.experimental import pallas as pl
from jax.experimental.pallas import tpu as pltpu

F32 = jnp.float32
BF16 = jnp.bfloat16
I32 = jnp.int32
I16 = jnp.int16
HIGHEST = lax.Precision.HIGHEST

D_MODEL = 1024
N_META = 16
RWKV_WIDTH = 512
ATT_WIDTH = 512
HEAD_DIM = 64
N_HEADS = 8
DECAY_RANK = 64
ICLR_RANK = 64
GATE_RANK = 128
Q_LORA_RANK = 256
IDX_HEADS = 8
IDX_DIM = 64
IDX_ROPE_DIM = 32
INDEX_TOPK = 256
ROPE_THETA = 10000.0
N_GROUPS = 4
EXPERTS_PER_GROUP = 8
N_EXPERTS = N_GROUPS * EXPERTS_PER_GROUP
D_EXPERT = 256
DN_ALPHA = 2.0 ** 0.25
LN_EPS = 1e-5
RMS_EPS = 1e-6
GN_EPS = 64e-5
RWKV_COLS = 3 * RWKV_WIDTH + DECAY_RANK + ICLR_RANK + GATE_RANK
ATT_COLS = Q_LORA_RANK + 2 * ATT_WIDTH + IDX_DIM + IDX_HEADS
ATT_COLS_PAD = 1408

LANES = 128
PAD_ROWS = 256
OFF = PAD_ROWS - N_META
CHUNK = 64
ROWS_A = 256
ROWS_R = 256
TQ = 256
KC = 256
TM = 1024
SCAN_BATCH = 4
EXPERT_LANE0 = 64
NEG = -1e30
MASKED = -3e38
BELOW_ALL = -1e38
ABOVE_ALL = 3e38
MAX_REFINE = 400
INT_MIN = -2147483648
HALF16 = 32768
VMEM_LIMIT = 56 * 1024 * 1024


def _mm(a, b, precision=None):
    return jnp.dot(a, b, preferred_element_type=F32, precision=precision)


def _mm_nt(a, b, precision=None):
    return lax.dot_general(a, b, (((1,), (1,)), ((), ())), preferred_element_type=F32, precision=precision)


def _mm_tn(a, b, precision=None):
    return lax.dot_general(a, b, (((0,), (0,)), ((), ())), preferred_element_type=F32, precision=precision)


def _sigmoid(x):
    return 1.0 / (1.0 + jnp.exp(-x))


def _layer_norm(x, g, b):
    mu = jnp.mean(x, -1, keepdims=True)
    xc = x - mu
    var = jnp.mean(xc * xc, -1, keepdims=True)
    return xc * lax.rsqrt(var + LN_EPS) * g + b


def _params(*sem):
    return pltpu.CompilerParams(dimension_semantics=sem, vmem_limit_bytes=VMEM_LIMIT)


def _const_spec(shape):
    nd = len(shape)
    return pl.BlockSpec(shape, lambda *_: (0,) * nd)


def _ln_inproj_body(x_ref, meta_ref, g_ref, b_ref, w_ref, ur_ref, ua_ref):
    blk = pl.program_id(1)
    xin = jnp.where(blk == 0, meta_ref[...], x_ref[0])
    h = _layer_norm(xin, g_ref[...], b_ref[...])
    row = lax.broadcasted_iota(I32, (ROWS_A, 1), 0)
    h = jnp.where((blk > 0) | (row >= OFF), h, 0.0)
    hb = h.astype(BF16)
    step = 256
    for n0 in range(0, RWKV_COLS, step):
        n1 = min(n0 + step, RWKV_COLS)
        ur_ref[0, :, n0:n1] = _mm(hb, w_ref[:, n0:n1])
    for n0 in range(0, ATT_COLS_PAD, step):
        n1 = min(n0 + step, ATT_COLS_PAD)
        ua_ref[0, :, n0:n1] = _mm(hb, w_ref[:, RWKV_COLS + n0:RWKV_COLS + n1])


def _ln_inproj(x, meta_pad, g, b, w):
    B, S, D = x.shape
    nblk = (S + PAD_ROWS) // ROWS_A
    Lp = S + PAD_ROWS
    ncols = RWKV_COLS + ATT_COLS_PAD
    return pl.pallas_call(
        _ln_inproj_body,
        grid=(B, nblk),
        in_specs=[
            pl.BlockSpec((1, ROWS_A, D), lambda b_, i: (b_, jnp.maximum(i - 1, 0), 0)),
            _const_spec((ROWS_A, D)),
            _const_spec((1, D)),
            _const_spec((1, D)),
            _const_spec((D, ncols)),
        ],
        out_specs=[
            pl.BlockSpec((1, ROWS_A, RWKV_COLS), lambda b_, i: (b_, i, 0)),
            pl.BlockSpec((1, ROWS_A, ATT_COLS_PAD), lambda b_, i: (b_, i, 0)),
        ],
        out_shape=[
            jax.ShapeDtypeStruct((B, Lp, RWKV_COLS), F32),
            jax.ShapeDtypeStruct((B, Lp, ATT_COLS_PAD), F32),
        ],
        compiler_params=_params("parallel", "arbitrary"),
        name="ln_inproj",
    )(x, meta_pad, g, b, w)


def _split3(x):
    hi = x.astype(BF16)
    r1 = x - hi.astype(F32)
    mid = r1.astype(BF16)
    return hi, mid, (r1 - mid.astype(F32)).astype(BF16)


def _mm_exact_rhs(x, m):
    return sum(_mm(p, m) for p in _split3(x))


def _mm_exact_lhs(m, x):
    return sum(_mm(m, p) for p in _split3(x))


def _rwkv_prep_body(u_ref, prev_ref, mu_ref, w0_ref, w2_ref, a0_ref, a2_ref, g2_ref, kk_ref, ka_ref, rk_ref,
                    hs_ref, hb_ref, tri_ref,
                    rp_ref, kp_ref, bp_ref, ap_ref, v_ref, pc_ref, g_ref, bonus_ref):
    blk = pl.program_id(1)
    u = u_ref[0]
    prev = jnp.where(blk == 0, 0.0, prev_ref[0][7:8, :])
    row = lax.broadcasted_iota(I32, (ROWS_R, 1), 0)
    shifted = jnp.where(row == 0, prev, pltpu.roll(u, 1, 0))
    ul = u + (shifted - u) * mu_ref[...]
    W = RWKV_WIDTH
    r = ul[:, 0:W]
    k = ul[:, W:2 * W]
    v = ul[:, 2 * W:3 * W]
    wa = ul[:, 3 * W:3 * W + 128]
    gd = ul[:, 3 * W + 128:3 * W + 256]
    w = w0_ref[...] + _mm(jnp.tanh(wa).astype(BF16), w2_ref[...])
    softplus_neg_w = jnp.maximum(-w, 0.0) + jnp.log(1.0 + jnp.exp(-jnp.abs(w)))
    logd = -jnp.exp(-softplus_neg_w - 0.5)
    a = _sigmoid(a0_ref[...] + _mm(wa.astype(BF16), a2_ref[...]))
    g_ref[0] = _mm(_sigmoid(gd).astype(BF16), g2_ref[...])
    head_sum = lambda t: _mm_exact_rhs(_mm_exact_rhs(t, hs_ref[...]), hb_ref[...])
    kkr = k * kk_ref[...]
    kk = kkr / jnp.maximum(jnp.sqrt(head_sum(kkr * kkr)), 1e-12)
    kmod = k * (1.0 + (a - 1.0) * ka_ref[...])
    bonus_ref[0] = head_sum(r * kmod * rk_ref[...]) * v
    v_ref[0] = v
    cum = _mm_exact_lhs(tri_ref[...], logd)
    rp_ref[0] = r * jnp.exp(cum)
    einv = jnp.exp(-cum)
    kp_ref[0] = kmod * einv
    bp_ref[0] = kk * a * einv
    ap_ref[0] = -kk * jnp.exp(cum - logd)
    for c in range(ROWS_R // CHUNK):
        last = c * CHUNK + CHUNK - 1
        pc_ref[0, c] = jnp.broadcast_to(jnp.exp(cum[last:last + 1]), (8, W))


def _rwkv_prep(u_r, mu, w0, w2p, a0, a2p, g2, k_k, k_a, r_k, hs, hb, tri):
    B, Lp, _ = u_r.shape
    nblk = Lp // ROWS_R
    W = RWKV_WIDTH
    row_spec = pl.BlockSpec((1, ROWS_R, W), lambda b_, i: (b_, i, 0))
    row_shape = jax.ShapeDtypeStruct((B, Lp, W), F32)
    cpb = ROWS_R // CHUNK
    return pl.pallas_call(
        _rwkv_prep_body,
        grid=(B, nblk),
        in_specs=[
            pl.BlockSpec((1, ROWS_R, RWKV_COLS), lambda b_, i: (b_, i, 0)),
            pl.BlockSpec((1, 8, RWKV_COLS), lambda b_, i: (b_, jnp.maximum(i * (ROWS_R // 8) - 1, 0), 0)),
            _const_spec((1, RWKV_COLS)),
            _const_spec((1, W)),
            _const_spec((128, W)),
            _const_spec((1, W)),
            _const_spec((128, W)),
            _const_spec((128, W)),
            _const_spec((1, W)),
            _const_spec((1, W)),
            _const_spec((1, W)),
            _const_spec((W, LANES)),
            _const_spec((LANES, W)),
            _const_spec((ROWS_R, ROWS_R)),
        ],
        out_specs=[row_spec] * 5 + [pl.BlockSpec((1, cpb, 8, W), lambda b_, i: (b_, i, 0, 0))] + [row_spec] * 2,
        out_shape=[row_shape] * 5 + [jax.ShapeDtypeStruct((B, Lp // CHUNK, 8, W), F32)] + [row_shape] * 2,
        compiler_params=_params("parallel", "arbitrary"),
        name="rwkv_prep",
    )(u_r, u_r, mu, w0, w2p, a0, a2p, g2, k_k, k_a, r_k, hs, hb, tri)


def _rwkv_scan_body(rp_ref, kp_ref, bp_ref, ap_ref, v_ref, pc_ref, g_ref, bonus_ref, lg_ref, lb_ref, o_ref, s_ref,
                    *, nb):
    c = pl.program_id(1)

    @pl.when(c == 0)
    def _():
        s_ref[...] = jnp.zeros_like(s_ref)

    @pl.when(c < OFF // CHUNK)
    def _():
        o_ref[...] = jnp.zeros_like(o_ref)

    @pl.when(c >= OFF // CHUNK)
    def _():
        _rwkv_chunk(rp_ref, kp_ref, bp_ref, ap_ref, v_ref, pc_ref, g_ref, bonus_ref, lg_ref, lb_ref, o_ref, s_ref, nb)


def _split(x):
    hi = x.astype(BF16)
    return hi, (x - hi.astype(F32)).astype(BF16)


def _dot3(a, b, dims):
    dg = lambda p, q: lax.dot_general(p, q, (dims, ((), ())), preferred_element_type=F32)
    return dg(a[0], b[0]) + dg(a[0], b[1]) + dg(a[1], b[0])


_NN = ((1,), (0,))
_NT = ((1,), (1,))
_TN = ((0,), (0,))


def _rwkv_chunk(rp_ref, kp_ref, bp_ref, ap_ref, v_ref, pc_ref, g_ref, bonus_ref, lg_ref, lb_ref, o_ref, s_ref, nb):
    C, N = CHUNK, HEAD_DIM
    ri = lax.broadcasted_iota(I32, (C, C), 0)
    ci = lax.broadcasted_iota(I32, (C, C), 1)
    strict = ri > ci
    incl = ri >= ci
    eye = jnp.where(ri == ci, 1.0, 0.0)
    units = [(b, slice(h * N, (h + 1) * N)) for b in range(nb) for h in range(N_HEADS)]
    ids = range(len(units))
    rows2 = lambda top, bottom: jnp.concatenate([top, bottom], axis=0)
    pcs = [pc_ref[b, 0, 0:1, sl] for b, sl in units]
    ar = [_split(rows2(ap_ref[b, :, sl], rp_ref[b, :, sl])) for b, sl in units]
    bk = [_split(rows2(bp_ref[b, :, sl], kp_ref[b, :, sl])) for b, sl in units]
    v_ = [_split(v_ref[b, :, sl]) for b, sl in units]
    s0 = [s_ref[i] for i in ids]
    s0s = [_split(s) for s in s0]
    gram = [_dot3(ar[i], bk[i], _NT) for i in ids]
    a_ab = [jnp.where(strict, gram[i][0:C, 0:C], 0.0) for i in ids]
    a_ak = [jnp.where(strict, gram[i][0:C, C:2 * C], 0.0) for i in ids]
    a_rb = [_split(jnp.where(incl, gram[i][C:2 * C, 0:C], 0.0)) for i in ids]
    a_rk = [jnp.where(incl, gram[i][C:2 * C, C:2 * C], 0.0) for i in ids]
    t = [eye + a_ab[i] for i in ids]
    pb = [a_ab[i].astype(BF16) for i in ids]
    for _ in range(C.bit_length() - 2):
        pb = [_mm(pb[i], pb[i]).astype(BF16) for i in ids]
        t = [t[i] + _mm(t[i].astype(BF16), pb[i]) for i in ids]
    ts = [_split(t[i]) for i in ids]
    resid = [(eye - t[i]) + _dot3(_split(a_ab[i]), ts[i], _NN) for i in ids]
    t = [t[i] + _mm(ts[i][0], resid[i].astype(BF16)) for i in ids]
    sp = [_dot3(ar[i], s0s[i], _NT) for i in ids]
    av = [_dot3(_split(rows2(a_ak[i], a_rk[i])), v_[i], _NN) for i in ids]
    u_ = [_dot3(_split(t[i]), _split(sp[i][0:C] + av[i][0:C]), _NN) for i in ids]
    for i, (b, sl) in enumerate(units):
        o = sp[i][C:2 * C] + av[i][C:2 * C] + _dot3(a_rb[i], _split(u_[i]), _NN)
        oc = o - jnp.mean(o, axis=1, keepdims=True)
        var = jnp.mean(oc * oc, axis=1, keepdims=True)
        y = oc * lax.rsqrt(var + GN_EPS) * lg_ref[:, sl] + lb_ref[:, sl]
        o_ref[b, :, sl] = (y + bonus_ref[b, :, sl]) * g_ref[b, :, sl]
    for i, (b, sl) in enumerate(units):
        vu = _split(rows2(v_ref[b, :, sl], u_[i]))
        kb = _split(rows2(kp_ref[b, :, sl], bp_ref[b, :, sl]) * pcs[i])
        s_ref[i] = s0[i] * pcs[i] + _dot3(vu, kb, _TN)


def _rwkv_scan(rp, kp, bp, ap, v, pc, g, bonus, lg, lb):
    B, Lp, W = rp.shape
    nch = Lp // CHUNK
    nb = SCAN_BATCH if B % SCAN_BATCH == 0 else 1
    row_spec = pl.BlockSpec((nb, CHUNK, W), lambda b_, c: (b_, c, 0))
    return pl.pallas_call(
        functools.partial(_rwkv_scan_body, nb=nb),
        grid=(B // nb, nch),
        in_specs=[row_spec] * 5 + [pl.BlockSpec((nb, 1, 8, W), lambda b_, c: (b_, c, 0, 0))] + [row_spec] * 2
        + [_const_spec((1, W))] * 2,
        out_specs=row_spec,
        out_shape=jax.ShapeDtypeStruct((B, Lp, W), F32),
        scratch_shapes=[pltpu.VMEM((nb * N_HEADS, HEAD_DIM, HEAD_DIM), F32)],
        compiler_params=_params("parallel", "arbitrary"),
        name="rwkv_scan",
    )(rp, kp, bp, ap, v, pc, g, bonus, lg, lb)


def _rope(x, cos, sin, half, first):
    width = x.shape[1]
    rot = jnp.where(first, pltpu.roll(x, width - half, 1), pltpu.roll(x, half, 1))
    return x * cos + rot * sin


def _dsa_prep_body(u_ref, qg_ref, wuq_ref, wiq_ref, kng_ref, knb_ref, cf_ref, sf_ref, cp_ref, sp_ref,
                   qt_ref, k_ref, vt_ref, qit_ref, kx_ref, wt_ref):
    u = u_ref[0]
    cq = u[:, 0:Q_LORA_RANK]
    k = u[:, Q_LORA_RANK:Q_LORA_RANK + ATT_WIDTH]
    v = u[:, Q_LORA_RANK + ATT_WIDTH:Q_LORA_RANK + 2 * ATT_WIDTH]
    tail = u[:, Q_LORA_RANK + 2 * ATT_WIDTH:]
    cqn = (cq * lax.rsqrt(jnp.mean(cq * cq, -1, keepdims=True) + RMS_EPS) * qg_ref[...]).astype(BF16)
    reps = ATT_WIDTH // LANES
    cf = jnp.concatenate([cf_ref[...]] * reps, axis=1)
    sf = jnp.concatenate([sf_ref[...]] * reps, axis=1)
    cp = jnp.concatenate([cp_ref[...]] * reps, axis=1)
    sp = jnp.concatenate([sp_ref[...]] * reps, axis=1)
    lane_w = lax.broadcasted_iota(I32, (1, ATT_WIDTH), 1) % HEAD_DIM
    first_f = lane_w < HEAD_DIM // 2
    first_p = lane_w < IDX_ROPE_DIM // 2
    q = _rope(_mm(cqn, wuq_ref[...]), cf, sf, HEAD_DIM // 2, first_f)
    qt_ref[0] = (q * (HEAD_DIM ** -0.5)).T.astype(BF16)
    qi = _rope(_mm(cqn, wiq_ref[...]), cp, sp, IDX_ROPE_DIM // 2, first_p)
    qit_ref[0] = qi.T.astype(BF16)
    k_ref[0] = _rope(k, cf, sf, HEAD_DIM // 2, first_f).astype(BF16)
    vt_ref[0] = v.T.astype(BF16)
    lane = lax.broadcasted_iota(I32, (1, LANES), 1)
    is_key = lane < IDX_DIM
    mu = jnp.sum(jnp.where(is_key, tail, 0.0), -1, keepdims=True) * (1.0 / IDX_DIM)
    tc = jnp.where(is_key, tail - mu, 0.0)
    var = jnp.sum(tc * tc, -1, keepdims=True) * (1.0 / IDX_DIM)
    kn = tc * lax.rsqrt(var + LN_EPS) * kng_ref[...] + knb_ref[...]
    kn = _rope(kn, cp_ref[...], sp_ref[...], IDX_ROPE_DIM // 2, (lane % HEAD_DIM) < IDX_ROPE_DIM // 2)
    kn = kn * (IDX_DIM ** -0.5)
    kx_ref[0] = jnp.where(is_key, kn, pltpu.roll(kn, IDX_DIM, 1)).astype(BF16)
    wt_ref[0] = (tail * (IDX_HEADS ** -0.5)).T


def _dsa_prep(u_a, qg, wuq, wiq, kng, knb, cf, sf, cp, sp):
    B, Lp, _ = u_a.shape
    W = ATT_WIDTH
    row = lambda width: pl.BlockSpec((1, ROWS_A, width), lambda b_, i: (b_, i, 0))
    col = lambda height: pl.BlockSpec((1, height, ROWS_A), lambda b_, i: (b_, 0, i))
    tab = pl.BlockSpec((ROWS_A, LANES), lambda b_, i: (i, 0))
    return pl.pallas_call(
        _dsa_prep_body,
        grid=(B, Lp // ROWS_A),
        in_specs=[row(ATT_COLS_PAD), _const_spec((1, Q_LORA_RANK)), _const_spec((Q_LORA_RANK, W)),
                  _const_spec((Q_LORA_RANK, W)), _const_spec((1, LANES)), _const_spec((1, LANES)),
                  tab, tab, tab, tab],
        out_specs=[col(W), row(W), col(W), col(W), row(LANES), col(LANES)],
        out_shape=[jax.ShapeDtypeStruct((B, W, Lp), BF16), jax.ShapeDtypeStruct((B, Lp, W), BF16),
                   jax.ShapeDtypeStruct((B, W, Lp), BF16), jax.ShapeDtypeStruct((B, W, Lp), BF16),
                   jax.ShapeDtypeStruct((B, Lp, LANES), BF16), jax.ShapeDtypeStruct((B, LANES, Lp), F32)],
        compiler_params=_params("parallel", "arbitrary"),
        name="dsa_prep",
    )(u_a, qg, wuq, wiq, kng, knb, cf, sf, cp, sp)


def _fold_rows(x, op=jnp.add):
    parts = [x[8 * r:8 * r + 8] for r in range(x.shape[0] // 8)]
    while len(parts) > 1:
        parts = [op(a, b) for a, b in zip(parts[0::2], parts[1::2])] + parts[len(parts) & ~1:]
    return parts[0]


def _dsa_attn_body(qt_ref, qit_ref, wt_ref, k_ref, vt_ref, kx_ref, o_ref, khi_ref, klo_ref, sc_ref, acc_ref,
                   j_ref, s_ref, pe_ref, *, ksel):
    i = pl.program_id(1)
    nkc = (i * TQ + TQ - 1) // KC + 1
    kf = float(ksel)
    tcol = i * TQ + lax.broadcasted_iota(I32, (1, TQ), 1)
    int_min = jnp.int32(INT_MIN)
    row_in_pair = lax.broadcasted_iota(I32, (LANES, 1), 0)

    def head_operands(ref):
        out = []
        for h in range(N_HEADS):
            pair = ref[0, LANES * (h // 2):LANES * (h // 2 + 1), :]
            keep = (row_in_pair < HEAD_DIM) if h % 2 == 0 else (row_in_pair >= HEAD_DIM)
            out.append(jnp.where(keep, pair, jnp.zeros_like(pair)))
        return out

    def key_rows(ks):
        return ks + lax.broadcasted_iota(I32, (KC, 1), 0)

    qis = head_operands(qit_ref)
    wrows = [wt_ref[0, IDX_DIM + h:IDX_DIM + h + 1, :] for h in range(IDX_HEADS)]

    def score_chunk(kc, carry):
        lo8, hi8 = carry
        ks = pl.multiple_of(kc * KC, KC)
        kx = kx_ref[0, pl.ds(ks, KC), :]
        sc = jnp.zeros((KC, TQ), F32)
        for h in range(IDX_HEADS):
            sc = sc + jnp.maximum(_mm(kx, qis[h]), 0.0) * wrows[h]
        sc = sc + 0.0
        krow = key_rows(ks)
        sc = jnp.where(krow >= OFF, sc, MASKED)
        sc = jnp.where(krow <= tcol, sc, MASKED)
        sc_ref[pl.ds(ks, KC), :] = sc
        bits = lax.bitcast_convert_type(sc, I32)
        key = jnp.where(bits >= 0, bits, bits ^ jnp.int32(0x7FFFFFFF))
        khi_ref[pl.ds(ks, KC), :] = lax.shift_right_arithmetic(key, 16).astype(I16)
        klo_ref[pl.ds(ks, KC), :] = ((key & jnp.int32(0xFFFF)) - HALF16).astype(I16)
        lo8 = jnp.minimum(lo8, _fold_rows(jnp.where(sc <= MASKED, ABOVE_ALL, sc), jnp.minimum))
        hi8 = jnp.maximum(hi8, _fold_rows(sc, jnp.maximum))
        return lo8, hi8

    lo8, hi8 = lax.fori_loop(0, nkc, score_chunk,
                             (jnp.full((8, TQ), ABOVE_ALL, F32), jnp.full((8, TQ), MASKED, F32)))
    smin = jnp.min(lo8, axis=0, keepdims=True)
    smax = jnp.max(hi8, axis=0, keepdims=True)

    def scan_chunks(fn, init):
        def body(kc, carry):
            ks = pl.multiple_of(kc * KC, KC)
            return fn(carry, ks)
        return lax.fori_loop(0, nkc, body, init)

    zeros8 = jnp.zeros((8, TQ), F32)

    def count_where(ref, pred):
        cnt = scan_chunks(lambda c, ks: c + _fold_rows(pred(ref[pl.ds(ks, KC), :], ks)), zeros8)
        return jnp.sum(cnt, axis=0, keepdims=True)

    def count16(ref, pred):
        def fn(cnt, ks):
            m = pred(ref[pl.ds(ks, KC), :])
            parts = [m[16 * r:16 * r + 16] for r in range(KC // 16)]
            while len(parts) > 1:
                parts = [a + b for a, b in zip(parts[0::2], parts[1::2])]
            return cnt + parts[0]
        cnt = scan_chunks(fn, jnp.zeros((16, TQ), I16))
        return jnp.sum(cnt.astype(I32), axis=0, keepdims=True)

    one16, zero16 = jnp.int16(1), jnp.int16(0)

    def radix16(ref, target):
        def bit(bi, prefix):
            cand = prefix | lax.shift_left(jnp.int32(1), 15 - bi)
            cand16 = (cand - HALF16).astype(I16)
            cnt = count16(ref, lambda x: jnp.where(x >= cand16, one16, zero16))
            return jnp.where(cnt >= target, cand, prefix)
        return lax.fori_loop(0, 16, bit, jnp.zeros((1, TQ), I32))

    k_int = jnp.full((1, TQ), ksel, I32)
    thr_hi = radix16(khi_ref, k_int) - HALF16
    thr_hi16 = thr_hi.astype(I16)
    above = count16(khi_ref, lambda x: jnp.where(x > thr_hi16, one16, zero16))

    def keep_low_of_ties(carry, ks):
        rows = pl.ds(ks, KC)
        klo_ref[rows, :] = jnp.where(khi_ref[rows, :] == thr_hi16, klo_ref[rows, :], jnp.int16(-HALF16))
        return carry

    scan_chunks(keep_low_of_ties, 0)
    thr_lo = radix16(klo_ref, k_int - above)
    thr_key = lax.shift_left(thr_hi, 16) | thr_lo
    cand0 = lax.bitcast_convert_type(jnp.where(thr_key >= 0, thr_key, thr_key ^ jnp.int32(0x7FFFFFFF)), F32)

    n_adm = jnp.maximum(tcol - (OFF - 1), 0).astype(F32)
    searching = n_adm > kf

    def probe(mid):
        def fn(carry, ks):
            cnt, vmin = carry
            s = sc_ref[pl.ds(ks, KC), :]
            ge = s >= mid
            return (cnt + _fold_rows(jnp.where(ge, 1.0, 0.0)),
                    jnp.minimum(vmin, _fold_rows(jnp.where(ge, s, ABOVE_ALL), jnp.minimum)))
        cnt, vmin = scan_chunks(fn, (zeros8, jnp.full((8, TQ), ABOVE_ALL, F32)))
        return jnp.sum(cnt, axis=0, keepdims=True), jnp.min(vmin, axis=0, keepdims=True)

    def refine(state):
        it, lo, hi, c_lo, c_gt, done, _ = state
        mid = jnp.where(it == 0, jnp.where(searching, cand0, lo), lo + 0.5 * (hi - lo))
        c_mid, v_mid = probe(mid)
        up = c_mid >= kf
        lo_n = jnp.where(up, v_mid, lo)
        hi_n = jnp.where(up, hi, mid)
        c_lo_n = jnp.where(up, c_mid, c_lo)
        c_gt_n = count_where(sc_ref, lambda s, ks: jnp.where(s > lo_n, 1.0, 0.0))
        stalled = jnp.where(it > 0, jnp.where(mid <= lo, 1.0, jnp.where(mid >= hi, 1.0, 0.0)), 0.0)
        fin = jnp.maximum(jnp.where(c_gt_n < kf, 1.0, 0.0), stalled)
        frozen = done > 0.0
        keep = lambda old, new_: jnp.where(frozen, old, new_)
        done_n = jnp.maximum(done, fin)
        return (it + 1, keep(lo, lo_n), keep(hi, hi_n), keep(c_lo, c_lo_n), keep(c_gt, c_gt_n), done_n,
                jnp.max(1.0 - done_n))

    done0 = jnp.where(searching, 0.0, 1.0)
    state0 = (jnp.int32(0), smin, smax + (jnp.abs(smax) + 1.0) * 1e-6, n_adm, n_adm, done0, jnp.max(1.0 - done0))
    state = lax.while_loop(lambda st: jnp.logical_and(st[6] > 0.0, st[0] < MAX_REFINE), refine, state0)
    thr = jnp.where(searching, state[1], BELOW_ALL)
    cnt_gt = jnp.where(searching, state[4], n_adm)
    cnt_eq = jnp.where(searching, state[3] - state[4], 0.0)
    need = kf - cnt_gt

    j_ref[...] = jnp.full(j_ref.shape, 2 ** 30, I32)

    @pl.when(jnp.max(cnt_eq - need) > 0.0)
    def _():
        def index_bit(bi, prefix):
            cand = prefix | lax.shift_left(jnp.int32(1), 12 - bi)
            before = count_where(
                sc_ref, lambda s, ks: jnp.where(s == thr, jnp.where(key_rows(ks) < cand, 1.0, 0.0), 0.0))
            return jnp.where(before < need, cand, prefix)
        jst = lax.fori_loop(0, 13, index_bit, jnp.zeros((1, TQ), I32))
        j_ref[...] = jnp.broadcast_to(jst, j_ref.shape)

    jstar = j_ref[0:1, :]

    qs = head_operands(qt_ref)
    acc_ref[...] = jnp.zeros_like(acc_ref)

    ones_rows = jnp.ones((16, KC), BF16)

    def attend_chunk(kc, carry):
        ms, ls = carry
        ks = pl.multiple_of(kc * KC, KC)
        sc = sc_ref[pl.ds(ks, KC), :]
        tie = jnp.where(sc == thr, jnp.where(key_rows(ks) <= jstar, 0.0, NEG), NEG)
        bias = jnp.where(sc > thr, 0.0, tie)
        chunk_max = []
        for h in range(N_HEADS):
            p = h // 2
            kp = k_ref[0, pl.ds(ks, KC), LANES * p:LANES * (p + 1)]
            s = _mm(kp, qs[h]) + bias
            s_ref[h] = s
            chunk_max.append(jnp.max(s, axis=0, keepdims=True))
        new_ms, new_ls, alphas = [], [], []
        for h in range(N_HEADS):
            m_new = jnp.maximum(ms[h], chunk_max[h])
            alphas.append(jnp.exp(ms[h] - m_new))
            new_ms.append(m_new)
            pe_ref[h] = jnp.exp(s_ref[h] - m_new).astype(BF16)
        for h in range(N_HEADS):
            vt = vt_ref[0, HEAD_DIM * h:HEAD_DIM * (h + 1), pl.ds(ks, KC)]
            pv = _mm(jnp.concatenate([vt, ones_rows], axis=0), pe_ref[h])
            rows = slice(HEAD_DIM * h, HEAD_DIM * (h + 1))
            acc_ref[rows, :] = acc_ref[rows, :] * alphas[h] + pv[0:HEAD_DIM]
            new_ls.append(alphas[h] * ls[h] + pv[HEAD_DIM:HEAD_DIM + 1])
        return tuple(new_ms), tuple(new_ls)

    init = (tuple(jnp.full((1, TQ), NEG, F32) for _ in range(N_HEADS)),
            tuple(jnp.zeros((1, TQ), F32) for _ in range(N_HEADS)))
    _, ls = lax.fori_loop(0, nkc, attend_chunk, init)
    for p in range(N_HEADS // 2):
        parts = [acc_ref[HEAD_DIM * h:HEAD_DIM * (h + 1), :] / ls[h] for h in (2 * p, 2 * p + 1)]
        o_ref[0, :, LANES * p:LANES * (p + 1)] = jnp.concatenate(parts, axis=0).T


def _dsa_attn(qt, k, vt, qit, kx, wt, ksel):
    B, Lp, W = k.shape
    assert Lp % KC == 0 and Lp <= 8192
    qcol = lambda height: pl.BlockSpec((1, height, TQ), lambda b_, i: (b_, 0, i))
    full = lambda shape: pl.BlockSpec((1,) + shape, lambda b_, i: (b_, 0, 0))
    return pl.pallas_call(
        functools.partial(_dsa_attn_body, ksel=ksel),
        grid=(B, Lp // TQ),
        in_specs=[qcol(W), qcol(W), qcol(LANES), full((Lp, W)), full((W, Lp)), full((Lp, LANES))],
        out_specs=pl.BlockSpec((1, TQ, W), lambda b_, i: (b_, i, 0)),
        out_shape=jax.ShapeDtypeStruct((B, Lp, W), F32),
        scratch_shapes=[
            pltpu.VMEM((Lp, TQ), I16),
            pltpu.VMEM((Lp, TQ), I16),
            pltpu.VMEM((Lp, TQ), F32),
            pltpu.VMEM((W, TQ), F32),
            pltpu.VMEM((8, TQ), I32),
            pltpu.VMEM((N_HEADS, KC, TQ), F32),
            pltpu.VMEM((N_HEADS, KC, TQ), BF16),
        ],
        compiler_params=_params("parallel", "arbitrary"),
        name="dsa_attn",
    )(qt, qit, wt, k, vt, kx)


def _outproj_router_body(x_ref, yr_ref, ya_ref, eg_ref, eb_ref, wo_ref, g1_ref, b1_ref, wrh_ref, wrl_ref, br_ref,
                         h_ref, gates_ref):
    h0 = _layer_norm(x_ref[0], eg_ref[...], eb_ref[...])
    mix = (_mm(yr_ref[0].astype(BF16), wo_ref[0:RWKV_WIDTH, :])
           + _mm(ya_ref[0].astype(BF16), wo_ref[RWKV_WIDTH:, :]))
    h1 = _layer_norm(DN_ALPHA * h0 + mix, g1_ref[...], b1_ref[...])
    h_ref[0] = h1
    logits = _dot3(_split(h1), (wrh_ref[...], wrl_ref[...]), _NN) + br_ref[...]
    lane = lax.broadcasted_iota(I32, (1, LANES), 1)
    lanef = lane.astype(F32)
    low = -3e38
    lgm = jnp.where(lane < N_GROUPS, logits, low)
    gmax = jnp.max(lgm, axis=1, keepdims=True)
    gsel = jnp.min(jnp.where(lgm == gmax, lanef, 1e9), axis=1, keepdims=True)
    gsum = jnp.sum(jnp.where(lane < N_GROUPS, jnp.exp(lgm - gmax), 0.0), axis=1, keepdims=True)
    group_of_lane = ((lane - EXPERT_LANE0) // EXPERTS_PER_GROUP).astype(F32)
    lem = jnp.where(group_of_lane == gsel, logits, low)
    m1 = jnp.max(lem, axis=1, keepdims=True)
    i1 = jnp.min(jnp.where(lem == m1, lanef, 1e9), axis=1, keepdims=True)
    lem2 = jnp.where(lanef == i1, low, lem)
    m2 = jnp.max(lem2, axis=1, keepdims=True)
    i2 = jnp.min(jnp.where(lem2 == m2, lanef, 1e9), axis=1, keepdims=True)
    e2 = jnp.exp(m2 - m1)
    w1 = 1.0 / (1.0 + e2)
    w2 = e2 / (1.0 + e2)
    gates = jnp.where(lanef == i1, w1, jnp.where(lanef == i2, w2, 0.0)) / gsum
    gates_ref[0] = gates


def _outproj_router(x, y_r, y_a, eg, eb, wo, g1, b1, wrh, wrl, br):
    B, S, D = x.shape
    skip = PAD_ROWS // ROWS_A
    xrow = pl.BlockSpec((1, ROWS_A, D), lambda b_, i: (b_, i, 0))
    yrow = pl.BlockSpec((1, ROWS_A, RWKV_WIDTH), lambda b_, i: (b_, i + skip, 0))
    vec = _const_spec((1, D))
    return pl.pallas_call(
        _outproj_router_body,
        grid=(B, S // ROWS_A),
        in_specs=[xrow, yrow, yrow, vec, vec, _const_spec((D, D)), vec, vec,
                  _const_spec((D, LANES)), _const_spec((D, LANES)), _const_spec((1, LANES))],
        out_specs=[xrow, pl.BlockSpec((1, ROWS_A, LANES), lambda b_, i: (b_, i, 0))],
        out_shape=[jax.ShapeDtypeStruct((B, S, D), F32), jax.ShapeDtypeStruct((B, S, LANES), F32)],
        compiler_params=_params("parallel", "arbitrary"),
        name="outproj_router",
    )(x, y_r, y_a, eg, eb, wo, g1, b1, wrh, wrl, br)


def _moe_body(h_ref, gates_ref, wg_ref, wu_ref, wd_ref, g2_ref, b2_ref, o_ref, acc_ref, hb_ref):
    e = pl.program_id(1)

    @pl.when(e == 0)
    def _():
        acc_ref[...] = jnp.zeros_like(acc_ref)
        hb_ref[...] = h_ref[...].astype(BF16)

    t = hb_ref[...]
    lane = lax.broadcasted_iota(I32, (1, LANES), 1)
    gcol = jnp.sum(jnp.where(lane == e + EXPERT_LANE0, gates_ref[...], 0.0), axis=1, keepdims=True)
    a = _mm(t, wg_ref[0])
    hid = a * _sigmoid(a) * _mm(t, wu_ref[0]) * gcol
    acc_ref[...] += _mm(hid.astype(BF16), wd_ref[0])

    @pl.when(e == N_EXPERTS - 1)
    def _():
        o_ref[...] = _layer_norm(DN_ALPHA * h_ref[...] + acc_ref[...], g2_ref[...], b2_ref[...])


def _moe(h1, gates, wg, wu, wd, g2, b2):
    T, D = h1.shape
    tile = pl.BlockSpec((TM, D), lambda i, e: (i, 0))
    return pl.pallas_call(
        _moe_body,
        grid=(T // TM, N_EXPERTS),
        in_specs=[tile, pl.BlockSpec((TM, LANES), lambda i, e: (i, 0)),
                  pl.BlockSpec((1, D, D_EXPERT), lambda i, e: (e, 0, 0)),
                  pl.BlockSpec((1, D, D_EXPERT), lambda i, e: (e, 0, 0)),
                  pl.BlockSpec((1, D_EXPERT, D), lambda i, e: (e, 0, 0)),
                  _const_spec((1, D)), _const_spec((1, D))],
        out_specs=tile,
        out_shape=jax.ShapeDtypeStruct((T, D), F32),
        scratch_shapes=[pltpu.VMEM((TM, D), F32), pltpu.VMEM((TM, D), BF16)],
        compiler_params=_params("parallel", "arbitrary"),
        name="moe",
    )(h1, gates, wg, wu, wd, g2, b2)


def _rope_tables(Lp):
    pos = jnp.maximum(jnp.arange(Lp, dtype=I32) - OFF, 0).astype(F32)
    j = jnp.arange(LANES) % HEAD_DIM

    def table(half, rot_dim):
        inv = 1.0 / (ROPE_THETA ** (jnp.arange(half, dtype=F32) / half))
        ang = pos[:, None] * inv[None, :]
        cos, sin = jnp.cos(ang)[:, j % half], jnp.sin(ang)[:, j % half]
        rotated = (j < rot_dim)[None, :]
        sign = jnp.where(j < half, -1.0, 1.0)[None, :]
        return jnp.where(rotated, cos, 1.0), jnp.where(rotated, sin * sign, 0.0)

    cf, sf = table(HEAD_DIM // 2, HEAD_DIM)
    cp, sp = table(IDX_ROPE_DIM // 2, IDX_ROPE_DIM)
    return cf, sf, cp, sp


def _block_ones(n, block):
    idx = jnp.arange(n) // block
    return (idx[:, None] == idx[None, :]).astype(F32)


def kernel(x, meta_tokens, ln_emb_g, ln_emb_b, w_in, rw_mu, rw_w0, rw_w2, rw_a0, rw_a2, rw_g2, rw_kk, rw_ka,
           rw_rk, rw_lnx_g, rw_lnx_b, att_qnorm_g, att_wuq, idx_wq, idx_knorm_g, idx_knorm_b, w_out, ln1_g,
           ln1_b, rt_grp_w, rt_grp_b, rt_exp_w, rt_exp_b, ex_w_gate, ex_w_up, ex_w_down, ln2_g, ln2_b):
    B, S, D = x.shape
    assert w_in.shape[0] == 1 and D == D_MODEL and S % TM == 0
    Lp = S + PAD_ROWS
    ksel = min(INDEX_TOPK, S // 4)
    row = lambda t: t.reshape(1, -1)
    W = RWKV_WIDTH

    meta_pad = jnp.zeros((PAD_ROWS, D), F32).at[OFF:].set(meta_tokens)
    w_in_p = jnp.pad(w_in[0], ((0, 0), (0, ATT_COLS_PAD - ATT_COLS))).astype(BF16)
    u_r, u_a = _ln_inproj(x, meta_pad, row(ln_emb_g), row(ln_emb_b), w_in_p)

    w2p = jnp.concatenate([rw_w2[0], jnp.zeros((ICLR_RANK, W), F32)], 0).astype(BF16)
    a2p = jnp.concatenate([jnp.zeros((DECAY_RANK, W), F32), rw_a2[0]], 0).astype(BF16)
    head_of_lane = jnp.arange(W) // HEAD_DIM
    hs = (head_of_lane[:, None] == jnp.arange(LANES)[None, :]).astype(BF16)
    tri = (jnp.arange(ROWS_R)[:, None] >= jnp.arange(ROWS_R)[None, :]).astype(F32)
    tri = (tri * _block_ones(ROWS_R, CHUNK)).astype(BF16)
    rp, kp, bp, ap, v, pc, g, bonus = _rwkv_prep(
        u_r, row(rw_mu[0]), row(rw_w0[0]), w2p, row(rw_a0[0]), a2p, rw_g2[0].astype(BF16), row(rw_kk[0]),
        row(rw_ka[0]), row(rw_rk[0]), hs, hs.T, tri)
    y_r = _rwkv_scan(rp, kp, bp, ap, v, pc, g, bonus, row(rw_lnx_g[0]), row(rw_lnx_b[0]))

    pad_lanes = lambda t: jnp.pad(t, (0, LANES - t.shape[0])).reshape(1, LANES)
    cf, sf, cp, sp = _rope_tables(Lp)
    qt, k, vt, qit, kx, wt = _dsa_prep(
        u_a, row(att_qnorm_g[0]), att_wuq[0].astype(BF16), idx_wq[0].astype(BF16),
        pad_lanes(idx_knorm_g[0]), pad_lanes(idx_knorm_b[0]), cf, sf, cp, sp)
    y_a = _dsa_attn(qt, k, vt, qit, kx, wt, ksel)

    wr = jnp.zeros((D, LANES), F32).at[:, :N_GROUPS].set(rt_grp_w[0])
    wr = wr.at[:, EXPERT_LANE0:EXPERT_LANE0 + N_EXPERTS].set(rt_exp_w[0])
    br = jnp.zeros((1, LANES), F32).at[0, :N_GROUPS].set(rt_grp_b[0])
    br = br.at[0, EXPERT_LANE0:EXPERT_LANE0 + N_EXPERTS].set(rt_exp_b[0])
    wrh = wr.astype(BF16)
    wrl = (wr - wrh.astype(F32)).astype(BF16)
    h1, gates = _outproj_router(x, y_r, y_a, row(ln_emb_g), row(ln_emb_b), w_out[0].astype(BF16),
                                row(ln1_g[0]), row(ln1_b[0]), wrh, wrl, br)
    out = _moe(h1.reshape(B * S, D), gates.reshape(B * S, LANES), ex_w_gate[0].astype(BF16),
               ex_w_up[0].astype(BF16), ex_w_down[0].astype(BF16), row(ln2_g[0]), row(ln2_b[0]))
    return out.reshape(B, S, D)
```

```python
import functools

import jax
import jax.numpy as jnp
from jax import lax
from jax.experimental import pallas as pl
from jax.experimental.pallas import tpu as pltpu

F32 = jnp.float32
BF16 = jnp.bfloat16
I32 = jnp.int32
I16 = jnp.int16
HIGHEST = lax.Precision.HIGHEST

D_MODEL = 1024
N_META = 16
RWKV_WIDTH = 512
ATT_WIDTH = 512
HEAD_DIM = 64
N_HEADS = 8
DECAY_RANK = 64
ICLR_RANK = 64
GATE_RANK = 128
Q_LORA_RANK = 256
IDX_HEADS = 8
IDX_DIM = 64
IDX_ROPE_DIM = 32
INDEX_TOPK = 256
ROPE_THETA = 10000.0
N_GROUPS = 4
EXPERTS_PER_GROUP = 8
N_EXPERTS = N_GROUPS * EXPERTS_PER_GROUP
D_EXPERT = 256
DN_ALPHA = 2.0 ** 0.25
LN_EPS = 1e-5
RMS_EPS = 1e-6
GN_EPS = 64e-5
RWKV_COLS = 3 * RWKV_WIDTH + DECAY_RANK + ICLR_RANK + GATE_RANK
ATT_COLS = Q_LORA_RANK + 2 * ATT_WIDTH + IDX_DIM + IDX_HEADS
ATT_COLS_PAD = 1408

LANES = 128
PAD_ROWS = 256
OFF = PAD_ROWS - N_META
CHUNK = 64
ROWS_A = 256
ROWS_R = 256
TQ = 256
KC = 256
TMG = 512
SCAN_BATCH = 4
EXPERT_LANE0 = 64
NEG = -1e30
MASKED = -3e38
BELOW_ALL = -1e38
ABOVE_ALL = 3e38
MAX_REFINE = 400
INT_MIN = -2147483648
HALF16 = 32768
VMEM_LIMIT = 56 * 1024 * 1024


def _mm(a, b, precision=None):
    return jnp.dot(a, b, preferred_element_type=F32, precision=precision)


def _mm_nt(a, b, precision=None):
    return lax.dot_general(a, b, (((1,), (1,)), ((), ())), preferred_element_type=F32, precision=precision)


def _mm_tn(a, b, precision=None):
    return lax.dot_general(a, b, (((0,), (0,)), ((), ())), preferred_element_type=F32, precision=precision)


def _sigmoid(x):
    return 1.0 / (1.0 + jnp.exp(-x))


def _layer_norm(x, g, b):
    mu = jnp.mean(x, -1, keepdims=True)
    xc = x - mu
    var = jnp.mean(xc * xc, -1, keepdims=True)
    return xc * lax.rsqrt(var + LN_EPS) * g + b


def _params(*sem):
    return pltpu.CompilerParams(dimension_semantics=sem, vmem_limit_bytes=VMEM_LIMIT)


def _const_spec(shape):
    nd = len(shape)
    return pl.BlockSpec(shape, lambda *_: (0,) * nd)


def _ln_inproj_body(x_ref, meta_ref, g_ref, b_ref, w_ref, ur_ref, ua_ref):
    blk = pl.program_id(1)
    xin = jnp.where(blk == 0, meta_ref[...], x_ref[0])
    h = _layer_norm(xin, g_ref[...], b_ref[...])
    row = lax.broadcasted_iota(I32, (ROWS_A, 1), 0)
    h = jnp.where((blk > 0) | (row >= OFF), h, 0.0)
    hb = h.astype(BF16)
    step = 256
    for n0 in range(0, RWKV_COLS, step):
        n1 = min(n0 + step, RWKV_COLS)
        ur_ref[0, :, n0:n1] = _mm(hb, w_ref[:, n0:n1])
    for n0 in range(0, ATT_COLS_PAD, step):
        n1 = min(n0 + step, ATT_COLS_PAD)
        ua_ref[0, :, n0:n1] = _mm(hb, w_ref[:, RWKV_COLS + n0:RWKV_COLS + n1])


def _ln_inproj(x, meta_pad, g, b, w):
    B, S, D = x.shape
    nblk = (S + PAD_ROWS) // ROWS_A
    Lp = S + PAD_ROWS
    ncols = RWKV_COLS + ATT_COLS_PAD
    return pl.pallas_call(
        _ln_inproj_body,
        grid=(B, nblk),
        in_specs=[
            pl.BlockSpec((1, ROWS_A, D), lambda b_, i: (b_, jnp.maximum(i - 1, 0), 0)),
            _const_spec((ROWS_A, D)),
            _const_spec((1, D)),
            _const_spec((1, D)),
            _const_spec((D, ncols)),
        ],
        out_specs=[
            pl.BlockSpec((1, ROWS_A, RWKV_COLS), lambda b_, i: (b_, i, 0)),
            pl.BlockSpec((1, ROWS_A, ATT_COLS_PAD), lambda b_, i: (b_, i, 0)),
        ],
        out_shape=[
            jax.ShapeDtypeStruct((B, Lp, RWKV_COLS), F32),
            jax.ShapeDtypeStruct((B, Lp, ATT_COLS_PAD), F32),
        ],
        compiler_params=_params("parallel", "arbitrary"),
        name="ln_inproj",
    )(x, meta_pad, g, b, w)


def _split3(x):
    hi = x.astype(BF16)
    r1 = x - hi.astype(F32)
    mid = r1.astype(BF16)
    return hi, mid, (r1 - mid.astype(F32)).astype(BF16)


def _mm_exact_rhs(x, m):
    return sum(_mm(p, m) for p in _split3(x))


def _mm_exact_lhs(m, x):
    return sum(_mm(m, p) for p in _split3(x))


def _rwkv_prep_body(u_ref, prev_ref, mu_ref, w0_ref, w2_ref, a0_ref, a2_ref, g2_ref, kk_ref, ka_ref, rk_ref,
                    hs_ref, hb_ref, tri_ref,
                    rp_ref, kp_ref, bp_ref, ap_ref, v_ref, pc_ref, g_ref, bonus_ref):
    blk = pl.program_id(1)
    u = u_ref[0]
    prev = jnp.where(blk == 0, 0.0, prev_ref[0][7:8, :])
    row = lax.broadcasted_iota(I32, (ROWS_R, 1), 0)
    shifted = jnp.where(row == 0, prev, pltpu.roll(u, 1, 0))
    ul = u + (shifted - u) * mu_ref[...]
    W = RWKV_WIDTH
    r = ul[:, 0:W]
    k = ul[:, W:2 * W]
    v = ul[:, 2 * W:3 * W]
    wa = ul[:, 3 * W:3 * W + 128]
    gd = ul[:, 3 * W + 128:3 * W + 256]
    w = w0_ref[...] + _mm(jnp.tanh(wa).astype(BF16), w2_ref[...])
    softplus_neg_w = jnp.maximum(-w, 0.0) + jnp.log(1.0 + jnp.exp(-jnp.abs(w)))
    logd = -jnp.exp(-softplus_neg_w - 0.5)
    a = _sigmoid(a0_ref[...] + _mm(wa.astype(BF16), a2_ref[...]))
    g_ref[0] = _mm(_sigmoid(gd).astype(BF16), g2_ref[...])
    head_sum = lambda t: _mm_exact_rhs(_mm_exact_rhs(t, hs_ref[...]), hb_ref[...])
    kkr = k * kk_ref[...]
    kk = kkr / jnp.maximum(jnp.sqrt(head_sum(kkr * kkr)), 1e-12)
    kmod = k * (1.0 + (a - 1.0) * ka_ref[...])
    bonus_ref[0] = head_sum(r * kmod * rk_ref[...]) * v
    v_ref[0] = v
    cum = _mm_exact_lhs(tri_ref[...], logd)
    rp_ref[0] = r * jnp.exp(cum)
    einv = jnp.exp(-cum)
    kp_ref[0] = kmod * einv
    bp_ref[0] = kk * a * einv
    ap_ref[0] = -kk * jnp.exp(cum - logd)
    for c in range(ROWS_R // CHUNK):
        last = c * CHUNK + CHUNK - 1
        pc_ref[0, c] = jnp.broadcast_to(jnp.exp(cum[last:last + 1]), (8, W))


def _rwkv_prep(u_r, mu, w0, w2p, a0, a2p, g2, k_k, k_a, r_k, hs, hb, tri):
    B, Lp, _ = u_r.shape
    nblk = Lp // ROWS_R
    W = RWKV_WIDTH
    row_spec = pl.BlockSpec((1, ROWS_R, W), lambda b_, i: (b_, i, 0))
    row_shape = jax.ShapeDtypeStruct((B, Lp, W), F32)
    cpb = ROWS_R // CHUNK
    return pl.pallas_call(
        _rwkv_prep_body,
        grid=(B, nblk),
        in_specs=[
            pl.BlockSpec((1, ROWS_R, RWKV_COLS), lambda b_, i: (b_, i, 0)),
            pl.BlockSpec((1, 8, RWKV_COLS), lambda b_, i: (b_, jnp.maximum(i * (ROWS_R // 8) - 1, 0), 0)),
            _const_spec((1, RWKV_COLS)),
            _const_spec((1, W)),
            _const_spec((128, W)),
            _const_spec((1, W)),
            _const_spec((128, W)),
            _const_spec((128, W)),
            _const_spec((1, W)),
            _const_spec((1, W)),
            _const_spec((1, W)),
            _const_spec((W, LANES)),
            _const_spec((LANES, W)),
            _const_spec((ROWS_R, ROWS_R)),
        ],
        out_specs=[row_spec] * 5 + [pl.BlockSpec((1, cpb, 8, W), lambda b_, i: (b_, i, 0, 0))] + [row_spec] * 2,
        out_shape=[row_shape] * 5 + [jax.ShapeDtypeStruct((B, Lp // CHUNK, 8, W), F32)] + [row_shape] * 2,
        compiler_params=_params("parallel", "arbitrary"),
        name="rwkv_prep",
    )(u_r, u_r, mu, w0, w2p, a0, a2p, g2, k_k, k_a, r_k, hs, hb, tri)


def _rwkv_scan_body(rp_ref, kp_ref, bp_ref, ap_ref, v_ref, pc_ref, g_ref, bonus_ref, lg_ref, lb_ref, o_ref, s_ref,
                    *, nb):
    c = pl.program_id(1)

    @pl.when(c == 0)
    def _():
        s_ref[...] = jnp.zeros_like(s_ref)

    @pl.when(c < OFF // CHUNK)
    def _():
        o_ref[...] = jnp.zeros_like(o_ref)

    @pl.when(c >= OFF // CHUNK)
    def _():
        _rwkv_chunk(rp_ref, kp_ref, bp_ref, ap_ref, v_ref, pc_ref, g_ref, bonus_ref, lg_ref, lb_ref, o_ref, s_ref, nb)


def _split(x):
    hi = x.astype(BF16)
    return hi, (x - hi.astype(F32)).astype(BF16)


def _dot3(a, b, dims):
    dg = lambda p, q: lax.dot_general(p, q, (dims, ((), ())), preferred_element_type=F32)
    return dg(a[0], b[0]) + dg(a[0], b[1]) + dg(a[1], b[0])


_NN = ((1,), (0,))
_NT = ((1,), (1,))
_TN = ((0,), (0,))


def _rwkv_chunk(rp_ref, kp_ref, bp_ref, ap_ref, v_ref, pc_ref, g_ref, bonus_ref, lg_ref, lb_ref, o_ref, s_ref, nb):
    C, N = CHUNK, HEAD_DIM
    ri = lax.broadcasted_iota(I32, (C, C), 0)
    ci = lax.broadcasted_iota(I32, (C, C), 1)
    strict = ri > ci
    incl = ri >= ci
    eye = jnp.where(ri == ci, 1.0, 0.0)
    units = [(b, slice(h * N, (h + 1) * N)) for b in range(nb) for h in range(N_HEADS)]
    ids = range(len(units))
    rows2 = lambda top, bottom: jnp.concatenate([top, bottom], axis=0)
    pcs = [pc_ref[b, 0, 0:1, sl] for b, sl in units]
    ar = [_split(rows2(ap_ref[b, :, sl], rp_ref[b, :, sl])) for b, sl in units]
    bk = [_split(rows2(bp_ref[b, :, sl], kp_ref[b, :, sl])) for b, sl in units]
    v_ = [_split(v_ref[b, :, sl]) for b, sl in units]
    s0 = [s_ref[i] for i in ids]
    s0s = [_split(s) for s in s0]
    gram = [_dot3(ar[i], bk[i], _NT) for i in ids]
    a_ab = [jnp.where(strict, gram[i][0:C, 0:C], 0.0) for i in ids]
    a_ak = [jnp.where(strict, gram[i][0:C, C:2 * C], 0.0) for i in ids]
    a_rb = [_split(jnp.where(incl, gram[i][C:2 * C, 0:C], 0.0)) for i in ids]
    a_rk = [jnp.where(incl, gram[i][C:2 * C, C:2 * C], 0.0) for i in ids]
    t = [eye + a_ab[i] for i in ids]
    pb = [a_ab[i].astype(BF16) for i in ids]
    for _ in range(C.bit_length() - 2):
        pb = [_mm(pb[i], pb[i]).astype(BF16) for i in ids]
        t = [t[i] + _mm(t[i].astype(BF16), pb[i]) for i in ids]
    ts = [_split(t[i]) for i in ids]
    resid = [(eye - t[i]) + _dot3(_split(a_ab[i]), ts[i], _NN) for i in ids]
    t = [t[i] + _mm(ts[i][0], resid[i].astype(BF16)) for i in ids]
    sp = [_dot3(ar[i], s0s[i], _NT) for i in ids]
    av = [_dot3(_split(rows2(a_ak[i], a_rk[i])), v_[i], _NN) for i in ids]
    u_ = [_dot3(_split(t[i]), _split(sp[i][0:C] + av[i][0:C]), _NN) for i in ids]
    for i, (b, sl) in enumerate(units):
        o = sp[i][C:2 * C] + av[i][C:2 * C] + _dot3(a_rb[i], _split(u_[i]), _NN)
        oc = o - jnp.mean(o, axis=1, keepdims=True)
        var = jnp.mean(oc * oc, axis=1, keepdims=True)
        y = oc * lax.rsqrt(var + GN_EPS) * lg_ref[:, sl] + lb_ref[:, sl]
        o_ref[b, :, sl] = (y + bonus_ref[b, :, sl]) * g_ref[b, :, sl]
    for i, (b, sl) in enumerate(units):
        vu = _split(rows2(v_ref[b, :, sl], u_[i]))
        kb = _split(rows2(kp_ref[b, :, sl], bp_ref[b, :, sl]) * pcs[i])
        s_ref[i] = s0[i] * pcs[i] + _dot3(vu, kb, _TN)


def _rwkv_scan(rp, kp, bp, ap, v, pc, g, bonus, lg, lb):
    B, Lp, W = rp.shape
    nch = Lp // CHUNK
    nb = SCAN_BATCH if B % SCAN_BATCH == 0 else 1
    row_spec = pl.BlockSpec((nb, CHUNK, W), lambda b_, c: (b_, c, 0))
    return pl.pallas_call(
        functools.partial(_rwkv_scan_body, nb=nb),
        grid=(B // nb, nch),
        in_specs=[row_spec] * 5 + [pl.BlockSpec((nb, 1, 8, W), lambda b_, c: (b_, c, 0, 0))] + [row_spec] * 2
        + [_const_spec((1, W))] * 2,
        out_specs=row_spec,
        out_shape=jax.ShapeDtypeStruct((B, Lp, W), F32),
        scratch_shapes=[pltpu.VMEM((nb * N_HEADS, HEAD_DIM, HEAD_DIM), F32)],
        compiler_params=_params("parallel", "arbitrary"),
        name="rwkv_scan",
    )(rp, kp, bp, ap, v, pc, g, bonus, lg, lb)


def _rope(x, cos, sin, half, first):
    width = x.shape[1]
    rot = jnp.where(first, pltpu.roll(x, width - half, 1), pltpu.roll(x, half, 1))
    return x * cos + rot * sin


def _dsa_prep_body(u_ref, qg_ref, wuq_ref, wiq_ref, kng_ref, knb_ref, cf_ref, sf_ref, cp_ref, sp_ref,
                   qt_ref, k_ref, vt_ref, qit_ref, kx_ref, wt_ref):
    u = u_ref[0]
    cq = u[:, 0:Q_LORA_RANK]
    k = u[:, Q_LORA_RANK:Q_LORA_RANK + ATT_WIDTH]
    v = u[:, Q_LORA_RANK + ATT_WIDTH:Q_LORA_RANK + 2 * ATT_WIDTH]
    tail = u[:, Q_LORA_RANK + 2 * ATT_WIDTH:]
    cqn = (cq * lax.rsqrt(jnp.mean(cq * cq, -1, keepdims=True) + RMS_EPS) * qg_ref[...]).astype(BF16)
    reps = ATT_WIDTH // LANES
    cf = jnp.concatenate([cf_ref[...]] * reps, axis=1)
    sf = jnp.concatenate([sf_ref[...]] * reps, axis=1)
    cp = jnp.concatenate([cp_ref[...]] * reps, axis=1)
    sp = jnp.concatenate([sp_ref[...]] * reps, axis=1)
    lane_w = lax.broadcasted_iota(I32, (1, ATT_WIDTH), 1) % HEAD_DIM
    first_f = lane_w < HEAD_DIM // 2
    first_p = lane_w < IDX_ROPE_DIM // 2
    q = _rope(_mm(cqn, wuq_ref[...]), cf, sf, HEAD_DIM // 2, first_f)
    qt_ref[0] = (q * (HEAD_DIM ** -0.5)).T.astype(BF16)
    qi = _rope(_mm(cqn, wiq_ref[...]), cp, sp, IDX_ROPE_DIM // 2, first_p)
    qit_ref[0] = qi.T.astype(BF16)
    k_ref[0] = _rope(k, cf, sf, HEAD_DIM // 2, first_f).astype(BF16)
    vt_ref[0] = v.T.astype(BF16)
    lane = lax.broadcasted_iota(I32, (1, LANES), 1)
    is_key = lane < IDX_DIM
    mu = jnp.sum(jnp.where(is_key, tail, 0.0), -1, keepdims=True) * (1.0 / IDX_DIM)
    tc = jnp.where(is_key, tail - mu, 0.0)
    var = jnp.sum(tc * tc, -1, keepdims=True) * (1.0 / IDX_DIM)
    kn = tc * lax.rsqrt(var + LN_EPS) * kng_ref[...] + knb_ref[...]
    kn = _rope(kn, cp_ref[...], sp_ref[...], IDX_ROPE_DIM // 2, (lane % HEAD_DIM) < IDX_ROPE_DIM // 2)
    kn = kn * (IDX_DIM ** -0.5)
    kx_ref[0] = jnp.where(is_key, kn, pltpu.roll(kn, IDX_DIM, 1)).astype(BF16)
    wt_ref[0] = (tail * (IDX_HEADS ** -0.5)).T


def _dsa_prep(u_a, qg, wuq, wiq, kng, knb, cf, sf, cp, sp):
    B, Lp, _ = u_a.shape
    W = ATT_WIDTH
    row = lambda width: pl.BlockSpec((1, ROWS_A, width), lambda b_, i: (b_, i, 0))
    col = lambda height: pl.BlockSpec((1, height, ROWS_A), lambda b_, i: (b_, 0, i))
    tab = pl.BlockSpec((ROWS_A, LANES), lambda b_, i: (i, 0))
    return pl.pallas_call(
        _dsa_prep_body,
        grid=(B, Lp // ROWS_A),
        in_specs=[row(ATT_COLS_PAD), _const_spec((1, Q_LORA_RANK)), _const_spec((Q_LORA_RANK, W)),
                  _const_spec((Q_LORA_RANK, W)), _const_spec((1, LANES)), _const_spec((1, LANES)),
                  tab, tab, tab, tab],
        out_specs=[col(W), row(W), col(W), col(W), row(LANES), col(LANES)],
        out_shape=[jax.ShapeDtypeStruct((B, W, Lp), BF16), jax.ShapeDtypeStruct((B, Lp, W), BF16),
                   jax.ShapeDtypeStruct((B, W, Lp), BF16), jax.ShapeDtypeStruct((B, W, Lp), BF16),
                   jax.ShapeDtypeStruct((B, Lp, LANES), BF16), jax.ShapeDtypeStruct((B, LANES, Lp), F32)],
        compiler_params=_params("parallel", "arbitrary"),
        name="dsa_prep",
    )(u_a, qg, wuq, wiq, kng, knb, cf, sf, cp, sp)


def _fold_rows(x, op=jnp.add):
    parts = [x[8 * r:8 * r + 8] for r in range(x.shape[0] // 8)]
    while len(parts) > 1:
        parts = [op(a, b) for a, b in zip(parts[0::2], parts[1::2])] + parts[len(parts) & ~1:]
    return parts[0]


def _dsa_attn_body(qt_ref, qit_ref, wt_ref, k_ref, vt_ref, kx_ref, o_ref, khi_ref, klo_ref, sc_ref, acc_ref,
                   j_ref, s_ref, pe_ref, *, ksel):
    i = pl.program_id(1)
    nkc = (i * TQ + TQ - 1) // KC + 1
    kf = float(ksel)
    tcol = i * TQ + lax.broadcasted_iota(I32, (1, TQ), 1)
    int_min = jnp.int32(INT_MIN)
    row_in_pair = lax.broadcasted_iota(I32, (LANES, 1), 0)

    def head_operands(ref):
        out = []
        for h in range(N_HEADS):
            pair = ref[0, LANES * (h // 2):LANES * (h // 2 + 1), :]
            keep = (row_in_pair < HEAD_DIM) if h % 2 == 0 else (row_in_pair >= HEAD_DIM)
            out.append(jnp.where(keep, pair, jnp.zeros_like(pair)))
        return out

    def key_rows(ks):
        return ks + lax.broadcasted_iota(I32, (KC, 1), 0)

    qis = head_operands(qit_ref)
    wrows = [wt_ref[0, IDX_DIM + h:IDX_DIM + h + 1, :] for h in range(IDX_HEADS)]

    def score_chunk(kc, carry):
        lo8, hi8 = carry
        ks = pl.multiple_of(kc * KC, KC)
        kx = kx_ref[0, pl.ds(ks, KC), :]
        sc = jnp.zeros((KC, TQ), F32)
        for h in range(IDX_HEADS):
            sc = sc + jnp.maximum(_mm(kx, qis[h]), 0.0) * wrows[h]
        sc = sc + 0.0
        krow = key_rows(ks)
        sc = jnp.where(krow >= OFF, sc, MASKED)
        sc = jnp.where(krow <= tcol, sc, MASKED)
        sc_ref[pl.ds(ks, KC), :] = sc
        bits = lax.bitcast_convert_type(sc, I32)
        key = jnp.where(bits >= 0, bits, bits ^ jnp.int32(0x7FFFFFFF))
        khi_ref[pl.ds(ks, KC), :] = lax.shift_right_arithmetic(key, 16).astype(I16)
        klo_ref[pl.ds(ks, KC), :] = ((key & jnp.int32(0xFFFF)) - HALF16).astype(I16)
        lo8 = jnp.minimum(lo8, _fold_rows(jnp.where(sc <= MASKED, ABOVE_ALL, sc), jnp.minimum))
        hi8 = jnp.maximum(hi8, _fold_rows(sc, jnp.maximum))
        return lo8, hi8

    lo8, hi8 = lax.fori_loop(0, nkc, score_chunk,
                             (jnp.full((8, TQ), ABOVE_ALL, F32), jnp.full((8, TQ), MASKED, F32)))
    smin = jnp.min(lo8, axis=0, keepdims=True)
    smax = jnp.max(hi8, axis=0, keepdims=True)

    def scan_chunks(fn, init):
        def body(kc, carry):
            ks = pl.multiple_of(kc * KC, KC)
            return fn(carry, ks)
        return lax.fori_loop(0, nkc, body, init)

    zeros8 = jnp.zeros((8, TQ), F32)

    def count_where(ref, pred):
        cnt = scan_chunks(lambda c, ks: c + _fold_rows(pred(ref[pl.ds(ks, KC), :], ks)), zeros8)
        return jnp.sum(cnt, axis=0, keepdims=True)

    def count16(ref, pred):
        def fn(cnt, ks):
            m = pred(ref[pl.ds(ks, KC), :])
            parts = [m[16 * r:16 * r + 16] for r in range(KC // 16)]
            while len(parts) > 1:
                parts = [a + b for a, b in zip(parts[0::2], parts[1::2])]
            return cnt + parts[0]
        cnt = scan_chunks(fn, jnp.zeros((16, TQ), I16))
        return jnp.sum(cnt.astype(I32), axis=0, keepdims=True)

    one16, zero16 = jnp.int16(1), jnp.int16(0)

    def radix16(ref, target):
        def bit(bi, prefix):
            cand = prefix | lax.shift_left(jnp.int32(1), 15 - bi)
            cand16 = (cand - HALF16).astype(I16)
            cnt = count16(ref, lambda x: jnp.where(x >= cand16, one16, zero16))
            return jnp.where(cnt >= target, cand, prefix)
        return lax.fori_loop(0, 16, bit, jnp.zeros((1, TQ), I32))

    k_int = jnp.full((1, TQ), ksel, I32)
    thr_hi = radix16(khi_ref, k_int) - HALF16
    thr_hi16 = thr_hi.astype(I16)
    above = count16(khi_ref, lambda x: jnp.where(x > thr_hi16, one16, zero16))

    def keep_low_of_ties(carry, ks):
        rows = pl.ds(ks, KC)
        klo_ref[rows, :] = jnp.where(khi_ref[rows, :] == thr_hi16, klo_ref[rows, :], jnp.int16(-HALF16))
        return carry

    scan_chunks(keep_low_of_ties, 0)
    thr_lo = radix16(klo_ref, k_int - above)
    thr_key = lax.shift_left(thr_hi, 16) | thr_lo
    cand0 = lax.bitcast_convert_type(jnp.where(thr_key >= 0, thr_key, thr_key ^ jnp.int32(0x7FFFFFFF)), F32)

    n_adm = jnp.maximum(tcol - (OFF - 1), 0).astype(F32)
    searching = n_adm > kf

    def probe(mid):
        def fn(carry, ks):
            cnt, vmin = carry
            s = sc_ref[pl.ds(ks, KC), :]
            ge = s >= mid
            return (cnt + _fold_rows(jnp.where(ge, 1.0, 0.0)),
                    jnp.minimum(vmin, _fold_rows(jnp.where(ge, s, ABOVE_ALL), jnp.minimum)))
        cnt, vmin = scan_chunks(fn, (zeros8, jnp.full((8, TQ), ABOVE_ALL, F32)))
        return jnp.sum(cnt, axis=0, keepdims=True), jnp.min(vmin, axis=0, keepdims=True)

    def refine(state):
        it, lo, hi, c_lo, c_gt, done, _ = state
        mid = jnp.where(it == 0, jnp.where(searching, cand0, lo), lo + 0.5 * (hi - lo))
        c_mid, v_mid = probe(mid)
        up = c_mid >= kf
        lo_n = jnp.where(up, v_mid, lo)
        hi_n = jnp.where(up, hi, mid)
        c_lo_n = jnp.where(up, c_mid, c_lo)
        c_gt_n = count_where(sc_ref, lambda s, ks: jnp.where(s > lo_n, 1.0, 0.0))
        stalled = jnp.where(it > 0, jnp.where(mid <= lo, 1.0, jnp.where(mid >= hi, 1.0, 0.0)), 0.0)
        fin = jnp.maximum(jnp.where(c_gt_n < kf, 1.0, 0.0), stalled)
        frozen = done > 0.0
        keep = lambda old, new_: jnp.where(frozen, old, new_)
        done_n = jnp.maximum(done, fin)
        return (it + 1, keep(lo, lo_n), keep(hi, hi_n), keep(c_lo, c_lo_n), keep(c_gt, c_gt_n), done_n,
                jnp.max(1.0 - done_n))

    done0 = jnp.where(searching, 0.0, 1.0)
    state0 = (jnp.int32(0), smin, smax + (jnp.abs(smax) + 1.0) * 1e-6, n_adm, n_adm, done0, jnp.max(1.0 - done0))
    state = lax.while_loop(lambda st: jnp.logical_and(st[6] > 0.0, st[0] < MAX_REFINE), refine, state0)
    thr = jnp.where(searching, state[1], BELOW_ALL)
    cnt_gt = jnp.where(searching, state[4], n_adm)
    cnt_eq = jnp.where(searching, state[3] - state[4], 0.0)
    need = kf - cnt_gt

    j_ref[...] = jnp.full(j_ref.shape, 2 ** 30, I32)

    @pl.when(jnp.max(cnt_eq - need) > 0.0)
    def _():
        def index_bit(bi, prefix):
            cand = prefix | lax.shift_left(jnp.int32(1), 12 - bi)
            before = count_where(
                sc_ref, lambda s, ks: jnp.where(s == thr, jnp.where(key_rows(ks) < cand, 1.0, 0.0), 0.0))
            return jnp.where(before < need, cand, prefix)
        jst = lax.fori_loop(0, 13, index_bit, jnp.zeros((1, TQ), I32))
        j_ref[...] = jnp.broadcast_to(jst, j_ref.shape)

    jstar = j_ref[0:1, :]

    qs = head_operands(qt_ref)
    acc_ref[...] = jnp.zeros_like(acc_ref)

    ones_rows = jnp.ones((16, KC), BF16)

    def attend_chunk(kc, carry):
        ms, ls = carry
        ks = pl.multiple_of(kc * KC, KC)
        sc = sc_ref[pl.ds(ks, KC), :]
        tie = jnp.where(sc == thr, jnp.where(key_rows(ks) <= jstar, 0.0, NEG), NEG)
        bias = jnp.where(sc > thr, 0.0, tie)
        chunk_max = []
        for h in range(N_HEADS):
            p = h // 2
            kp = k_ref[0, pl.ds(ks, KC), LANES * p:LANES * (p + 1)]
            s = _mm(kp, qs[h]) + bias
            s_ref[h] = s
            chunk_max.append(jnp.max(s, axis=0, keepdims=True))
        new_ms, new_ls, alphas = [], [], []
        for h in range(N_HEADS):
            m_new = jnp.maximum(ms[h], chunk_max[h])
            alphas.append(jnp.exp(ms[h] - m_new))
            new_ms.append(m_new)
            pe_ref[h] = jnp.exp(s_ref[h] - m_new).astype(BF16)
        for h in range(N_HEADS):
            vt = vt_ref[0, HEAD_DIM * h:HEAD_DIM * (h + 1), pl.ds(ks, KC)]
            pv = _mm(jnp.concatenate([vt, ones_rows], axis=0), pe_ref[h])
            rows = slice(HEAD_DIM * h, HEAD_DIM * (h + 1))
            acc_ref[rows, :] = acc_ref[rows, :] * alphas[h] + pv[0:HEAD_DIM]
            new_ls.append(alphas[h] * ls[h] + pv[HEAD_DIM:HEAD_DIM + 1])
        return tuple(new_ms), tuple(new_ls)

    init = (tuple(jnp.full((1, TQ), NEG, F32) for _ in range(N_HEADS)),
            tuple(jnp.zeros((1, TQ), F32) for _ in range(N_HEADS)))
    _, ls = lax.fori_loop(0, nkc, attend_chunk, init)
    for p in range(N_HEADS // 2):
        parts = [acc_ref[HEAD_DIM * h:HEAD_DIM * (h + 1), :] / ls[h] for h in (2 * p, 2 * p + 1)]
        o_ref[0, :, LANES * p:LANES * (p + 1)] = jnp.concatenate(parts, axis=0).T


def _dsa_attn(qt, k, vt, qit, kx, wt, ksel):
    B, Lp, W = k.shape
    assert Lp % KC == 0 and Lp <= 8192
    qcol = lambda height: pl.BlockSpec((1, height, TQ), lambda b_, i: (b_, 0, i))
    full = lambda shape: pl.BlockSpec((1,) + shape, lambda b_, i: (b_, 0, 0))
    return pl.pallas_call(
        functools.partial(_dsa_attn_body, ksel=ksel),
        grid=(B, Lp // TQ),
        in_specs=[qcol(W), qcol(W), qcol(LANES), full((Lp, W)), full((W, Lp)), full((Lp, LANES))],
        out_specs=pl.BlockSpec((1, TQ, W), lambda b_, i: (b_, i, 0)),
        out_shape=jax.ShapeDtypeStruct((B, Lp, W), F32),
        scratch_shapes=[
            pltpu.VMEM((Lp, TQ), I16),
            pltpu.VMEM((Lp, TQ), I16),
            pltpu.VMEM((Lp, TQ), F32),
            pltpu.VMEM((W, TQ), F32),
            pltpu.VMEM((8, TQ), I32),
            pltpu.VMEM((N_HEADS, KC, TQ), F32),
            pltpu.VMEM((N_HEADS, KC, TQ), BF16),
        ],
        compiler_params=_params("parallel", "arbitrary"),
        name="dsa_attn",
    )(qt, qit, wt, k, vt, kx)


def _outproj_router_body(x_ref, yr_ref, ya_ref, eg_ref, eb_ref, wo_ref, g1_ref, b1_ref, wrh_ref, wrl_ref, br_ref,
                         h_ref, grp_ref):
    h0 = _layer_norm(x_ref[0], eg_ref[...], eb_ref[...])
    mix = (_mm(yr_ref[0].astype(BF16), wo_ref[0:RWKV_WIDTH, :])
           + _mm(ya_ref[0].astype(BF16), wo_ref[RWKV_WIDTH:, :]))
    h1 = _layer_norm(DN_ALPHA * h0 + mix, g1_ref[...], b1_ref[...])
    h_ref[0, :, 0:D_MODEL] = h1
    logits = _dot3(_split(h1), (wrh_ref[...], wrl_ref[...]), _NN) + br_ref[...]
    lane = lax.broadcasted_iota(I32, (1, LANES), 1)
    lanef = lane.astype(F32)
    low = -3e38
    lgm = jnp.where(lane < N_GROUPS, logits, low)
    gmax = jnp.max(lgm, axis=1, keepdims=True)
    gsel = jnp.min(jnp.where(lgm == gmax, lanef, 1e9), axis=1, keepdims=True)
    gsum = jnp.sum(jnp.where(lane < N_GROUPS, jnp.exp(lgm - gmax), 0.0), axis=1, keepdims=True)
    group_of_lane = ((lane - EXPERT_LANE0) // EXPERTS_PER_GROUP).astype(F32)
    lem = jnp.where(group_of_lane == gsel, logits, low)
    m1 = jnp.max(lem, axis=1, keepdims=True)
    i1 = jnp.min(jnp.where(lem == m1, lanef, 1e9), axis=1, keepdims=True)
    lem2 = jnp.where(lanef == i1, low, lem)
    m2 = jnp.max(lem2, axis=1, keepdims=True)
    i2 = jnp.min(jnp.where(lem2 == m2, lanef, 1e9), axis=1, keepdims=True)
    e2 = jnp.exp(m2 - m1)
    w1 = 1.0 / (1.0 + e2)
    w2 = e2 / (1.0 + e2)
    gates = jnp.where(lanef == i1, w1, jnp.where(lanef == i2, w2, 0.0)) / gsum
    h_ref[0, :, D_MODEL:] = gates
    grp_ref[0] = jnp.broadcast_to(gsel, (ROWS_A, LANES))


def _outproj_router(x, y_r, y_a, eg, eb, wo, g1, b1, wrh, wrl, br):
    B, S, D = x.shape
    skip = PAD_ROWS // ROWS_A
    xrow = pl.BlockSpec((1, ROWS_A, D), lambda b_, i: (b_, i, 0))
    yrow = pl.BlockSpec((1, ROWS_A, RWKV_WIDTH), lambda b_, i: (b_, i + skip, 0))
    vec = _const_spec((1, D))
    return pl.pallas_call(
        _outproj_router_body,
        grid=(B, S // ROWS_A),
        in_specs=[xrow, yrow, yrow, vec, vec, _const_spec((D, D)), vec, vec,
                  _const_spec((D, LANES)), _const_spec((D, LANES)), _const_spec((1, LANES))],
        out_specs=[pl.BlockSpec((1, ROWS_A, D + LANES), lambda b_, i: (b_, i, 0)),
                   pl.BlockSpec((1, ROWS_A, LANES), lambda b_, i: (b_, i, 0))],
        out_shape=[jax.ShapeDtypeStruct((B, S, D + LANES), F32), jax.ShapeDtypeStruct((B, S, LANES), F32)],
        compiler_params=_params("parallel", "arbitrary"),
        name="outproj_router",
    )(x, y_r, y_a, eg, eb, wo, g1, b1, wrh, wrl, br)


def _moe_body(tgrp_ref, tcnt_ref, idx_ref, hx_ref, wg_ref, wu_ref, wd_ref, g2_ref, b2_ref, out_ref,
              xg_ref, acc_ref, hb_ref, ob_ref, gsem, ssem):
    i = pl.program_id(0)
    e = pl.program_id(1)
    n = tcnt_ref[i]
    D = D_MODEL

    def row_gather(r):
        return pltpu.make_async_copy(hx_ref.at[pl.ds(idx_ref[0, 0, r], 1)], xg_ref.at[pl.ds(r, 1)], gsem)

    def row_scatter(r):
        return pltpu.make_async_copy(ob_ref.at[pl.ds(r, 1)], out_ref.at[pl.ds(idx_ref[0, 0, r], 1)], ssem)

    def for_rows(fn):
        lax.fori_loop(0, n, lambda r, c: (fn(r), c)[1], 0)

    @pl.when(jnp.logical_and(e == 0, n > 0))
    def _():
        for_rows(lambda r: row_gather(r).start())
        for_rows(lambda r: row_gather(r).wait())
        acc_ref[...] = jnp.zeros_like(acc_ref)
        hb_ref[...] = xg_ref[:, 0:D].astype(BF16)

    @pl.when(n > 0)
    def _():
        t = hb_ref[...]
        lane = lax.broadcasted_iota(I32, (1, LANES), 1)
        gate_lane = EXPERT_LANE0 + tgrp_ref[i] * EXPERTS_PER_GROUP + e
        gcol = jnp.sum(jnp.where(lane == gate_lane, xg_ref[:, D:], 0.0), axis=1, keepdims=True)
        a = _mm(t, wg_ref[0])
        hid = a * _sigmoid(a) * _mm(t, wu_ref[0]) * gcol
        acc_ref[...] += _mm(hid.astype(BF16), wd_ref[0])

    @pl.when(jnp.logical_and(e == EXPERTS_PER_GROUP - 1, n > 0))
    def _():
        ob_ref[...] = _layer_norm(DN_ALPHA * xg_ref[:, 0:D] + acc_ref[...], g2_ref[...], b2_ref[...])
        for_rows(lambda r: row_scatter(r).start())
        for_rows(lambda r: row_scatter(r).wait())


def _moe(hx, tile_group, tile_count, tile_rows, wg, wu, wd, g2, b2):
    T, DX = hx.shape
    D = D_MODEL
    n_tiles = tile_rows.shape[0]
    wspec = lambda shape: pl.BlockSpec(
        (1,) + shape, lambda i, e, tg, tc: (tg[i] * EXPERTS_PER_GROUP + e, 0, 0))
    vec = pl.BlockSpec((1, D), lambda i, e, tg, tc: (0, 0))
    return pl.pallas_call(
        _moe_body,
        grid_spec=pltpu.PrefetchScalarGridSpec(
            num_scalar_prefetch=2,
            grid=(n_tiles, EXPERTS_PER_GROUP),
            in_specs=[pl.BlockSpec((1, 1, TMG), lambda i, e, tg, tc: (i, 0, 0), memory_space=pltpu.SMEM),
                      pl.BlockSpec(memory_space=pl.ANY),
                      wspec((D, D_EXPERT)), wspec((D, D_EXPERT)), wspec((D_EXPERT, D)), vec, vec],
            out_specs=pl.BlockSpec(memory_space=pl.ANY),
            scratch_shapes=[pltpu.VMEM((TMG, DX), F32), pltpu.VMEM((TMG, D), F32), pltpu.VMEM((TMG, D), BF16),
                            pltpu.VMEM((TMG, D), F32), pltpu.SemaphoreType.DMA, pltpu.SemaphoreType.DMA],
        ),
        out_shape=jax.ShapeDtypeStruct((T, D), F32),
        compiler_params=_params("arbitrary", "arbitrary"),
        name="moe",
    )(tile_group, tile_count, tile_rows, hx, wg, wu, wd, g2, b2)


def _group_tiles(grp, n_tiles):
    T = grp.shape[0]
    onehot = (grp[:, None] == jnp.arange(N_GROUPS)[None, :]).astype(I32)
    rank = jnp.cumsum(onehot, axis=0) - onehot
    count = jnp.sum(onehot, axis=0)
    tiles_per_group = (count + TMG - 1) // TMG
    first_tile = jnp.cumsum(tiles_per_group) - tiles_per_group
    pos = first_tile[grp] * TMG + jnp.sum(rank * onehot, axis=1)
    tile_rows = jnp.zeros((n_tiles * TMG,), I32).at[pos].set(jnp.arange(T, dtype=I32)).reshape(n_tiles, 1, TMG)
    tile = jnp.arange(n_tiles)
    tile_group = jnp.clip(jnp.sum((tile[:, None] >= first_tile[None, :]).astype(I32), axis=1) - 1, 0, N_GROUPS - 1)
    in_group = tile - first_tile[tile_group]
    tile_count = jnp.clip(count[tile_group] - in_group * TMG, 0, TMG)
    tile_count = jnp.where(in_group < tiles_per_group[tile_group], tile_count, 0)
    return tile_group.astype(I32), tile_count.astype(I32), tile_rows


def _rope_tables(Lp):
    pos = jnp.maximum(jnp.arange(Lp, dtype=I32) - OFF, 0).astype(F32)
    j = jnp.arange(LANES) % HEAD_DIM

    def table(half, rot_dim):
        inv = 1.0 / (ROPE_THETA ** (jnp.arange(half, dtype=F32) / half))
        ang = pos[:, None] * inv[None, :]
        cos, sin = jnp.cos(ang)[:, j % half], jnp.sin(ang)[:, j % half]
        rotated = (j < rot_dim)[None, :]
        sign = jnp.where(j < half, -1.0, 1.0)[None, :]
        return jnp.where(rotated, cos, 1.0), jnp.where(rotated, sin * sign, 0.0)

    cf, sf = table(HEAD_DIM // 2, HEAD_DIM)
    cp, sp = table(IDX_ROPE_DIM // 2, IDX_ROPE_DIM)
    return cf, sf, cp, sp


def _block_ones(n, block):
    idx = jnp.arange(n) // block
    return (idx[:, None] == idx[None, :]).astype(F32)


def kernel(x, meta_tokens, ln_emb_g, ln_emb_b, w_in, rw_mu, rw_w0, rw_w2, rw_a0, rw_a2, rw_g2, rw_kk, rw_ka,
           rw_rk, rw_lnx_g, rw_lnx_b, att_qnorm_g, att_wuq, idx_wq, idx_knorm_g, idx_knorm_b, w_out, ln1_g,
           ln1_b, rt_grp_w, rt_grp_b, rt_exp_w, rt_exp_b, ex_w_gate, ex_w_up, ex_w_down, ln2_g, ln2_b):
    B, S, D = x.shape
    assert w_in.shape[0] == 1 and D == D_MODEL and (B * S) % TMG == 0
    Lp = S + PAD_ROWS
    ksel = min(INDEX_TOPK, S // 4)
    row = lambda t: t.reshape(1, -1)
    W = RWKV_WIDTH

    meta_pad = jnp.zeros((PAD_ROWS, D), F32).at[OFF:].set(meta_tokens)
    w_in_p = jnp.pad(w_in[0], ((0, 0), (0, ATT_COLS_PAD - ATT_COLS))).astype(BF16)
    u_r, u_a = _ln_inproj(x, meta_pad, row(ln_emb_g), row(ln_emb_b), w_in_p)

    w2p = jnp.concatenate([rw_w2[0], jnp.zeros((ICLR_RANK, W), F32)], 0).astype(BF16)
    a2p = jnp.concatenate([jnp.zeros((DECAY_RANK, W), F32), rw_a2[0]], 0).astype(BF16)
    head_of_lane = jnp.arange(W) // HEAD_DIM
    hs = (head_of_lane[:, None] == jnp.arange(LANES)[None, :]).astype(BF16)
    tri = (jnp.arange(ROWS_R)[:, None] >= jnp.arange(ROWS_R)[None, :]).astype(F32)
    tri = (tri * _block_ones(ROWS_R, CHUNK)).astype(BF16)
    rp, kp, bp, ap, v, pc, g, bonus = _rwkv_prep(
        u_r, row(rw_mu[0]), row(rw_w0[0]), w2p, row(rw_a0[0]), a2p, rw_g2[0].astype(BF16), row(rw_kk[0]),
        row(rw_ka[0]), row(rw_rk[0]), hs, hs.T, tri)
    y_r = _rwkv_scan(rp, kp, bp, ap, v, pc, g, bonus, row(rw_lnx_g[0]), row(rw_lnx_b[0]))

    pad_lanes = lambda t: jnp.pad(t, (0, LANES - t.shape[0])).reshape(1, LANES)
    cf, sf, cp, sp = _rope_tables(Lp)
    qt, k, vt, qit, kx, wt = _dsa_prep(
        u_a, row(att_qnorm_g[0]), att_wuq[0].astype(BF16), idx_wq[0].astype(BF16),
        pad_lanes(idx_knorm_g[0]), pad_lanes(idx_knorm_b[0]), cf, sf, cp, sp)
    y_a = _dsa_attn(qt, k, vt, qit, kx, wt, ksel)

    wr = jnp.zeros((D, LANES), F32).at[:, :N_GROUPS].set(rt_grp_w[0])
    wr = wr.at[:, EXPERT_LANE0:EXPERT_LANE0 + N_EXPERTS].set(rt_exp_w[0])
    br = jnp.zeros((1, LANES), F32).at[0, :N_GROUPS].set(rt_grp_b[0])
    br = br.at[0, EXPERT_LANE0:EXPERT_LANE0 + N_EXPERTS].set(rt_exp_b[0])
    wrh = wr.astype(BF16)
    wrl = (wr - wrh.astype(F32)).astype(BF16)
    h1, grp = _outproj_router(x, y_r, y_a, row(ln_emb_g), row(ln_emb_b), w_out[0].astype(BF16),
                                row(ln1_g[0]), row(ln1_b[0]), wrh, wrl, br)
    T = B * S
    n_tiles = T // TMG + N_GROUPS
    tile_group, tile_count, tile_rows = _group_tiles(grp.reshape(T, LANES)[:, 0].astype(I32), n_tiles)
    out = _moe(h1.reshape(T, D + LANES), tile_group, tile_count, tile_rows, ex_w_gate[0].astype(BF16),
               ex_w_up[0].astype(BF16), ex_w_down[0].astype(BF16), row(ln2_g[0]), row(ln2_b[0]))
    return out.reshape(B, S, D)
```

```python
import functools

import jax
import jax.numpy as jnp
from jax import lax
from jax.experimental import pallas as pl
from jax.experimental.pallas import tpu as pltpu

F32 = jnp.float32
BF16 = jnp.bfloat16
I32 = jnp.int32
I16 = jnp.int16
HIGHEST = lax.Precision.HIGHEST

D_MODEL = 1024
N_META = 16
RWKV_WIDTH = 512
ATT_WIDTH = 512
HEAD_DIM = 64
N_HEADS = 8
DECAY_RANK = 64
ICLR_RANK = 64
GATE_RANK = 128
Q_LORA_RANK = 256
IDX_HEADS = 8
IDX_DIM = 64
IDX_ROPE_DIM = 32
INDEX_TOPK = 256
ROPE_THETA = 10000.0
N_GROUPS = 4
EXPERTS_PER_GROUP = 8
N_EXPERTS = N_GROUPS * EXPERTS_PER_GROUP
D_EXPERT = 256
DN_ALPHA = 2.0 ** 0.25
LN_EPS = 1e-5
RMS_EPS = 1e-6
GN_EPS = 64e-5
RWKV_COLS = 3 * RWKV_WIDTH + DECAY_RANK + ICLR_RANK + GATE_RANK
ATT_COLS = Q_LORA_RANK + 2 * ATT_WIDTH + IDX_DIM + IDX_HEADS
ATT_COLS_PAD = 1408

LANES = 128
PAD_ROWS = 256
OFF = PAD_ROWS - N_META
CHUNK = 64
ROWS_A = 256
ROWS_R = 256
TQ = 256
KC = 256
TMG = 512
SCAN_BATCH = 4
EXPERT_LANE0 = 64
NEG = -1e30
MASKED = -3e38
BELOW_ALL = -1e38
ABOVE_ALL = 3e38
MAX_REFINE = 400
INT_MIN = -2147483648
HALF16 = 32768
VMEM_LIMIT = 56 * 1024 * 1024


def _mm(a, b, precision=None):
    return jnp.dot(a, b, preferred_element_type=F32, precision=precision)


def _mm_nt(a, b, precision=None):
    return lax.dot_general(a, b, (((1,), (1,)), ((), ())), preferred_element_type=F32, precision=precision)


def _mm_tn(a, b, precision=None):
    return lax.dot_general(a, b, (((0,), (0,)), ((), ())), preferred_element_type=F32, precision=precision)


def _sigmoid(x):
    return 1.0 / (1.0 + jnp.exp(-x))


def _layer_norm(x, g, b):
    mu = jnp.mean(x, -1, keepdims=True)
    xc = x - mu
    var = jnp.mean(xc * xc, -1, keepdims=True)
    return xc * lax.rsqrt(var + LN_EPS) * g + b


def _params(*sem):
    return pltpu.CompilerParams(dimension_semantics=sem, vmem_limit_bytes=VMEM_LIMIT)


def _const_spec(shape):
    nd = len(shape)
    return pl.BlockSpec(shape, lambda *_: (0,) * nd)


def _ln_inproj_body(x_ref, meta_ref, g_ref, b_ref, w_ref, ur_ref, ua_ref):
    blk = pl.program_id(1)
    xin = jnp.where(blk == 0, meta_ref[...], x_ref[0])
    h = _layer_norm(xin, g_ref[...], b_ref[...])
    row = lax.broadcasted_iota(I32, (ROWS_A, 1), 0)
    h = jnp.where((blk > 0) | (row >= OFF), h, 0.0)
    hb = h.astype(BF16)
    step = 256
    for n0 in range(0, RWKV_COLS, step):
        n1 = min(n0 + step, RWKV_COLS)
        ur_ref[0, :, n0:n1] = _mm(hb, w_ref[:, n0:n1])
    for n0 in range(0, ATT_COLS_PAD, step):
        n1 = min(n0 + step, ATT_COLS_PAD)
        ua_ref[0, :, n0:n1] = _mm(hb, w_ref[:, RWKV_COLS + n0:RWKV_COLS + n1])


def _ln_inproj(x, meta_pad, g, b, w):
    B, S, D = x.shape
    nblk = (S + PAD_ROWS) // ROWS_A
    Lp = S + PAD_ROWS
    ncols = RWKV_COLS + ATT_COLS_PAD
    return pl.pallas_call(
        _ln_inproj_body,
        grid=(B, nblk),
        in_specs=[
            pl.BlockSpec((1, ROWS_A, D), lambda b_, i: (b_, jnp.maximum(i - 1, 0), 0)),
            _const_spec((ROWS_A, D)),
            _const_spec((1, D)),
            _const_spec((1, D)),
            _const_spec((D, ncols)),
        ],
        out_specs=[
            pl.BlockSpec((1, ROWS_A, RWKV_COLS), lambda b_, i: (b_, i, 0)),
            pl.BlockSpec((1, ROWS_A, ATT_COLS_PAD), lambda b_, i: (b_, i, 0)),
        ],
        out_shape=[
            jax.ShapeDtypeStruct((B, Lp, RWKV_COLS), F32),
            jax.ShapeDtypeStruct((B, Lp, ATT_COLS_PAD), F32),
        ],
        compiler_params=_params("parallel", "arbitrary"),
        name="ln_inproj",
    )(x, meta_pad, g, b, w)


def _split3(x):
    hi = x.astype(BF16)
    r1 = x - hi.astype(F32)
    mid = r1.astype(BF16)
    return hi, mid, (r1 - mid.astype(F32)).astype(BF16)


def _mm_exact_rhs(x, m):
    return sum(_mm(p, m) for p in _split3(x))


def _mm_exact_lhs(m, x):
    return sum(_mm(m, p) for p in _split3(x))


def _rwkv_prep_body(u_ref, prev_ref, mu_ref, w0_ref, w2_ref, a0_ref, a2_ref, g2_ref, kk_ref, ka_ref, rk_ref,
                    hs_ref, hb_ref, tri_ref,
                    rp_ref, kp_ref, bp_ref, ap_ref, v_ref, pc_ref, g_ref, bonus_ref):
    blk = pl.program_id(1)
    u = u_ref[0]
    prev = jnp.where(blk == 0, 0.0, prev_ref[0][7:8, :])
    row = lax.broadcasted_iota(I32, (ROWS_R, 1), 0)
    shifted = jnp.where(row == 0, prev, pltpu.roll(u, 1, 0))
    ul = u + (shifted - u) * mu_ref[...]
    W = RWKV_WIDTH
    r = ul[:, 0:W]
    k = ul[:, W:2 * W]
    v = ul[:, 2 * W:3 * W]
    wa = ul[:, 3 * W:3 * W + 128]
    gd = ul[:, 3 * W + 128:3 * W + 256]
    w = w0_ref[...] + _mm(jnp.tanh(wa).astype(BF16), w2_ref[...])
    softplus_neg_w = jnp.maximum(-w, 0.0) + jnp.log(1.0 + jnp.exp(-jnp.abs(w)))
    logd = -jnp.exp(-softplus_neg_w - 0.5)
    a = _sigmoid(a0_ref[...] + _mm(wa.astype(BF16), a2_ref[...]))
    g_ref[0] = _mm(_sigmoid(gd).astype(BF16), g2_ref[...])
    head_sum = lambda t: _mm_exact_rhs(_mm_exact_rhs(t, hs_ref[...]), hb_ref[...])
    kkr = k * kk_ref[...]
    kk = kkr / jnp.maximum(jnp.sqrt(head_sum(kkr * kkr)), 1e-12)
    kmod = k * (1.0 + (a - 1.0) * ka_ref[...])
    bonus_ref[0] = head_sum(r * kmod * rk_ref[...]) * v
    v_ref[0] = v
    cum = _mm_exact_lhs(tri_ref[...], logd)
    rp_ref[0] = r * jnp.exp(cum)
    einv = jnp.exp(-cum)
    kp_ref[0] = kmod * einv
    bp_ref[0] = kk * a * einv
    ap_ref[0] = -kk * jnp.exp(cum - logd)
    for c in range(ROWS_R // CHUNK):
        last = c * CHUNK + CHUNK - 1
        pc_ref[0, c] = jnp.broadcast_to(jnp.exp(cum[last:last + 1]), (8, W))


def _rwkv_prep(u_r, mu, w0, w2p, a0, a2p, g2, k_k, k_a, r_k, hs, hb, tri):
    B, Lp, _ = u_r.shape
    nblk = Lp // ROWS_R
    W = RWKV_WIDTH
    row_spec = pl.BlockSpec((1, ROWS_R, W), lambda b_, i: (b_, i, 0))
    row_shape = jax.ShapeDtypeStruct((B, Lp, W), F32)
    cpb = ROWS_R // CHUNK
    return pl.pallas_call(
        _rwkv_prep_body,
        grid=(B, nblk),
        in_specs=[
            pl.BlockSpec((1, ROWS_R, RWKV_COLS), lambda b_, i: (b_, i, 0)),
            pl.BlockSpec((1, 8, RWKV_COLS), lambda b_, i: (b_, jnp.maximum(i * (ROWS_R // 8) - 1, 0), 0)),
            _const_spec((1, RWKV_COLS)),
            _const_spec((1, W)),
            _const_spec((128, W)),
            _const_spec((1, W)),
            _const_spec((128, W)),
            _const_spec((128, W)),
            _const_spec((1, W)),
            _const_spec((1, W)),
            _const_spec((1, W)),
            _const_spec((W, LANES)),
            _const_spec((LANES, W)),
            _const_spec((ROWS_R, ROWS_R)),
        ],
        out_specs=[row_spec] * 5 + [pl.BlockSpec((1, cpb, 8, W), lambda b_, i: (b_, i, 0, 0))] + [row_spec] * 2,
        out_shape=[row_shape] * 5 + [jax.ShapeDtypeStruct((B, Lp // CHUNK, 8, W), F32)] + [row_shape] * 2,
        compiler_params=_params("parallel", "arbitrary"),
        name="rwkv_prep",
    )(u_r, u_r, mu, w0, w2p, a0, a2p, g2, k_k, k_a, r_k, hs, hb, tri)


def _rwkv_scan_body(rp_ref, kp_ref, bp_ref, ap_ref, v_ref, pc_ref, g_ref, bonus_ref, lg_ref, lb_ref, o_ref, s_ref,
                    *, nb):
    c = pl.program_id(1)

    @pl.when(c == 0)
    def _():
        s_ref[...] = jnp.zeros_like(s_ref)

    @pl.when(c < OFF // CHUNK)
    def _():
        o_ref[...] = jnp.zeros_like(o_ref)

    @pl.when(c >= OFF // CHUNK)
    def _():
        _rwkv_chunk(rp_ref, kp_ref, bp_ref, ap_ref, v_ref, pc_ref, g_ref, bonus_ref, lg_ref, lb_ref, o_ref, s_ref, nb)


def _split(x):
    hi = x.astype(BF16)
    return hi, (x - hi.astype(F32)).astype(BF16)


def _dot3(a, b, dims):
    dg = lambda p, q: lax.dot_general(p, q, (dims, ((), ())), preferred_element_type=F32)
    return dg(a[0], b[0]) + dg(a[0], b[1]) + dg(a[1], b[0])


_NN = ((1,), (0,))
_NT = ((1,), (1,))
_TN = ((0,), (0,))


def _rwkv_chunk(rp_ref, kp_ref, bp_ref, ap_ref, v_ref, pc_ref, g_ref, bonus_ref, lg_ref, lb_ref, o_ref, s_ref, nb):
    C, N = CHUNK, HEAD_DIM
    ri = lax.broadcasted_iota(I32, (C, C), 0)
    ci = lax.broadcasted_iota(I32, (C, C), 1)
    strict = ri > ci
    incl = ri >= ci
    eye = jnp.where(ri == ci, 1.0, 0.0)
    units = [(b, slice(h * N, (h + 1) * N)) for b in range(nb) for h in range(N_HEADS)]
    ids = range(len(units))
    rows2 = lambda top, bottom: jnp.concatenate([top, bottom], axis=0)
    pcs = [pc_ref[b, 0, 0:1, sl] for b, sl in units]
    ar = [_split(rows2(ap_ref[b, :, sl], rp_ref[b, :, sl])) for b, sl in units]
    bk = [_split(rows2(bp_ref[b, :, sl], kp_ref[b, :, sl])) for b, sl in units]
    v_ = [_split(v_ref[b, :, sl]) for b, sl in units]
    s0 = [s_ref[i] for i in ids]
    s0s = [_split(s) for s in s0]
    gram = [_dot3(ar[i], bk[i], _NT) for i in ids]
    a_ab = [jnp.where(strict, gram[i][0:C, 0:C], 0.0) for i in ids]
    a_ak = [jnp.where(strict, gram[i][0:C, C:2 * C], 0.0) for i in ids]
    a_rb = [_split(jnp.where(incl, gram[i][C:2 * C, 0:C], 0.0)) for i in ids]
    a_rk = [jnp.where(incl, gram[i][C:2 * C, C:2 * C], 0.0) for i in ids]
    t = [eye + a_ab[i] for i in ids]
    pb = [a_ab[i].astype(BF16) for i in ids]
    for _ in range(C.bit_length() - 2):
        pb = [_mm(pb[i], pb[i]).astype(BF16) for i in ids]
        t = [t[i] + _mm(t[i].astype(BF16), pb[i]) for i in ids]
    ts = [_split(t[i]) for i in ids]
    resid = [(eye - t[i]) + _dot3(_split(a_ab[i]), ts[i], _NN) for i in ids]
    t = [t[i] + _mm(ts[i][0], resid[i].astype(BF16)) for i in ids]
    sp = [_dot3(ar[i], s0s[i], _NT) for i in ids]
    av = [_dot3(_split(rows2(a_ak[i], a_rk[i])), v_[i], _NN) for i in ids]
    u_ = [_dot3(_split(t[i]), _split(sp[i][0:C] + av[i][0:C]), _NN) for i in ids]
    for i, (b, sl) in enumerate(units):
        o = sp[i][C:2 * C] + av[i][C:2 * C] + _dot3(a_rb[i], _split(u_[i]), _NN)
        oc = o - jnp.mean(o, axis=1, keepdims=True)
        var = jnp.mean(oc * oc, axis=1, keepdims=True)
        y = oc * lax.rsqrt(var + GN_EPS) * lg_ref[:, sl] + lb_ref[:, sl]
        o_ref[b, :, sl] = (y + bonus_ref[b, :, sl]) * g_ref[b, :, sl]
    for i, (b, sl) in enumerate(units):
        vu = _split(rows2(v_ref[b, :, sl], u_[i]))
        kb = _split(rows2(kp_ref[b, :, sl], bp_ref[b, :, sl]) * pcs[i])
        s_ref[i] = s0[i] * pcs[i] + _dot3(vu, kb, _TN)


def _rwkv_scan(rp, kp, bp, ap, v, pc, g, bonus, lg, lb):
    B, Lp, W = rp.shape
    nch = Lp // CHUNK
    nb = SCAN_BATCH if B % SCAN_BATCH == 0 else 1
    row_spec = pl.BlockSpec((nb, CHUNK, W), lambda b_, c: (b_, c, 0))
    return pl.pallas_call(
        functools.partial(_rwkv_scan_body, nb=nb),
        grid=(B // nb, nch),
        in_specs=[row_spec] * 5 + [pl.BlockSpec((nb, 1, 8, W), lambda b_, c: (b_, c, 0, 0))] + [row_spec] * 2
        + [_const_spec((1, W))] * 2,
        out_specs=row_spec,
        out_shape=jax.ShapeDtypeStruct((B, Lp, W), F32),
        scratch_shapes=[pltpu.VMEM((nb * N_HEADS, HEAD_DIM, HEAD_DIM), F32)],
        compiler_params=_params("parallel", "arbitrary"),
        name="rwkv_scan",
    )(rp, kp, bp, ap, v, pc, g, bonus, lg, lb)


def _rope(x, cos, sin, half, first):
    width = x.shape[1]
    rot = jnp.where(first, pltpu.roll(x, width - half, 1), pltpu.roll(x, half, 1))
    return x * cos + rot * sin


def _dsa_prep_body(u_ref, qg_ref, wuq_ref, wiq_ref, kng_ref, knb_ref, cf_ref, sf_ref, cp_ref, sp_ref,
                   qt_ref, k_ref, vt_ref, qit_ref, kx_ref, wt_ref):
    u = u_ref[0]
    cq = u[:, 0:Q_LORA_RANK]
    k = u[:, Q_LORA_RANK:Q_LORA_RANK + ATT_WIDTH]
    v = u[:, Q_LORA_RANK + ATT_WIDTH:Q_LORA_RANK + 2 * ATT_WIDTH]
    tail = u[:, Q_LORA_RANK + 2 * ATT_WIDTH:]
    cqn = (cq * lax.rsqrt(jnp.mean(cq * cq, -1, keepdims=True) + RMS_EPS) * qg_ref[...]).astype(BF16)
    reps = ATT_WIDTH // LANES
    cf = jnp.concatenate([cf_ref[...]] * reps, axis=1)
    sf = jnp.concatenate([sf_ref[...]] * reps, axis=1)
    cp = jnp.concatenate([cp_ref[...]] * reps, axis=1)
    sp = jnp.concatenate([sp_ref[...]] * reps, axis=1)
    lane_w = lax.broadcasted_iota(I32, (1, ATT_WIDTH), 1) % HEAD_DIM
    first_f = lane_w < HEAD_DIM // 2
    first_p = lane_w < IDX_ROPE_DIM // 2
    q = _rope(_mm(cqn, wuq_ref[...]), cf, sf, HEAD_DIM // 2, first_f)
    qt_ref[0] = (q * (HEAD_DIM ** -0.5)).T.astype(BF16)
    qi = _rope(_mm(cqn, wiq_ref[...]), cp, sp, IDX_ROPE_DIM // 2, first_p)
    qit_ref[0] = qi.T.astype(BF16)
    k_ref[0] = _rope(k, cf, sf, HEAD_DIM // 2, first_f).astype(BF16)
    vt_ref[0] = v.T.astype(BF16)
    lane = lax.broadcasted_iota(I32, (1, LANES), 1)
    is_key = lane < IDX_DIM
    mu = jnp.sum(jnp.where(is_key, tail, 0.0), -1, keepdims=True) * (1.0 / IDX_DIM)
    tc = jnp.where(is_key, tail - mu, 0.0)
    var = jnp.sum(tc * tc, -1, keepdims=True) * (1.0 / IDX_DIM)
    kn = tc * lax.rsqrt(var + LN_EPS) * kng_ref[...] + knb_ref[...]
    kn = _rope(kn, cp_ref[...], sp_ref[...], IDX_ROPE_DIM // 2, (lane % HEAD_DIM) < IDX_ROPE_DIM // 2)
    kn = kn * (IDX_DIM ** -0.5)
    kx_ref[0] = jnp.where(is_key, kn, pltpu.roll(kn, IDX_DIM, 1)).astype(BF16)
    wt_ref[0] = (tail * (IDX_HEADS ** -0.5)).T


def _dsa_prep(u_a, qg, wuq, wiq, kng, knb, cf, sf, cp, sp):
    B, Lp, _ = u_a.shape
    W = ATT_WIDTH
    row = lambda width: pl.BlockSpec((1, ROWS_A, width), lambda b_, i: (b_, i, 0))
    col = lambda height: pl.BlockSpec((1, height, ROWS_A), lambda b_, i: (b_, 0, i))
    tab = pl.BlockSpec((ROWS_A, LANES), lambda b_, i: (i, 0))
    return pl.pallas_call(
        _dsa_prep_body,
        grid=(B, Lp // ROWS_A),
        in_specs=[row(ATT_COLS_PAD), _const_spec((1, Q_LORA_RANK)), _const_spec((Q_LORA_RANK, W)),
                  _const_spec((Q_LORA_RANK, W)), _const_spec((1, LANES)), _const_spec((1, LANES)),
                  tab, tab, tab, tab],
        out_specs=[col(W), row(W), col(W), col(W), row(LANES), col(LANES)],
        out_shape=[jax.ShapeDtypeStruct((B, W, Lp), BF16), jax.ShapeDtypeStruct((B, Lp, W), BF16),
                   jax.ShapeDtypeStruct((B, W, Lp), BF16), jax.ShapeDtypeStruct((B, W, Lp), BF16),
                   jax.ShapeDtypeStruct((B, Lp, LANES), BF16), jax.ShapeDtypeStruct((B, LANES, Lp), F32)],
        compiler_params=_params("parallel", "arbitrary"),
        name="dsa_prep",
    )(u_a, qg, wuq, wiq, kng, knb, cf, sf, cp, sp)


def _fold_rows(x, op=jnp.add):
    parts = [x[8 * r:8 * r + 8] for r in range(x.shape[0] // 8)]
    while len(parts) > 1:
        parts = [op(a, b) for a, b in zip(parts[0::2], parts[1::2])] + parts[len(parts) & ~1:]
    return parts[0]


def _dsa_attn_body(qt_ref, qit_ref, wt_ref, k_ref, vt_ref, kx_ref, o_ref, khi_ref, klo_ref, sc_ref, acc_ref,
                   j_ref, s_ref, pe_ref, *, ksel):
    i = pl.program_id(1)
    nkc = (i * TQ + TQ - 1) // KC + 1
    kf = float(ksel)
    tcol = i * TQ + lax.broadcasted_iota(I32, (1, TQ), 1)
    int_min = jnp.int32(INT_MIN)
    row_in_pair = lax.broadcasted_iota(I32, (LANES, 1), 0)

    def head_operands(ref):
        out = []
        for h in range(N_HEADS):
            pair = ref[0, LANES * (h // 2):LANES * (h // 2 + 1), :]
            keep = (row_in_pair < HEAD_DIM) if h % 2 == 0 else (row_in_pair >= HEAD_DIM)
            out.append(jnp.where(keep, pair, jnp.zeros_like(pair)))
        return out

    def key_rows(ks):
        return ks + lax.broadcasted_iota(I32, (KC, 1), 0)

    qis = head_operands(qit_ref)
    wrows = [wt_ref[0, IDX_DIM + h:IDX_DIM + h + 1, :] for h in range(IDX_HEADS)]

    def score_chunk(kc, carry):
        lo8, hi8 = carry
        ks = pl.multiple_of(kc * KC, KC)
        kx = kx_ref[0, pl.ds(ks, KC), :]
        sc = jnp.zeros((KC, TQ), F32)
        for h in range(IDX_HEADS):
            sc = sc + jnp.maximum(_mm(kx, qis[h]), 0.0) * wrows[h]
        sc = sc + 0.0
        krow = key_rows(ks)
        sc = jnp.where(krow >= OFF, sc, MASKED)
        sc = jnp.where(krow <= tcol, sc, MASKED)
        sc_ref[pl.ds(ks, KC), :] = sc
        bits = lax.bitcast_convert_type(sc, I32)
        key = jnp.where(bits >= 0, bits, bits ^ jnp.int32(0x7FFFFFFF))
        khi_ref[pl.ds(ks, KC), :] = lax.shift_right_arithmetic(key, 16).astype(I16)
        klo_ref[pl.ds(ks, KC), :] = ((key & jnp.int32(0xFFFF)) - HALF16).astype(I16)
        lo8 = jnp.minimum(lo8, _fold_rows(jnp.where(sc <= MASKED, ABOVE_ALL, sc), jnp.minimum))
        hi8 = jnp.maximum(hi8, _fold_rows(sc, jnp.maximum))
        return lo8, hi8

    lo8, hi8 = lax.fori_loop(0, nkc, score_chunk,
                             (jnp.full((8, TQ), ABOVE_ALL, F32), jnp.full((8, TQ), MASKED, F32)))
    smin = jnp.min(lo8, axis=0, keepdims=True)
    smax = jnp.max(hi8, axis=0, keepdims=True)

    def scan_chunks(fn, init):
        def body(kc, carry):
            ks = pl.multiple_of(kc * KC, KC)
            return fn(carry, ks)
        return lax.fori_loop(0, nkc, body, init)

    zeros8 = jnp.zeros((8, TQ), F32)

    def count_where(ref, pred):
        cnt = scan_chunks(lambda c, ks: c + _fold_rows(pred(ref[pl.ds(ks, KC), :], ks)), zeros8)
        return jnp.sum(cnt, axis=0, keepdims=True)

    def count16(ref, pred):
        def fn(cnt, ks):
            m = pred(ref[pl.ds(ks, KC), :])
            parts = [m[16 * r:16 * r + 16] for r in range(KC // 16)]
            while len(parts) > 1:
                parts = [a + b for a, b in zip(parts[0::2], parts[1::2])]
            return cnt + parts[0]
        cnt = scan_chunks(fn, jnp.zeros((16, TQ), I16))
        return jnp.sum(cnt.astype(I32), axis=0, keepdims=True)

    one16, zero16 = jnp.int16(1), jnp.int16(0)

    def radix16(ref, target):
        def bit(bi, prefix):
            cand = prefix | lax.shift_left(jnp.int32(1), 15 - bi)
            cand16 = (cand - HALF16).astype(I16)
            cnt = count16(ref, lambda x: jnp.where(x >= cand16, one16, zero16))
            return jnp.where(cnt >= target, cand, prefix)
        return lax.fori_loop(0, 16, bit, jnp.zeros((1, TQ), I32))

    k_int = jnp.full((1, TQ), ksel, I32)
    thr_hi = radix16(khi_ref, k_int) - HALF16
    thr_hi16 = thr_hi.astype(I16)
    above = count16(khi_ref, lambda x: jnp.where(x > thr_hi16, one16, zero16))

    def keep_low_of_ties(carry, ks):
        rows = pl.ds(ks, KC)
        klo_ref[rows, :] = jnp.where(khi_ref[rows, :] == thr_hi16, klo_ref[rows, :], jnp.int16(-HALF16))
        return carry

    scan_chunks(keep_low_of_ties, 0)
    thr_lo = radix16(klo_ref, k_int - above)
    thr_key = lax.shift_left(thr_hi, 16) | thr_lo
    cand0 = lax.bitcast_convert_type(jnp.where(thr_key >= 0, thr_key, thr_key ^ jnp.int32(0x7FFFFFFF)), F32)

    n_adm = jnp.maximum(tcol - (OFF - 1), 0).astype(F32)
    searching = n_adm > kf

    def probe(mid):
        def fn(carry, ks):
            cnt, vmin = carry
            s = sc_ref[pl.ds(ks, KC), :]
            ge = s >= mid
            return (cnt + _fold_rows(jnp.where(ge, 1.0, 0.0)),
                    jnp.minimum(vmin, _fold_rows(jnp.where(ge, s, ABOVE_ALL), jnp.minimum)))
        cnt, vmin = scan_chunks(fn, (zeros8, jnp.full((8, TQ), ABOVE_ALL, F32)))
        return jnp.sum(cnt, axis=0, keepdims=True), jnp.min(vmin, axis=0, keepdims=True)

    def refine(state):
        it, lo, hi, c_lo, c_gt, done, _ = state
        mid = jnp.where(it == 0, jnp.where(searching, cand0, lo), lo + 0.5 * (hi - lo))
        c_mid, v_mid = probe(mid)
        up = c_mid >= kf
        lo_n = jnp.where(up, v_mid, lo)
        hi_n = jnp.where(up, hi, mid)
        c_lo_n = jnp.where(up, c_mid, c_lo)
        c_gt_n = count_where(sc_ref, lambda s, ks: jnp.where(s > lo_n, 1.0, 0.0))
        stalled = jnp.where(it > 0, jnp.where(mid <= lo, 1.0, jnp.where(mid >= hi, 1.0, 0.0)), 0.0)
        fin = jnp.maximum(jnp.where(c_gt_n < kf, 1.0, 0.0), stalled)
        frozen = done > 0.0
        keep = lambda old, new_: jnp.where(frozen, old, new_)
        done_n = jnp.maximum(done, fin)
        return (it + 1, keep(lo, lo_n), keep(hi, hi_n), keep(c_lo, c_lo_n), keep(c_gt, c_gt_n), done_n,
                jnp.max(1.0 - done_n))

    done0 = jnp.where(searching, 0.0, 1.0)
    state0 = (jnp.int32(0), smin, smax + (jnp.abs(smax) + 1.0) * 1e-6, n_adm, n_adm, done0, jnp.max(1.0 - done0))
    state = lax.while_loop(lambda st: jnp.logical_and(st[6] > 0.0, st[0] < MAX_REFINE), refine, state0)
    thr = jnp.where(searching, state[1], BELOW_ALL)
    cnt_gt = jnp.where(searching, state[4], n_adm)
    cnt_eq = jnp.where(searching, state[3] - state[4], 0.0)
    need = kf - cnt_gt

    j_ref[...] = jnp.full(j_ref.shape, 2 ** 30, I32)

    @pl.when(jnp.max(cnt_eq - need) > 0.0)
    def _():
        def index_bit(bi, prefix):
            cand = prefix | lax.shift_left(jnp.int32(1), 12 - bi)
            before = count_where(
                sc_ref, lambda s, ks: jnp.where(s == thr, jnp.where(key_rows(ks) < cand, 1.0, 0.0), 0.0))
            return jnp.where(before < need, cand, prefix)
        jst = lax.fori_loop(0, 13, index_bit, jnp.zeros((1, TQ), I32))
        j_ref[...] = jnp.broadcast_to(jst, j_ref.shape)

    jstar = j_ref[0:1, :]

    qs = head_operands(qt_ref)
    acc_ref[...] = jnp.zeros_like(acc_ref)

    ones_rows = jnp.ones((16, KC), BF16)

    def attend_chunk(kc, carry):
        ms, ls = carry
        ks = pl.multiple_of(kc * KC, KC)
        sc = sc_ref[pl.ds(ks, KC), :]
        tie = jnp.where(sc == thr, jnp.where(key_rows(ks) <= jstar, 0.0, NEG), NEG)
        bias = jnp.where(sc > thr, 0.0, tie)
        chunk_max = []
        for h in range(N_HEADS):
            p = h // 2
            kp = k_ref[0, pl.ds(ks, KC), LANES * p:LANES * (p + 1)]
            s = _mm(kp, qs[h]) + bias
            s_ref[h] = s
            chunk_max.append(jnp.max(s, axis=0, keepdims=True))
        new_ms, new_ls, alphas = [], [], []
        for h in range(N_HEADS):
            m_new = jnp.maximum(ms[h], chunk_max[h])
            alphas.append(jnp.exp(ms[h] - m_new))
            new_ms.append(m_new)
            pe_ref[h] = jnp.exp(s_ref[h] - m_new).astype(BF16)
        for h in range(N_HEADS):
            vt = vt_ref[0, HEAD_DIM * h:HEAD_DIM * (h + 1), pl.ds(ks, KC)]
            pv = _mm(jnp.concatenate([vt, ones_rows], axis=0), pe_ref[h])
            rows = slice(HEAD_DIM * h, HEAD_DIM * (h + 1))
            acc_ref[rows, :] = acc_ref[rows, :] * alphas[h] + pv[0:HEAD_DIM]
            new_ls.append(alphas[h] * ls[h] + pv[HEAD_DIM:HEAD_DIM + 1])
        return tuple(new_ms), tuple(new_ls)

    init = (tuple(jnp.full((1, TQ), NEG, F32) for _ in range(N_HEADS)),
            tuple(jnp.zeros((1, TQ), F32) for _ in range(N_HEADS)))
    _, ls = lax.fori_loop(0, nkc, attend_chunk, init)
    for p in range(N_HEADS // 2):
        parts = [acc_ref[HEAD_DIM * h:HEAD_DIM * (h + 1), :] / ls[h] for h in (2 * p, 2 * p + 1)]
        o_ref[0, :, LANES * p:LANES * (p + 1)] = jnp.concatenate(parts, axis=0).T


def _dsa_attn(qt, k, vt, qit, kx, wt, ksel):
    B, Lp, W = k.shape
    assert Lp % KC == 0 and Lp <= 8192
    qcol = lambda height: pl.BlockSpec((1, height, TQ), lambda b_, i: (b_, 0, i))
    full = lambda shape: pl.BlockSpec((1,) + shape, lambda b_, i: (b_, 0, 0))
    return pl.pallas_call(
        functools.partial(_dsa_attn_body, ksel=ksel),
        grid=(B, Lp // TQ),
        in_specs=[qcol(W), qcol(W), qcol(LANES), full((Lp, W)), full((W, Lp)), full((Lp, LANES))],
        out_specs=pl.BlockSpec((1, TQ, W), lambda b_, i: (b_, i, 0)),
        out_shape=jax.ShapeDtypeStruct((B, Lp, W), F32),
        scratch_shapes=[
            pltpu.VMEM((Lp, TQ), I16),
            pltpu.VMEM((Lp, TQ), I16),
            pltpu.VMEM((Lp, TQ), F32),
            pltpu.VMEM((W, TQ), F32),
            pltpu.VMEM((8, TQ), I32),
            pltpu.VMEM((N_HEADS, KC, TQ), F32),
            pltpu.VMEM((N_HEADS, KC, TQ), BF16),
        ],
        compiler_params=_params("parallel", "arbitrary"),
        name="dsa_attn",
    )(qt, qit, wt, k, vt, kx)


def _outproj_router_body(x_ref, yr_ref, ya_ref, eg_ref, eb_ref, wo_ref, g1_ref, b1_ref, wrh_ref, wrl_ref, br_ref,
                         h_ref, grp_ref):
    h0 = _layer_norm(x_ref[0], eg_ref[...], eb_ref[...])
    mix = (_mm(yr_ref[0].astype(BF16), wo_ref[0:RWKV_WIDTH, :])
           + _mm(ya_ref[0].astype(BF16), wo_ref[RWKV_WIDTH:, :]))
    h1 = _layer_norm(DN_ALPHA * h0 + mix, g1_ref[...], b1_ref[...])
    h_ref[0, :, 0:D_MODEL] = h1
    logits = _dot3(_split(h1), (wrh_ref[...], wrl_ref[...]), _NN) + br_ref[...]
    lane = lax.broadcasted_iota(I32, (1, LANES), 1)
    lanef = lane.astype(F32)
    low = -3e38
    lgm = jnp.where(lane < N_GROUPS, logits, low)
    gmax = jnp.max(lgm, axis=1, keepdims=True)
    gsel = jnp.min(jnp.where(lgm == gmax, lanef, 1e9), axis=1, keepdims=True)
    gsum = jnp.sum(jnp.where(lane < N_GROUPS, jnp.exp(lgm - gmax), 0.0), axis=1, keepdims=True)
    group_of_lane = ((lane - EXPERT_LANE0) // EXPERTS_PER_GROUP).astype(F32)
    lem = jnp.where(group_of_lane == gsel, logits, low)
    m1 = jnp.max(lem, axis=1, keepdims=True)
    i1 = jnp.min(jnp.where(lem == m1, lanef, 1e9), axis=1, keepdims=True)
    lem2 = jnp.where(lanef == i1, low, lem)
    m2 = jnp.max(lem2, axis=1, keepdims=True)
    i2 = jnp.min(jnp.where(lem2 == m2, lanef, 1e9), axis=1, keepdims=True)
    e2 = jnp.exp(m2 - m1)
    w1 = 1.0 / (1.0 + e2)
    w2 = e2 / (1.0 + e2)
    gates = jnp.where(lanef == i1, w1, jnp.where(lanef == i2, w2, 0.0)) / gsum
    h_ref[0, :, D_MODEL:] = gates
    grp_ref[0] = jnp.broadcast_to(gsel, (ROWS_A, LANES))


def _outproj_router(x, y_r, y_a, eg, eb, wo, g1, b1, wrh, wrl, br):
    B, S, D = x.shape
    skip = PAD_ROWS // ROWS_A
    xrow = pl.BlockSpec((1, ROWS_A, D), lambda b_, i: (b_, i, 0))
    yrow = pl.BlockSpec((1, ROWS_A, RWKV_WIDTH), lambda b_, i: (b_, i + skip, 0))
    vec = _const_spec((1, D))
    return pl.pallas_call(
        _outproj_router_body,
        grid=(B, S // ROWS_A),
        in_specs=[xrow, yrow, yrow, vec, vec, _const_spec((D, D)), vec, vec,
                  _const_spec((D, LANES)), _const_spec((D, LANES)), _const_spec((1, LANES))],
        out_specs=[pl.BlockSpec((1, ROWS_A, D + LANES), lambda b_, i: (b_, i, 0)),
                   pl.BlockSpec((1, ROWS_A, LANES), lambda b_, i: (b_, i, 0))],
        out_shape=[jax.ShapeDtypeStruct((B, S, D + LANES), F32), jax.ShapeDtypeStruct((B, S, LANES), F32)],
        compiler_params=_params("parallel", "arbitrary"),
        name="outproj_router",
    )(x, y_r, y_a, eg, eb, wo, g1, b1, wrh, wrl, br)


def _moe_body(tgrp_ref, tcnt_ref, idx_ref, idx_next_ref, hx_ref, wg_ref, wu_ref, wd_ref, g2_ref, b2_ref, out_ref,
              xg_ref, acc_ref, hb_ref, ob_ref, gsem, ssem):
    i = pl.program_id(0)
    e = pl.program_id(1)
    n_tiles = pl.num_programs(0)
    n = tcnt_ref[i]
    slot = i % 2
    D = D_MODEL

    def row_gather(rows_ref, r, to_slot):
        return pltpu.make_async_copy(hx_ref.at[pl.ds(rows_ref[0, 0, r], 1)], xg_ref.at[to_slot, pl.ds(r, 1)],
                                     gsem.at[to_slot])

    def row_scatter(r):
        return pltpu.make_async_copy(ob_ref.at[pl.ds(r, 1)], out_ref.at[pl.ds(idx_ref[0, 0, r], 1)], ssem)

    def for_rows(count, fn):
        lax.fori_loop(0, count, lambda r, c: (fn(r), c)[1], 0)

    @pl.when(jnp.logical_and(i == 0, e == 0))
    def _():
        for_rows(n, lambda r: row_gather(idx_ref, r, slot).start())

    @pl.when(jnp.logical_and(e == 1, i + 1 < n_tiles))
    def _():
        for_rows(tcnt_ref[jnp.minimum(i + 1, n_tiles - 1)], lambda r: row_gather(idx_next_ref, r, 1 - slot).start())

    @pl.when(jnp.logical_and(e == 0, n == TMG))
    def _():
        pltpu.make_async_copy(hx_ref.at[pl.ds(0, TMG)], xg_ref.at[slot], gsem.at[slot]).wait()

    @pl.when(jnp.logical_and(e == 0, n < TMG))
    def _():
        for_rows(n, lambda r: row_gather(idx_ref, r, slot).wait())

    @pl.when(jnp.logical_and(e == 0, n > 0))
    def _():
        acc_ref[...] = jnp.zeros_like(acc_ref)
        hb_ref[...] = xg_ref[slot, :, 0:D].astype(BF16)

    @pl.when(n > 0)
    def _():
        t = hb_ref[...]
        lane = lax.broadcasted_iota(I32, (1, LANES), 1)
        gate_lane = EXPERT_LANE0 + tgrp_ref[i] * EXPERTS_PER_GROUP + e
        gcol = jnp.sum(jnp.where(lane == gate_lane, xg_ref[slot, :, D:], 0.0), axis=1, keepdims=True)
        a = _mm(t, wg_ref[0])
        hid = a * _sigmoid(a) * _mm(t, wu_ref[0]) * gcol
        acc_ref[...] += _mm(hid.astype(BF16), wd_ref[0])

    last = e == EXPERTS_PER_GROUP - 1

    @pl.when(jnp.logical_and(last, n > 0))
    def _():
        ob_ref[...] = _layer_norm(DN_ALPHA * xg_ref[slot, :, 0:D] + acc_ref[...], g2_ref[...], b2_ref[...])
        for_rows(n, lambda r: row_scatter(r).start())

    @pl.when(jnp.logical_and(last, n == TMG))
    def _():
        pltpu.make_async_copy(ob_ref, out_ref.at[pl.ds(0, TMG)], ssem).wait()

    @pl.when(jnp.logical_and(last, n < TMG))
    def _():
        for_rows(n, lambda r: row_scatter(r).wait())


def _moe(hx, tile_group, tile_count, tile_rows, wg, wu, wd, g2, b2):
    T, DX = hx.shape
    D = D_MODEL
    n_tiles = tile_rows.shape[0]
    wspec = lambda shape: pl.BlockSpec(
        (1,) + shape, lambda i, e, tg, tc: (tg[i] * EXPERTS_PER_GROUP + e, 0, 0))
    vec = pl.BlockSpec((1, D), lambda i, e, tg, tc: (0, 0))
    rows_of = lambda step: pl.BlockSpec(
        (1, 1, TMG), lambda i, e, tg, tc: (jnp.minimum(i + step, n_tiles - 1), 0, 0), memory_space=pltpu.SMEM)
    return pl.pallas_call(
        _moe_body,
        grid_spec=pltpu.PrefetchScalarGridSpec(
            num_scalar_prefetch=2,
            grid=(n_tiles, EXPERTS_PER_GROUP),
            in_specs=[rows_of(0), rows_of(1), pl.BlockSpec(memory_space=pl.ANY),
                      wspec((D, D_EXPERT)), wspec((D, D_EXPERT)), wspec((D_EXPERT, D)), vec, vec],
            out_specs=pl.BlockSpec(memory_space=pl.ANY),
            scratch_shapes=[pltpu.VMEM((2, TMG, DX), F32), pltpu.VMEM((TMG, D), F32), pltpu.VMEM((TMG, D), BF16),
                            pltpu.VMEM((TMG, D), F32), pltpu.SemaphoreType.DMA((2,)), pltpu.SemaphoreType.DMA],
        ),
        out_shape=jax.ShapeDtypeStruct((T, D), F32),
        compiler_params=_params("arbitrary", "arbitrary"),
        name="moe",
    )(tile_group, tile_count, tile_rows, tile_rows, hx, wg, wu, wd, g2, b2)


def _group_tiles(grp, n_tiles):
    T = grp.shape[0]
    onehot = (grp[:, None] == jnp.arange(N_GROUPS)[None, :]).astype(I32)
    rank = jnp.cumsum(onehot, axis=0) - onehot
    count = jnp.sum(onehot, axis=0)
    tiles_per_group = (count + TMG - 1) // TMG
    first_tile = jnp.cumsum(tiles_per_group) - tiles_per_group
    pos = first_tile[grp] * TMG + jnp.sum(rank * onehot, axis=1)
    tile_rows = jnp.zeros((n_tiles * TMG,), I32).at[pos].set(jnp.arange(T, dtype=I32)).reshape(n_tiles, 1, TMG)
    tile = jnp.arange(n_tiles)
    tile_group = jnp.clip(jnp.sum((tile[:, None] >= first_tile[None, :]).astype(I32), axis=1) - 1, 0, N_GROUPS - 1)
    in_group = tile - first_tile[tile_group]
    tile_count = jnp.clip(count[tile_group] - in_group * TMG, 0, TMG)
    tile_count = jnp.where(in_group < tiles_per_group[tile_group], tile_count, 0)
    return tile_group.astype(I32), tile_count.astype(I32), tile_rows


def _rope_tables(Lp):
    pos = jnp.maximum(jnp.arange(Lp, dtype=I32) - OFF, 0).astype(F32)
    j = jnp.arange(LANES) % HEAD_DIM

    def table(half, rot_dim):
        inv = 1.0 / (ROPE_THETA ** (jnp.arange(half, dtype=F32) / half))
        ang = pos[:, None] * inv[None, :]
        cos, sin = jnp.cos(ang)[:, j % half], jnp.sin(ang)[:, j % half]
        rotated = (j < rot_dim)[None, :]
        sign = jnp.where(j < half, -1.0, 1.0)[None, :]
        return jnp.where(rotated, cos, 1.0), jnp.where(rotated, sin * sign, 0.0)

    cf, sf = table(HEAD_DIM // 2, HEAD_DIM)
    cp, sp = table(IDX_ROPE_DIM // 2, IDX_ROPE_DIM)
    return cf, sf, cp, sp


def _block_ones(n, block):
    idx = jnp.arange(n) // block
    return (idx[:, None] == idx[None, :]).astype(F32)


def kernel(x, meta_tokens, ln_emb_g, ln_emb_b, w_in, rw_mu, rw_w0, rw_w2, rw_a0, rw_a2, rw_g2, rw_kk, rw_ka,
           rw_rk, rw_lnx_g, rw_lnx_b, att_qnorm_g, att_wuq, idx_wq, idx_knorm_g, idx_knorm_b, w_out, ln1_g,
           ln1_b, rt_grp_w, rt_grp_b, rt_exp_w, rt_exp_b, ex_w_gate, ex_w_up, ex_w_down, ln2_g, ln2_b):
    B, S, D = x.shape
    assert w_in.shape[0] == 1 and D == D_MODEL and (B * S) % TMG == 0
    Lp = S + PAD_ROWS
    ksel = min(INDEX_TOPK, S // 4)
    row = lambda t: t.reshape(1, -1)
    W = RWKV_WIDTH

    meta_pad = jnp.zeros((PAD_ROWS, D), F32).at[OFF:].set(meta_tokens)
    w_in_p = jnp.pad(w_in[0], ((0, 0), (0, ATT_COLS_PAD - ATT_COLS))).astype(BF16)
    u_r, u_a = _ln_inproj(x, meta_pad, row(ln_emb_g), row(ln_emb_b), w_in_p)

    w2p = jnp.concatenate([rw_w2[0], jnp.zeros((ICLR_RANK, W), F32)], 0).astype(BF16)
    a2p = jnp.concatenate([jnp.zeros((DECAY_RANK, W), F32), rw_a2[0]], 0).astype(BF16)
    head_of_lane = jnp.arange(W) // HEAD_DIM
    hs = (head_of_lane[:, None] == jnp.arange(LANES)[None, :]).astype(BF16)
    tri = (jnp.arange(ROWS_R)[:, None] >= jnp.arange(ROWS_R)[None, :]).astype(F32)
    tri = (tri * _block_ones(ROWS_R, CHUNK)).astype(BF16)
    rp, kp, bp, ap, v, pc, g, bonus = _rwkv_prep(
        u_r, row(rw_mu[0]), row(rw_w0[0]), w2p, row(rw_a0[0]), a2p, rw_g2[0].astype(BF16), row(rw_kk[0]),
        row(rw_ka[0]), row(rw_rk[0]), hs, hs.T, tri)
    y_r = _rwkv_scan(rp, kp, bp, ap, v, pc, g, bonus, row(rw_lnx_g[0]), row(rw_lnx_b[0]))

    pad_lanes = lambda t: jnp.pad(t, (0, LANES - t.shape[0])).reshape(1, LANES)
    cf, sf, cp, sp = _rope_tables(Lp)
    qt, k, vt, qit, kx, wt = _dsa_prep(
        u_a, row(att_qnorm_g[0]), att_wuq[0].astype(BF16), idx_wq[0].astype(BF16),
        pad_lanes(idx_knorm_g[0]), pad_lanes(idx_knorm_b[0]), cf, sf, cp, sp)
    y_a = _dsa_attn(qt, k, vt, qit, kx, wt, ksel)

    wr = jnp.zeros((D, LANES), F32).at[:, :N_GROUPS].set(rt_grp_w[0])
    wr = wr.at[:, EXPERT_LANE0:EXPERT_LANE0 + N_EXPERTS].set(rt_exp_w[0])
    br = jnp.zeros((1, LANES), F32).at[0, :N_GROUPS].set(rt_grp_b[0])
    br = br.at[0, EXPERT_LANE0:EXPERT_LANE0 + N_EXPERTS].set(rt_exp_b[0])
    wrh = wr.astype(BF16)
    wrl = (wr - wrh.astype(F32)).astype(BF16)
    h1, grp = _outproj_router(x, y_r, y_a, row(ln_emb_g), row(ln_emb_b), w_out[0].astype(BF16),
                                row(ln1_g[0]), row(ln1_b[0]), wrh, wrl, br)
    T = B * S
    n_tiles = T // TMG + N_GROUPS
    tile_group, tile_count, tile_rows = _group_tiles(grp.reshape(T, LANES)[:, 0].astype(I32), n_tiles)
    out = _moe(h1.reshape(T, D + LANES), tile_group, tile_count, tile_rows, ex_w_gate[0].astype(BF16),
               ex_w_up[0].astype(BF16), ex_w_down[0].astype(BF16), row(ln2_g[0]), row(ln2_b[0]))
    return out.reshape(B, S, D)
```

```python
import functools

import jax
import jax.numpy as jnp
from jax import lax
from jax.experimental import pallas as pl
from jax.experimental.pallas import tpu as pltpu

F32 = jnp.float32
BF16 = jnp.bfloat16
I32 = jnp.int32
I16 = jnp.int16
HIGHEST = lax.Precision.HIGHEST

D_MODEL = 1024
N_META = 16
RWKV_WIDTH = 512
ATT_WIDTH = 512
HEAD_DIM = 64
N_HEADS = 8
DECAY_RANK = 64
ICLR_RANK = 64
GATE_RANK = 128
Q_LORA_RANK = 256
IDX_HEADS = 8
IDX_DIM = 64
IDX_ROPE_DIM = 32
INDEX_TOPK = 256
ROPE_THETA = 10000.0
N_GROUPS = 4
EXPERTS_PER_GROUP = 8
N_EXPERTS = N_GROUPS * EXPERTS_PER_GROUP
D_EXPERT = 256
DN_ALPHA = 2.0 ** 0.25
LN_EPS = 1e-5
RMS_EPS = 1e-6
GN_EPS = 64e-5
RWKV_COLS = 3 * RWKV_WIDTH + DECAY_RANK + ICLR_RANK + GATE_RANK
ATT_COLS = Q_LORA_RANK + 2 * ATT_WIDTH + IDX_DIM + IDX_HEADS
ATT_COLS_PAD = 1408

LANES = 128
PAD_ROWS = 256
OFF = PAD_ROWS - N_META
CHUNK = 64
ROWS_A = 256
ROWS_R = 256
TQ = 256
KC = 256
TMG = 512
SCAN_BATCH = 4
EXPERT_LANE0 = 64
NEG = -1e30
MASKED = -3e38
BELOW_ALL = -1e38
ABOVE_ALL = 3e38
MAX_REFINE = 400
INT_MIN = -2147483648
HALF16 = 32768
VMEM_LIMIT = 56 * 1024 * 1024


def _mm(a, b, precision=None):
    return jnp.dot(a, b, preferred_element_type=F32, precision=precision)


def _mm_nt(a, b, precision=None):
    return lax.dot_general(a, b, (((1,), (1,)), ((), ())), preferred_element_type=F32, precision=precision)


def _mm_tn(a, b, precision=None):
    return lax.dot_general(a, b, (((0,), (0,)), ((), ())), preferred_element_type=F32, precision=precision)


def _sigmoid(x):
    return 1.0 / (1.0 + jnp.exp(-x))


def _layer_norm(x, g, b):
    mu = jnp.mean(x, -1, keepdims=True)
    xc = x - mu
    var = jnp.mean(xc * xc, -1, keepdims=True)
    return xc * lax.rsqrt(var + LN_EPS) * g + b


def _params(*sem):
    return pltpu.CompilerParams(dimension_semantics=sem, vmem_limit_bytes=VMEM_LIMIT)


def _const_spec(shape):
    nd = len(shape)
    return pl.BlockSpec(shape, lambda *_: (0,) * nd)


def _ln_inproj_body(x_ref, meta_ref, g_ref, b_ref, w_ref, ur_ref, ua_ref):
    blk = pl.program_id(1)
    xin = jnp.where(blk == 0, meta_ref[...], x_ref[0])
    h = _layer_norm(xin, g_ref[...], b_ref[...])
    row = lax.broadcasted_iota(I32, (ROWS_A, 1), 0)
    h = jnp.where((blk > 0) | (row >= OFF), h, 0.0)
    hb = h.astype(BF16)
    step = 256
    for n0 in range(0, RWKV_COLS, step):
        n1 = min(n0 + step, RWKV_COLS)
        ur_ref[0, :, n0:n1] = _mm(hb, w_ref[:, n0:n1])
    for n0 in range(0, ATT_COLS_PAD, step):
        n1 = min(n0 + step, ATT_COLS_PAD)
        ua_ref[0, :, n0:n1] = _mm(hb, w_ref[:, RWKV_COLS + n0:RWKV_COLS + n1])


def _ln_inproj(x, meta_pad, g, b, w):
    B, S, D = x.shape
    nblk = (S + PAD_ROWS) // ROWS_A
    Lp = S + PAD_ROWS
    ncols = RWKV_COLS + ATT_COLS_PAD
    return pl.pallas_call(
        _ln_inproj_body,
        grid=(B, nblk),
        in_specs=[
            pl.BlockSpec((1, ROWS_A, D), lambda b_, i: (b_, jnp.maximum(i - 1, 0), 0)),
            _const_spec((ROWS_A, D)),
            _const_spec((1, D)),
            _const_spec((1, D)),
            _const_spec((D, ncols)),
        ],
        out_specs=[
            pl.BlockSpec((1, ROWS_A, RWKV_COLS), lambda b_, i: (b_, i, 0)),
            pl.BlockSpec((1, ROWS_A, ATT_COLS_PAD), lambda b_, i: (b_, i, 0)),
        ],
        out_shape=[
            jax.ShapeDtypeStruct((B, Lp, RWKV_COLS), F32),
            jax.ShapeDtypeStruct((B, Lp, ATT_COLS_PAD), F32),
        ],
        compiler_params=_params("parallel", "arbitrary"),
        name="ln_inproj",
    )(x, meta_pad, g, b, w)


def _split3(x):
    hi = x.astype(BF16)
    r1 = x - hi.astype(F32)
    mid = r1.astype(BF16)
    return hi, mid, (r1 - mid.astype(F32)).astype(BF16)


def _mm_exact_rhs(x, m):
    return sum(_mm(p, m) for p in _split3(x))


def _mm_exact_lhs(m, x):
    return sum(_mm(m, p) for p in _split3(x))


def _rwkv_prep_body(u_ref, prev_ref, mu_ref, w0_ref, w2_ref, a0_ref, a2_ref, g2_ref, kk_ref, ka_ref, rk_ref,
                    hs_ref, hb_ref, tri_ref,
                    rp_ref, kp_ref, bp_ref, ap_ref, v_ref, pc_ref, g_ref, bonus_ref):
    blk = pl.program_id(1)
    u = u_ref[0]
    prev = jnp.where(blk == 0, 0.0, prev_ref[0][7:8, :])
    row = lax.broadcasted_iota(I32, (ROWS_R, 1), 0)
    shifted = jnp.where(row == 0, prev, pltpu.roll(u, 1, 0))
    ul = u + (shifted - u) * mu_ref[...]
    W = RWKV_WIDTH
    r = ul[:, 0:W]
    k = ul[:, W:2 * W]
    v = ul[:, 2 * W:3 * W]
    wa = ul[:, 3 * W:3 * W + 128]
    gd = ul[:, 3 * W + 128:3 * W + 256]
    w = w0_ref[...] + _mm(jnp.tanh(wa).astype(BF16), w2_ref[...])
    softplus_neg_w = jnp.maximum(-w, 0.0) + jnp.log(1.0 + jnp.exp(-jnp.abs(w)))
    logd = -jnp.exp(-softplus_neg_w - 0.5)
    a = _sigmoid(a0_ref[...] + _mm(wa.astype(BF16), a2_ref[...]))
    g_ref[0] = _mm(_sigmoid(gd).astype(BF16), g2_ref[...])
    head_sum = lambda t: _mm_exact_rhs(_mm_exact_rhs(t, hs_ref[...]), hb_ref[...])
    kkr = k * kk_ref[...]
    kk = kkr / jnp.maximum(jnp.sqrt(head_sum(kkr * kkr)), 1e-12)
    kmod = k * (1.0 + (a - 1.0) * ka_ref[...])
    bonus_ref[0] = head_sum(r * kmod * rk_ref[...]) * v
    v_ref[0] = v
    cum = _mm_exact_lhs(tri_ref[...], logd)
    rp_ref[0] = r * jnp.exp(cum)
    einv = jnp.exp(-cum)
    kp_ref[0] = kmod * einv
    bp_ref[0] = kk * a * einv
    ap_ref[0] = -kk * jnp.exp(cum - logd)
    for c in range(ROWS_R // CHUNK):
        last = c * CHUNK + CHUNK - 1
        pc_ref[0, c] = jnp.broadcast_to(jnp.exp(cum[last:last + 1]), (8, W))


def _rwkv_prep(u_r, mu, w0, w2p, a0, a2p, g2, k_k, k_a, r_k, hs, hb, tri):
    B, Lp, _ = u_r.shape
    nblk = Lp // ROWS_R
    W = RWKV_WIDTH
    row_spec = pl.BlockSpec((1, ROWS_R, W), lambda b_, i: (b_, i, 0))
    row_shape = jax.ShapeDtypeStruct((B, Lp, W), F32)
    cpb = ROWS_R // CHUNK
    return pl.pallas_call(
        _rwkv_prep_body,
        grid=(B, nblk),
        in_specs=[
            pl.BlockSpec((1, ROWS_R, RWKV_COLS), lambda b_, i: (b_, i, 0)),
            pl.BlockSpec((1, 8, RWKV_COLS), lambda b_, i: (b_, jnp.maximum(i * (ROWS_R // 8) - 1, 0), 0)),
            _const_spec((1, RWKV_COLS)),
            _const_spec((1, W)),
            _const_spec((128, W)),
            _const_spec((1, W)),
            _const_spec((128, W)),
            _const_spec((128, W)),
            _const_spec((1, W)),
            _const_spec((1, W)),
            _const_spec((1, W)),
            _const_spec((W, LANES)),
            _const_spec((LANES, W)),
            _const_spec((ROWS_R, ROWS_R)),
        ],
        out_specs=[row_spec] * 5 + [pl.BlockSpec((1, cpb, 8, W), lambda b_, i: (b_, i, 0, 0))] + [row_spec] * 2,
        out_shape=[row_shape] * 5 + [jax.ShapeDtypeStruct((B, Lp // CHUNK, 8, W), F32)] + [row_shape] * 2,
        compiler_params=_params("parallel", "arbitrary"),
        name="rwkv_prep",
    )(u_r, u_r, mu, w0, w2p, a0, a2p, g2, k_k, k_a, r_k, hs, hb, tri)


def _rwkv_scan_body(rp_ref, kp_ref, bp_ref, ap_ref, v_ref, pc_ref, g_ref, bonus_ref, lg_ref, lb_ref, o_ref, s_ref,
                    *, nb):
    c = pl.program_id(1)

    @pl.when(c == 0)
    def _():
        s_ref[...] = jnp.zeros_like(s_ref)

    @pl.when(c < OFF // CHUNK)
    def _():
        o_ref[...] = jnp.zeros_like(o_ref)

    @pl.when(c >= OFF // CHUNK)
    def _():
        _rwkv_chunk(rp_ref, kp_ref, bp_ref, ap_ref, v_ref, pc_ref, g_ref, bonus_ref, lg_ref, lb_ref, o_ref, s_ref, nb)


def _split(x):
    hi = x.astype(BF16)
    return hi, (x - hi.astype(F32)).astype(BF16)


def _dot3(a, b, dims):
    dg = lambda p, q: lax.dot_general(p, q, (dims, ((), ())), preferred_element_type=F32)
    return dg(a[0], b[0]) + dg(a[0], b[1]) + dg(a[1], b[0])


_NN = ((1,), (0,))
_NT = ((1,), (1,))
_TN = ((0,), (0,))


def _rwkv_chunk(rp_ref, kp_ref, bp_ref, ap_ref, v_ref, pc_ref, g_ref, bonus_ref, lg_ref, lb_ref, o_ref, s_ref, nb):
    C, N = CHUNK, HEAD_DIM
    ri = lax.broadcasted_iota(I32, (C, C), 0)
    ci = lax.broadcasted_iota(I32, (C, C), 1)
    strict = ri > ci
    incl = ri >= ci
    eye = jnp.where(ri == ci, 1.0, 0.0)
    units = [(b, slice(h * N, (h + 1) * N)) for b in range(nb) for h in range(N_HEADS)]
    ids = range(len(units))
    rows2 = lambda top, bottom: jnp.concatenate([top, bottom], axis=0)
    pcs = [pc_ref[b, 0, 0:1, sl] for b, sl in units]
    ar = [_split(rows2(ap_ref[b, :, sl], rp_ref[b, :, sl])) for b, sl in units]
    bk = [_split(rows2(bp_ref[b, :, sl], kp_ref[b, :, sl])) for b, sl in units]
    v_ = [_split(v_ref[b, :, sl]) for b, sl in units]
    s0 = [s_ref[i] for i in ids]
    s0s = [_split(s) for s in s0]
    gram = [_dot3(ar[i], bk[i], _NT) for i in ids]
    a_ab = [jnp.where(strict, gram[i][0:C, 0:C], 0.0) for i in ids]
    a_ak = [jnp.where(strict, gram[i][0:C, C:2 * C], 0.0) for i in ids]
    a_rb = [_split(jnp.where(incl, gram[i][C:2 * C, 0:C], 0.0)) for i in ids]
    a_rk = [jnp.where(incl, gram[i][C:2 * C, C:2 * C], 0.0) for i in ids]
    t = [eye + a_ab[i] for i in ids]
    pb = [a_ab[i].astype(BF16) for i in ids]
    for _ in range(C.bit_length() - 2):
        pb = [_mm(pb[i], pb[i]).astype(BF16) for i in ids]
        t = [t[i] + _mm(t[i].astype(BF16), pb[i]) for i in ids]
    ts = [_split(t[i]) for i in ids]
    resid = [(eye - t[i]) + _dot3(_split(a_ab[i]), ts[i], _NN) for i in ids]
    t = [t[i] + _mm(ts[i][0], resid[i].astype(BF16)) for i in ids]
    sp = [_dot3(ar[i], s0s[i], _NT) for i in ids]
    av = [_dot3(_split(rows2(a_ak[i], a_rk[i])), v_[i], _NN) for i in ids]
    u_ = [_dot3(_split(t[i]), _split(sp[i][0:C] + av[i][0:C]), _NN) for i in ids]
    for i, (b, sl) in enumerate(units):
        o = sp[i][C:2 * C] + av[i][C:2 * C] + _dot3(a_rb[i], _split(u_[i]), _NN)
        oc = o - jnp.mean(o, axis=1, keepdims=True)
        var = jnp.mean(oc * oc, axis=1, keepdims=True)
        y = oc * lax.rsqrt(var + GN_EPS) * lg_ref[:, sl] + lb_ref[:, sl]
        o_ref[b, :, sl] = (y + bonus_ref[b, :, sl]) * g_ref[b, :, sl]
    for i, (b, sl) in enumerate(units):
        vu = _split(rows2(v_ref[b, :, sl], u_[i]))
        kb = _split(rows2(kp_ref[b, :, sl], bp_ref[b, :, sl]) * pcs[i])
        s_ref[i] = s0[i] * pcs[i] + _dot3(vu, kb, _TN)


def _rwkv_scan(rp, kp, bp, ap, v, pc, g, bonus, lg, lb):
    B, Lp, W = rp.shape
    nch = Lp // CHUNK
    nb = SCAN_BATCH if B % SCAN_BATCH == 0 else 1
    row_spec = pl.BlockSpec((nb, CHUNK, W), lambda b_, c: (b_, c, 0))
    return pl.pallas_call(
        functools.partial(_rwkv_scan_body, nb=nb),
        grid=(B // nb, nch),
        in_specs=[row_spec] * 5 + [pl.BlockSpec((nb, 1, 8, W), lambda b_, c: (b_, c, 0, 0))] + [row_spec] * 2
        + [_const_spec((1, W))] * 2,
        out_specs=row_spec,
        out_shape=jax.ShapeDtypeStruct((B, Lp, W), F32),
        scratch_shapes=[pltpu.VMEM((nb * N_HEADS, HEAD_DIM, HEAD_DIM), F32)],
        compiler_params=_params("parallel", "arbitrary"),
        name="rwkv_scan",
    )(rp, kp, bp, ap, v, pc, g, bonus, lg, lb)


def _rope(x, cos, sin, half, first):
    width = x.shape[1]
    rot = jnp.where(first, pltpu.roll(x, width - half, 1), pltpu.roll(x, half, 1))
    return x * cos + rot * sin


def _dsa_prep_body(u_ref, qg_ref, wuq_ref, wiq_ref, kng_ref, knb_ref, cf_ref, sf_ref, cp_ref, sp_ref,
                   qt_ref, k_ref, vt_ref, qit_ref, kx_ref, wt_ref):
    u = u_ref[0]
    cq = u[:, 0:Q_LORA_RANK]
    k = u[:, Q_LORA_RANK:Q_LORA_RANK + ATT_WIDTH]
    v = u[:, Q_LORA_RANK + ATT_WIDTH:Q_LORA_RANK + 2 * ATT_WIDTH]
    tail = u[:, Q_LORA_RANK + 2 * ATT_WIDTH:]
    cqn = (cq * lax.rsqrt(jnp.mean(cq * cq, -1, keepdims=True) + RMS_EPS) * qg_ref[...]).astype(BF16)
    reps = ATT_WIDTH // LANES
    cf = jnp.concatenate([cf_ref[...]] * reps, axis=1)
    sf = jnp.concatenate([sf_ref[...]] * reps, axis=1)
    cp = jnp.concatenate([cp_ref[...]] * reps, axis=1)
    sp = jnp.concatenate([sp_ref[...]] * reps, axis=1)
    lane_w = lax.broadcasted_iota(I32, (1, ATT_WIDTH), 1) % HEAD_DIM
    first_f = lane_w < HEAD_DIM // 2
    first_p = lane_w < IDX_ROPE_DIM // 2
    q = _rope(_mm(cqn, wuq_ref[...]), cf, sf, HEAD_DIM // 2, first_f)
    qt_ref[0] = (q * (HEAD_DIM ** -0.5)).T.astype(BF16)
    qi = _rope(_mm(cqn, wiq_ref[...]), cp, sp, IDX_ROPE_DIM // 2, first_p)
    qit_ref[0] = qi.T.astype(BF16)
    k_ref[0] = _rope(k, cf, sf, HEAD_DIM // 2, first_f).astype(BF16)
    vt_ref[0] = v.T.astype(BF16)
    lane = lax.broadcasted_iota(I32, (1, LANES), 1)
    is_key = lane < IDX_DIM
    mu = jnp.sum(jnp.where(is_key, tail, 0.0), -1, keepdims=True) * (1.0 / IDX_DIM)
    tc = jnp.where(is_key, tail - mu, 0.0)
    var = jnp.sum(tc * tc, -1, keepdims=True) * (1.0 / IDX_DIM)
    kn = tc * lax.rsqrt(var + LN_EPS) * kng_ref[...] + knb_ref[...]
    kn = _rope(kn, cp_ref[...], sp_ref[...], IDX_ROPE_DIM // 2, (lane % HEAD_DIM) < IDX_ROPE_DIM // 2)
    kn = kn * (IDX_DIM ** -0.5)
    kx_ref[0] = jnp.where(is_key, kn, pltpu.roll(kn, IDX_DIM, 1)).astype(BF16)
    wt_ref[0] = (tail * (IDX_HEADS ** -0.5)).T


def _dsa_prep(u_a, qg, wuq, wiq, kng, knb, cf, sf, cp, sp):
    B, Lp, _ = u_a.shape
    W = ATT_WIDTH
    row = lambda width: pl.BlockSpec((1, ROWS_A, width), lambda b_, i: (b_, i, 0))
    col = lambda height: pl.BlockSpec((1, height, ROWS_A), lambda b_, i: (b_, 0, i))
    tab = pl.BlockSpec((ROWS_A, LANES), lambda b_, i: (i, 0))
    return pl.pallas_call(
        _dsa_prep_body,
        grid=(B, Lp // ROWS_A),
        in_specs=[row(ATT_COLS_PAD), _const_spec((1, Q_LORA_RANK)), _const_spec((Q_LORA_RANK, W)),
                  _const_spec((Q_LORA_RANK, W)), _const_spec((1, LANES)), _const_spec((1, LANES)),
                  tab, tab, tab, tab],
        out_specs=[col(W), row(W), col(W), col(W), row(LANES), col(LANES)],
        out_shape=[jax.ShapeDtypeStruct((B, W, Lp), BF16), jax.ShapeDtypeStruct((B, Lp, W), BF16),
                   jax.ShapeDtypeStruct((B, W, Lp), BF16), jax.ShapeDtypeStruct((B, W, Lp), BF16),
                   jax.ShapeDtypeStruct((B, Lp, LANES), BF16), jax.ShapeDtypeStruct((B, LANES, Lp), F32)],
        compiler_params=_params("parallel", "arbitrary"),
        name="dsa_prep",
    )(u_a, qg, wuq, wiq, kng, knb, cf, sf, cp, sp)


def _fold_rows(x, op=jnp.add):
    parts = [x[8 * r:8 * r + 8] for r in range(x.shape[0] // 8)]
    while len(parts) > 1:
        parts = [op(a, b) for a, b in zip(parts[0::2], parts[1::2])] + parts[len(parts) & ~1:]
    return parts[0]


def _dsa_attn_body(qt_ref, qit_ref, wt_ref, k_ref, vt_ref, kx_ref, o_ref, khi_ref, klo_ref, sc_ref, acc_ref,
                   j_ref, s_ref, pe_ref, *, ksel):
    i = pl.program_id(1)
    nkc = (i * TQ + TQ - 1) // KC + 1
    kf = float(ksel)
    tcol = i * TQ + lax.broadcasted_iota(I32, (1, TQ), 1)
    int_min = jnp.int32(INT_MIN)
    row_in_pair = lax.broadcasted_iota(I32, (LANES, 1), 0)

    def head_operands(ref):
        out = []
        for h in range(N_HEADS):
            pair = ref[0, LANES * (h // 2):LANES * (h // 2 + 1), :]
            keep = (row_in_pair < HEAD_DIM) if h % 2 == 0 else (row_in_pair >= HEAD_DIM)
            out.append(jnp.where(keep, pair, jnp.zeros_like(pair)))
        return out

    def key_rows(ks):
        return ks + lax.broadcasted_iota(I32, (KC, 1), 0)

    qis = head_operands(qit_ref)
    wrows = [wt_ref[0, IDX_DIM + h:IDX_DIM + h + 1, :] for h in range(IDX_HEADS)]

    def score_chunk(kc, carry):
        lo8, hi8 = carry
        ks = pl.multiple_of(kc * KC, KC)
        kx = kx_ref[0, pl.ds(ks, KC), :]
        sc = jnp.zeros((KC, TQ), F32)
        for h in range(IDX_HEADS):
            sc = sc + jnp.maximum(_mm(kx, qis[h]), 0.0) * wrows[h]
        sc = sc + 0.0
        krow = key_rows(ks)
        sc = jnp.where(krow >= OFF, sc, MASKED)
        sc = jnp.where(krow <= tcol, sc, MASKED)
        sc_ref[pl.ds(ks, KC), :] = sc
        bits = lax.bitcast_convert_type(sc, I32)
        key = jnp.where(bits >= 0, bits, bits ^ jnp.int32(0x7FFFFFFF))
        khi_ref[pl.ds(ks, KC), :] = lax.shift_right_arithmetic(key, 16).astype(I16)
        klo_ref[pl.ds(ks, KC), :] = ((key & jnp.int32(0xFFFF)) - HALF16).astype(I16)
        lo8 = jnp.minimum(lo8, _fold_rows(jnp.where(sc <= MASKED, ABOVE_ALL, sc), jnp.minimum))
        hi8 = jnp.maximum(hi8, _fold_rows(sc, jnp.maximum))
        return lo8, hi8

    lo8, hi8 = lax.fori_loop(0, nkc, score_chunk,
                             (jnp.full((8, TQ), ABOVE_ALL, F32), jnp.full((8, TQ), MASKED, F32)))
    smin = jnp.min(lo8, axis=0, keepdims=True)
    smax = jnp.max(hi8, axis=0, keepdims=True)

    def scan_chunks(fn, init):
        def body(kc, carry):
            ks = pl.multiple_of(kc * KC, KC)
            return fn(carry, ks)
        return lax.fori_loop(0, nkc, body, init)

    zeros8 = jnp.zeros((8, TQ), F32)

    def count_where(ref, pred):
        cnt = scan_chunks(lambda c, ks: c + _fold_rows(pred(ref[pl.ds(ks, KC), :], ks)), zeros8)
        return jnp.sum(cnt, axis=0, keepdims=True)

    def count16(ref, pred):
        def fn(cnt, ks):
            m = pred(ref[pl.ds(ks, KC), :])
            parts = [m[16 * r:16 * r + 16] for r in range(KC // 16)]
            while len(parts) > 1:
                parts = [a + b for a, b in zip(parts[0::2], parts[1::2])]
            return cnt + parts[0]
        cnt = scan_chunks(fn, jnp.zeros((16, TQ), I16))
        return jnp.sum(cnt.astype(I32), axis=0, keepdims=True)

    one16, zero16 = jnp.int16(1), jnp.int16(0)

    def radix16(ref, target):
        def bit(bi, prefix):
            cand = prefix | lax.shift_left(jnp.int32(1), 15 - bi)
            cand16 = (cand - HALF16).astype(I16)
            cnt = count16(ref, lambda x: jnp.where(x >= cand16, one16, zero16))
            return jnp.where(cnt >= target, cand, prefix)
        return lax.fori_loop(0, 16, bit, jnp.zeros((1, TQ), I32))

    k_int = jnp.full((1, TQ), ksel, I32)
    thr_hi = radix16(khi_ref, k_int) - HALF16
    thr_hi16 = thr_hi.astype(I16)
    above = count16(khi_ref, lambda x: jnp.where(x > thr_hi16, one16, zero16))

    def keep_low_of_ties(carry, ks):
        rows = pl.ds(ks, KC)
        klo_ref[rows, :] = jnp.where(khi_ref[rows, :] == thr_hi16, klo_ref[rows, :], jnp.int16(-HALF16))
        return carry

    scan_chunks(keep_low_of_ties, 0)
    thr_lo = radix16(klo_ref, k_int - above)
    thr_key = lax.shift_left(thr_hi, 16) | thr_lo
    cand0 = lax.bitcast_convert_type(jnp.where(thr_key >= 0, thr_key, thr_key ^ jnp.int32(0x7FFFFFFF)), F32)

    n_adm = jnp.maximum(tcol - (OFF - 1), 0).astype(F32)
    searching = n_adm > kf

    def probe(mid):
        def fn(carry, ks):
            cnt, vmin = carry
            s = sc_ref[pl.ds(ks, KC), :]
            ge = s >= mid
            return (cnt + _fold_rows(jnp.where(ge, 1.0, 0.0)),
                    jnp.minimum(vmin, _fold_rows(jnp.where(ge, s, ABOVE_ALL), jnp.minimum)))
        cnt, vmin = scan_chunks(fn, (zeros8, jnp.full((8, TQ), ABOVE_ALL, F32)))
        return jnp.sum(cnt, axis=0, keepdims=True), jnp.min(vmin, axis=0, keepdims=True)

    def refine(state):
        it, lo, hi, c_lo, c_gt, done, _ = state
        mid = jnp.where(it == 0, jnp.where(searching, cand0, lo), lo + 0.5 * (hi - lo))
        c_mid, v_mid = probe(mid)
        up = c_mid >= kf
        lo_n = jnp.where(up, v_mid, lo)
        hi_n = jnp.where(up, hi, mid)
        c_lo_n = jnp.where(up, c_mid, c_lo)
        c_gt_n = count_where(sc_ref, lambda s, ks: jnp.where(s > lo_n, 1.0, 0.0))
        stalled = jnp.where(it > 0, jnp.where(mid <= lo, 1.0, jnp.where(mid >= hi, 1.0, 0.0)), 0.0)
        fin = jnp.maximum(jnp.where(c_gt_n < kf, 1.0, 0.0), stalled)
        frozen = done > 0.0
        keep = lambda old, new_: jnp.where(frozen, old, new_)
        done_n = jnp.maximum(done, fin)
        return (it + 1, keep(lo, lo_n), keep(hi, hi_n), keep(c_lo, c_lo_n), keep(c_gt, c_gt_n), done_n,
                jnp.max(1.0 - done_n))

    done0 = jnp.where(searching, 0.0, 1.0)
    state0 = (jnp.int32(0), smin, smax + (jnp.abs(smax) + 1.0) * 1e-6, n_adm, n_adm, done0, jnp.max(1.0 - done0))
    state = lax.while_loop(lambda st: jnp.logical_and(st[6] > 0.0, st[0] < MAX_REFINE), refine, state0)
    thr = jnp.where(searching, state[1], BELOW_ALL)
    cnt_gt = jnp.where(searching, state[4], n_adm)
    cnt_eq = jnp.where(searching, state[3] - state[4], 0.0)
    need = kf - cnt_gt

    j_ref[...] = jnp.full(j_ref.shape, 2 ** 30, I32)

    @pl.when(jnp.max(cnt_eq - need) > 0.0)
    def _():
        def index_bit(bi, prefix):
            cand = prefix | lax.shift_left(jnp.int32(1), 12 - bi)
            before = count_where(
                sc_ref, lambda s, ks: jnp.where(s == thr, jnp.where(key_rows(ks) < cand, 1.0, 0.0), 0.0))
            return jnp.where(before < need, cand, prefix)
        jst = lax.fori_loop(0, 13, index_bit, jnp.zeros((1, TQ), I32))
        j_ref[...] = jnp.broadcast_to(jst, j_ref.shape)

    jstar = j_ref[0:1, :]

    qs = head_operands(qt_ref)
    acc_ref[...] = jnp.zeros_like(acc_ref)

    ones_rows = jnp.ones((16, KC), BF16)

    def attend_chunk(kc, carry):
        ms, ls = carry
        ks = pl.multiple_of(kc * KC, KC)
        sc = sc_ref[pl.ds(ks, KC), :]
        tie = jnp.where(sc == thr, jnp.where(key_rows(ks) <= jstar, 0.0, NEG), NEG)
        bias = jnp.where(sc > thr, 0.0, tie)
        chunk_max = []
        for h in range(N_HEADS):
            p = h // 2
            kp = k_ref[0, pl.ds(ks, KC), LANES * p:LANES * (p + 1)]
            s = _mm(kp, qs[h]) + bias
            s_ref[h] = s
            chunk_max.append(jnp.max(s, axis=0, keepdims=True))
        new_ms, new_ls, alphas = [], [], []
        for h in range(N_HEADS):
            m_new = jnp.maximum(ms[h], chunk_max[h])
            alphas.append(jnp.exp(ms[h] - m_new))
            new_ms.append(m_new)
            pe_ref[h] = jnp.exp(s_ref[h] - m_new).astype(BF16)
        for h in range(N_HEADS):
            vt = vt_ref[0, HEAD_DIM * h:HEAD_DIM * (h + 1), pl.ds(ks, KC)]
            pv = _mm(jnp.concatenate([vt, ones_rows], axis=0), pe_ref[h])
            rows = slice(HEAD_DIM * h, HEAD_DIM * (h + 1))
            acc_ref[rows, :] = acc_ref[rows, :] * alphas[h] + pv[0:HEAD_DIM]
            new_ls.append(alphas[h] * ls[h] + pv[HEAD_DIM:HEAD_DIM + 1])
        return tuple(new_ms), tuple(new_ls)

    init = (tuple(jnp.full((1, TQ), NEG, F32) for _ in range(N_HEADS)),
            tuple(jnp.zeros((1, TQ), F32) for _ in range(N_HEADS)))
    _, ls = lax.fori_loop(0, nkc, attend_chunk, init)
    for p in range(N_HEADS // 2):
        parts = [acc_ref[HEAD_DIM * h:HEAD_DIM * (h + 1), :] / ls[h] for h in (2 * p, 2 * p + 1)]
        o_ref[0, :, LANES * p:LANES * (p + 1)] = jnp.concatenate(parts, axis=0).T


def _dsa_attn(qt, k, vt, qit, kx, wt, ksel):
    B, Lp, W = k.shape
    assert Lp % KC == 0 and Lp <= 8192
    qcol = lambda height: pl.BlockSpec((1, height, TQ), lambda b_, i: (b_, 0, i))
    full = lambda shape: pl.BlockSpec((1,) + shape, lambda b_, i: (b_, 0, 0))
    return pl.pallas_call(
        functools.partial(_dsa_attn_body, ksel=ksel),
        grid=(B, Lp // TQ),
        in_specs=[qcol(W), qcol(W), qcol(LANES), full((Lp, W)), full((W, Lp)), full((Lp, LANES))],
        out_specs=pl.BlockSpec((1, TQ, W), lambda b_, i: (b_, i, 0)),
        out_shape=jax.ShapeDtypeStruct((B, Lp, W), F32),
        scratch_shapes=[
            pltpu.VMEM((Lp, TQ), I16),
            pltpu.VMEM((Lp, TQ), I16),
            pltpu.VMEM((Lp, TQ), F32),
            pltpu.VMEM((W, TQ), F32),
            pltpu.VMEM((8, TQ), I32),
            pltpu.VMEM((N_HEADS, KC, TQ), F32),
            pltpu.VMEM((N_HEADS, KC, TQ), BF16),
        ],
        compiler_params=_params("parallel", "arbitrary"),
        name="dsa_attn",
    )(qt, qit, wt, k, vt, kx)


def _outproj_router_body(x_ref, yr_ref, ya_ref, eg_ref, eb_ref, wo_ref, g1_ref, b1_ref, wrh_ref, wrl_ref, br_ref,
                         h_ref, grp_ref):
    h0 = _layer_norm(x_ref[0], eg_ref[...], eb_ref[...])
    mix = (_mm(yr_ref[0].astype(BF16), wo_ref[0:RWKV_WIDTH, :])
           + _mm(ya_ref[0].astype(BF16), wo_ref[RWKV_WIDTH:, :]))
    h1 = _layer_norm(DN_ALPHA * h0 + mix, g1_ref[...], b1_ref[...])
    h_ref[0, :, 0:D_MODEL] = h1
    logits = _dot3(_split(h1), (wrh_ref[...], wrl_ref[...]), _NN) + br_ref[...]
    lane = lax.broadcasted_iota(I32, (1, LANES), 1)
    lanef = lane.astype(F32)
    low = -3e38
    lgm = jnp.where(lane < N_GROUPS, logits, low)
    gmax = jnp.max(lgm, axis=1, keepdims=True)
    gsel = jnp.min(jnp.where(lgm == gmax, lanef, 1e9), axis=1, keepdims=True)
    gsum = jnp.sum(jnp.where(lane < N_GROUPS, jnp.exp(lgm - gmax), 0.0), axis=1, keepdims=True)
    group_of_lane = ((lane - EXPERT_LANE0) // EXPERTS_PER_GROUP).astype(F32)
    lem = jnp.where(group_of_lane == gsel, logits, low)
    m1 = jnp.max(lem, axis=1, keepdims=True)
    i1 = jnp.min(jnp.where(lem == m1, lanef, 1e9), axis=1, keepdims=True)
    lem2 = jnp.where(lanef == i1, low, lem)
    m2 = jnp.max(lem2, axis=1, keepdims=True)
    i2 = jnp.min(jnp.where(lem2 == m2, lanef, 1e9), axis=1, keepdims=True)
    e2 = jnp.exp(m2 - m1)
    w1 = 1.0 / (1.0 + e2)
    w2 = e2 / (1.0 + e2)
    gates = jnp.where(lanef == i1, w1, jnp.where(lanef == i2, w2, 0.0)) / gsum
    h_ref[0, :, D_MODEL:] = gates
    grp_ref[0] = jnp.broadcast_to(gsel, (ROWS_A, LANES))


def _outproj_router(x, y_r, y_a, eg, eb, wo, g1, b1, wrh, wrl, br):
    B, S, D = x.shape
    skip = PAD_ROWS // ROWS_A
    xrow = pl.BlockSpec((1, ROWS_A, D), lambda b_, i: (b_, i, 0))
    yrow = pl.BlockSpec((1, ROWS_A, RWKV_WIDTH), lambda b_, i: (b_, i + skip, 0))
    vec = _const_spec((1, D))
    return pl.pallas_call(
        _outproj_router_body,
        grid=(B, S // ROWS_A),
        in_specs=[xrow, yrow, yrow, vec, vec, _const_spec((D, D)), vec, vec,
                  _const_spec((D, LANES)), _const_spec((D, LANES)), _const_spec((1, LANES))],
        out_specs=[pl.BlockSpec((1, ROWS_A, D + LANES), lambda b_, i: (b_, i, 0)),
                   pl.BlockSpec((1, ROWS_A, LANES), lambda b_, i: (b_, i, 0))],
        out_shape=[jax.ShapeDtypeStruct((B, S, D + LANES), F32), jax.ShapeDtypeStruct((B, S, LANES), F32)],
        compiler_params=_params("parallel", "arbitrary"),
        name="outproj_router",
    )(x, y_r, y_a, eg, eb, wo, g1, b1, wrh, wrl, br)


def _moe_body(tgrp_ref, tcnt_ref, idx_ref, idx_next_ref, hx_ref, wg_ref, wu_ref, wd_ref, g2_ref, b2_ref, out_ref,
              xg_ref, acc_ref, hb_ref, ob_ref, gsem, ssem):
    i = pl.program_id(0)
    e = pl.program_id(1)
    n_tiles = pl.num_programs(0)
    n = tcnt_ref[i]
    slot = i % 2
    D = D_MODEL

    def row_gather(rows_ref, r, to_slot):
        return pltpu.make_async_copy(hx_ref.at[pl.ds(rows_ref[0, 0, r], 1)], xg_ref.at[to_slot, pl.ds(r, 1)],
                                     gsem.at[to_slot])

    def row_scatter(r):
        return pltpu.make_async_copy(ob_ref.at[pl.ds(r, 1)], out_ref.at[pl.ds(idx_ref[0, 0, r], 1)], ssem)

    def for_rows(count, fn):
        lax.fori_loop(0, count, lambda r, c: (fn(r), c)[1], 0)

    @pl.when(jnp.logical_and(i == 0, e == 0))
    def _():
        for_rows(n, lambda r: row_gather(idx_ref, r, slot).start())

    has_next = i + 1 < n_tiles
    n_next = tcnt_ref[jnp.minimum(i + 1, n_tiles - 1)]
    spread = jnp.logical_and(has_next, n_next == TMG)
    slice_rows = TMG // EXPERTS_PER_GROUP

    @pl.when(jnp.logical_and(jnp.logical_and(e == 1, has_next), n_next < TMG))
    def _():
        for_rows(n_next, lambda r: row_gather(idx_next_ref, r, 1 - slot).start())

    @pl.when(jnp.logical_and(e == 0, n == TMG))
    def _():
        pltpu.make_async_copy(hx_ref.at[pl.ds(0, TMG)], xg_ref.at[slot], gsem.at[slot]).wait()

    @pl.when(jnp.logical_and(e == 0, n < TMG))
    def _():
        for_rows(n, lambda r: row_gather(idx_ref, r, slot).wait())

    @pl.when(jnp.logical_and(e == 0, n > 0))
    def _():
        acc_ref[...] = jnp.zeros_like(acc_ref)
        hb_ref[...] = xg_ref[slot, :, 0:D].astype(BF16)

    def expert_step():
        t = hb_ref[...]
        lane = lax.broadcasted_iota(I32, (1, LANES), 1)
        gate_lane = EXPERT_LANE0 + tgrp_ref[i] * EXPERTS_PER_GROUP + e
        gcol = jnp.sum(jnp.where(lane == gate_lane, xg_ref[slot, :, D:], 0.0), axis=1, keepdims=True)
        a = _mm(t, wg_ref[0])
        hid = a * _sigmoid(a) * _mm(t, wu_ref[0]) * gcol
        acc_ref[...] += _mm(hid.astype(BF16), wd_ref[0])

    @pl.when(jnp.logical_and(n > 0, spread))
    def _():
        for j in range(slice_rows):
            row_gather(idx_next_ref, e * slice_rows + j, 1 - slot).start()
        expert_step()

    @pl.when(jnp.logical_and(n > 0, jnp.logical_not(spread)))
    def _():
        expert_step()

    last = e == EXPERTS_PER_GROUP - 1

    @pl.when(jnp.logical_and(last, n > 0))
    def _():
        ob_ref[...] = _layer_norm(DN_ALPHA * xg_ref[slot, :, 0:D] + acc_ref[...], g2_ref[...], b2_ref[...])

    @pl.when(jnp.logical_and(last, n == TMG))
    def _():
        lax.fori_loop(0, TMG, lambda r, c: (row_scatter(r).start(), c)[1], 0, unroll=slice_rows)
        pltpu.make_async_copy(ob_ref, out_ref.at[pl.ds(0, TMG)], ssem).wait()

    @pl.when(jnp.logical_and(last, n < TMG))
    def _():
        for_rows(n, lambda r: row_scatter(r).start())
        for_rows(n, lambda r: row_scatter(r).wait())


def _moe(hx, tile_group, tile_count, tile_rows, wg, wu, wd, g2, b2):
    T, DX = hx.shape
    D = D_MODEL
    n_tiles = tile_rows.shape[0]
    wspec = lambda shape: pl.BlockSpec(
        (1,) + shape, lambda i, e, tg, tc: (tg[i] * EXPERTS_PER_GROUP + e, 0, 0))
    vec = pl.BlockSpec((1, D), lambda i, e, tg, tc: (0, 0))
    rows_of = lambda step: pl.BlockSpec(
        (1, 1, TMG), lambda i, e, tg, tc: (jnp.minimum(i + step, n_tiles - 1), 0, 0), memory_space=pltpu.SMEM)
    return pl.pallas_call(
        _moe_body,
        grid_spec=pltpu.PrefetchScalarGridSpec(
            num_scalar_prefetch=2,
            grid=(n_tiles, EXPERTS_PER_GROUP),
            in_specs=[rows_of(0), rows_of(1), pl.BlockSpec(memory_space=pl.ANY),
                      wspec((D, D_EXPERT)), wspec((D, D_EXPERT)), wspec((D_EXPERT, D)), vec, vec],
            out_specs=pl.BlockSpec(memory_space=pl.ANY),
            scratch_shapes=[pltpu.VMEM((2, TMG, DX), F32), pltpu.VMEM((TMG, D), F32), pltpu.VMEM((TMG, D), BF16),
                            pltpu.VMEM((TMG, D), F32), pltpu.SemaphoreType.DMA((2,)), pltpu.SemaphoreType.DMA],
        ),
        out_shape=jax.ShapeDtypeStruct((T, D), F32),
        compiler_params=_params("arbitrary", "arbitrary"),
        name="moe",
    )(tile_group, tile_count, tile_rows, tile_rows, hx, wg, wu, wd, g2, b2)


def _group_tiles(grp, n_tiles):
    T = grp.shape[0]
    onehot = (grp[:, None] == jnp.arange(N_GROUPS)[None, :]).astype(I32)
    rank = jnp.cumsum(onehot, axis=0) - onehot
    count = jnp.sum(onehot, axis=0)
    tiles_per_group = (count + TMG - 1) // TMG
    first_tile = jnp.cumsum(tiles_per_group) - tiles_per_group
    pos = first_tile[grp] * TMG + jnp.sum(rank * onehot, axis=1)
    tile_rows = jnp.zeros((n_tiles * TMG,), I32).at[pos].set(jnp.arange(T, dtype=I32)).reshape(n_tiles, 1, TMG)
    tile = jnp.arange(n_tiles)
    tile_group = jnp.clip(jnp.sum((tile[:, None] >= first_tile[None, :]).astype(I32), axis=1) - 1, 0, N_GROUPS - 1)
    in_group = tile - first_tile[tile_group]
    tile_count = jnp.clip(count[tile_group] - in_group * TMG, 0, TMG)
    tile_count = jnp.where(in_group < tiles_per_group[tile_group], tile_count, 0)
    return tile_group.astype(I32), tile_count.astype(I32), tile_rows


def _rope_tables(Lp):
    pos = jnp.maximum(jnp.arange(Lp, dtype=I32) - OFF, 0).astype(F32)
    j = jnp.arange(LANES) % HEAD_DIM

    def table(half, rot_dim):
        inv = 1.0 / (ROPE_THETA ** (jnp.arange(half, dtype=F32) / half))
        ang = pos[:, None] * inv[None, :]
        cos, sin = jnp.cos(ang)[:, j % half], jnp.sin(ang)[:, j % half]
        rotated = (j < rot_dim)[None, :]
        sign = jnp.where(j < half, -1.0, 1.0)[None, :]
        return jnp.where(rotated, cos, 1.0), jnp.where(rotated, sin * sign, 0.0)

    cf, sf = table(HEAD_DIM // 2, HEAD_DIM)
    cp, sp = table(IDX_ROPE_DIM // 2, IDX_ROPE_DIM)
    return cf, sf, cp, sp


def _block_ones(n, block):
    idx = jnp.arange(n) // block
    return (idx[:, None] == idx[None, :]).astype(F32)


def kernel(x, meta_tokens, ln_emb_g, ln_emb_b, w_in, rw_mu, rw_w0, rw_w2, rw_a0, rw_a2, rw_g2, rw_kk, rw_ka,
           rw_rk, rw_lnx_g, rw_lnx_b, att_qnorm_g, att_wuq, idx_wq, idx_knorm_g, idx_knorm_b, w_out, ln1_g,
           ln1_b, rt_grp_w, rt_grp_b, rt_exp_w, rt_exp_b, ex_w_gate, ex_w_up, ex_w_down, ln2_g, ln2_b):
    B, S, D = x.shape
    assert w_in.shape[0] == 1 and D == D_MODEL and (B * S) % TMG == 0
    Lp = S + PAD_ROWS
    ksel = min(INDEX_TOPK, S // 4)
    row = lambda t: t.reshape(1, -1)
    W = RWKV_WIDTH

    meta_pad = jnp.zeros((PAD_ROWS, D), F32).at[OFF:].set(meta_tokens)
    w_in_p = jnp.pad(w_in[0], ((0, 0), (0, ATT_COLS_PAD - ATT_COLS))).astype(BF16)
    u_r, u_a = _ln_inproj(x, meta_pad, row(ln_emb_g), row(ln_emb_b), w_in_p)

    w2p = jnp.concatenate([rw_w2[0], jnp.zeros((ICLR_RANK, W), F32)], 0).astype(BF16)
    a2p = jnp.concatenate([jnp.zeros((DECAY_RANK, W), F32), rw_a2[0]], 0).astype(BF16)
    head_of_lane = jnp.arange(W) // HEAD_DIM
    hs = (head_of_lane[:, None] == jnp.arange(LANES)[None, :]).astype(BF16)
    tri = (jnp.arange(ROWS_R)[:, None] >= jnp.arange(ROWS_R)[None, :]).astype(F32)
    tri = (tri * _block_ones(ROWS_R, CHUNK)).astype(BF16)
    rp, kp, bp, ap, v, pc, g, bonus = _rwkv_prep(
        u_r, row(rw_mu[0]), row(rw_w0[0]), w2p, row(rw_a0[0]), a2p, rw_g2[0].astype(BF16), row(rw_kk[0]),
        row(rw_ka[0]), row(rw_rk[0]), hs, hs.T, tri)
    y_r = _rwkv_scan(rp, kp, bp, ap, v, pc, g, bonus, row(rw_lnx_g[0]), row(rw_lnx_b[0]))

    pad_lanes = lambda t: jnp.pad(t, (0, LANES - t.shape[0])).reshape(1, LANES)
    cf, sf, cp, sp = _rope_tables(Lp)
    qt, k, vt, qit, kx, wt = _dsa_prep(
        u_a, row(att_qnorm_g[0]), att_wuq[0].astype(BF16), idx_wq[0].astype(BF16),
        pad_lanes(idx_knorm_g[0]), pad_lanes(idx_knorm_b[0]), cf, sf, cp, sp)
    y_a = _dsa_attn(qt, k, vt, qit, kx, wt, ksel)

    wr = jnp.zeros((D, LANES), F32).at[:, :N_GROUPS].set(rt_grp_w[0])
    wr = wr.at[:, EXPERT_LANE0:EXPERT_LANE0 + N_EXPERTS].set(rt_exp_w[0])
    br = jnp.zeros((1, LANES), F32).at[0, :N_GROUPS].set(rt_grp_b[0])
    br = br.at[0, EXPERT_LANE0:EXPERT_LANE0 + N_EXPERTS].set(rt_exp_b[0])
    wrh = wr.astype(BF16)
    wrl = (wr - wrh.astype(F32)).astype(BF16)
    h1, grp = _outproj_router(x, y_r, y_a, row(ln_emb_g), row(ln_emb_b), w_out[0].astype(BF16),
                                row(ln1_g[0]), row(ln1_b[0]), wrh, wrl, br)
    T = B * S
    n_tiles = T // TMG + N_GROUPS
    tile_group, tile_count, tile_rows = _group_tiles(grp.reshape(T, LANES)[:, 0].astype(I32), n_tiles)
    out = _moe(h1.reshape(T, D + LANES), tile_group, tile_count, tile_rows, ex_w_gate[0].astype(BF16),
               ex_w_up[0].astype(BF16), ex_w_down[0].astype(BF16), row(ln2_g[0]), row(ln2_b[0]))
    return out.reshape(B, S, D)
```

```python
import functools

import jax
import jax.numpy as jnp
from jax import lax
from jax.experimental import pallas as pl
from jax.experimental.pallas import tpu as pltpu

F32 = jnp.float32
BF16 = jnp.bfloat16
I32 = jnp.int32
I16 = jnp.int16
HIGHEST = lax.Precision.HIGHEST

D_MODEL = 1024
N_META = 16
RWKV_WIDTH = 512
ATT_WIDTH = 512
HEAD_DIM = 64
N_HEADS = 8
DECAY_RANK = 64
ICLR_RANK = 64
GATE_RANK = 128
Q_LORA_RANK = 256
IDX_HEADS = 8
IDX_DIM = 64
IDX_ROPE_DIM = 32
INDEX_TOPK = 256
ROPE_THETA = 10000.0
N_GROUPS = 4
EXPERTS_PER_GROUP = 8
N_EXPERTS = N_GROUPS * EXPERTS_PER_GROUP
D_EXPERT = 256
DN_ALPHA = 2.0 ** 0.25
LN_EPS = 1e-5
RMS_EPS = 1e-6
GN_EPS = 64e-5
RWKV_COLS = 3 * RWKV_WIDTH + DECAY_RANK + ICLR_RANK + GATE_RANK
ATT_COLS = Q_LORA_RANK + 2 * ATT_WIDTH + IDX_DIM + IDX_HEADS
ATT_COLS_PAD = 1408

LANES = 128
PAD_ROWS = 256
OFF = PAD_ROWS - N_META
CHUNK = 64
ROWS_A = 256
ROWS_R = 256
TQ = 256
KC = 256
TMG = 512
SCAN_BATCH = 4
EXPERT_LANE0 = 64
NEG = -1e30
MASKED = -3e38
BELOW_ALL = -1e38
ABOVE_ALL = 3e38
MAX_REFINE = 400
INT_MIN = -2147483648
HALF16 = 32768
VMEM_LIMIT = 56 * 1024 * 1024


def _mm(a, b, precision=None):
    return jnp.dot(a, b, preferred_element_type=F32, precision=precision)


def _mm_nt(a, b, precision=None):
    return lax.dot_general(a, b, (((1,), (1,)), ((), ())), preferred_element_type=F32, precision=precision)


def _mm_tn(a, b, precision=None):
    return lax.dot_general(a, b, (((0,), (0,)), ((), ())), preferred_element_type=F32, precision=precision)


def _sigmoid(x):
    return 1.0 / (1.0 + jnp.exp(-x))


def _layer_norm(x, g, b):
    mu = jnp.mean(x, -1, keepdims=True)
    xc = x - mu
    var = jnp.mean(xc * xc, -1, keepdims=True)
    return xc * lax.rsqrt(var + LN_EPS) * g + b


def _params(*sem):
    return pltpu.CompilerParams(dimension_semantics=sem, vmem_limit_bytes=VMEM_LIMIT)


def _const_spec(shape):
    nd = len(shape)
    return pl.BlockSpec(shape, lambda *_: (0,) * nd)


def _ln_inproj_body(x_ref, meta_ref, g_ref, b_ref, w_ref, ur_ref, ua_ref):
    blk = pl.program_id(1)
    xin = jnp.where(blk == 0, meta_ref[...], x_ref[0])
    h = _layer_norm(xin, g_ref[...], b_ref[...])
    row = lax.broadcasted_iota(I32, (ROWS_A, 1), 0)
    h = jnp.where((blk > 0) | (row >= OFF), h, 0.0)
    hb = h.astype(BF16)
    step = 256
    for n0 in range(0, RWKV_COLS, step):
        n1 = min(n0 + step, RWKV_COLS)
        ur_ref[0, :, n0:n1] = _mm(hb, w_ref[:, n0:n1])
    for n0 in range(0, ATT_COLS_PAD, step):
        n1 = min(n0 + step, ATT_COLS_PAD)
        ua_ref[0, :, n0:n1] = _mm(hb, w_ref[:, RWKV_COLS + n0:RWKV_COLS + n1])


def _ln_inproj(x, meta_pad, g, b, w):
    B, S, D = x.shape
    nblk = (S + PAD_ROWS) // ROWS_A
    Lp = S + PAD_ROWS
    ncols = RWKV_COLS + ATT_COLS_PAD
    return pl.pallas_call(
        _ln_inproj_body,
        grid=(B, nblk),
        in_specs=[
            pl.BlockSpec((1, ROWS_A, D), lambda b_, i: (b_, jnp.maximum(i - 1, 0), 0)),
            _const_spec((ROWS_A, D)),
            _const_spec((1, D)),
            _const_spec((1, D)),
            _const_spec((D, ncols)),
        ],
        out_specs=[
            pl.BlockSpec((1, ROWS_A, RWKV_COLS), lambda b_, i: (b_, i, 0)),
            pl.BlockSpec((1, ROWS_A, ATT_COLS_PAD), lambda b_, i: (b_, i, 0)),
        ],
        out_shape=[
            jax.ShapeDtypeStruct((B, Lp, RWKV_COLS), F32),
            jax.ShapeDtypeStruct((B, Lp, ATT_COLS_PAD), F32),
        ],
        compiler_params=_params("parallel", "arbitrary"),
        name="ln_inproj",
    )(x, meta_pad, g, b, w)


def _split3(x):
    hi = x.astype(BF16)
    r1 = x - hi.astype(F32)
    mid = r1.astype(BF16)
    return hi, mid, (r1 - mid.astype(F32)).astype(BF16)


def _mm_exact_rhs(x, m):
    return sum(_mm(p, m) for p in _split3(x))


def _mm_exact_lhs(m, x):
    return sum(_mm(m, p) for p in _split3(x))


def _rwkv_prep_body(u_ref, prev_ref, mu_ref, w0_ref, w2_ref, a0_ref, a2_ref, g2_ref, kk_ref, ka_ref, rk_ref,
                    hs_ref, hb_ref, tri_ref,
                    rp_ref, kp_ref, bp_ref, ap_ref, v_ref, pc_ref, g_ref, bonus_ref):
    blk = pl.program_id(1)
    u = u_ref[0]
    prev = jnp.where(blk == 0, 0.0, prev_ref[0][7:8, :])
    row = lax.broadcasted_iota(I32, (ROWS_R, 1), 0)
    shifted = jnp.where(row == 0, prev, pltpu.roll(u, 1, 0))
    ul = u + (shifted - u) * mu_ref[...]
    W = RWKV_WIDTH
    r = ul[:, 0:W]
    k = ul[:, W:2 * W]
    v = ul[:, 2 * W:3 * W]
    wa = ul[:, 3 * W:3 * W + 128]
    gd = ul[:, 3 * W + 128:3 * W + 256]
    w = w0_ref[...] + _mm(jnp.tanh(wa).astype(BF16), w2_ref[...])
    softplus_neg_w = jnp.maximum(-w, 0.0) + jnp.log(1.0 + jnp.exp(-jnp.abs(w)))
    logd = -jnp.exp(-softplus_neg_w - 0.5)
    a = _sigmoid(a0_ref[...] + _mm(wa.astype(BF16), a2_ref[...]))
    g_ref[0] = _mm(_sigmoid(gd).astype(BF16), g2_ref[...])
    head_sum = lambda t: _mm_exact_rhs(_mm_exact_rhs(t, hs_ref[...]), hb_ref[...])
    kkr = k * kk_ref[...]
    kk = kkr / jnp.maximum(jnp.sqrt(head_sum(kkr * kkr)), 1e-12)
    kmod = k * (1.0 + (a - 1.0) * ka_ref[...])
    bonus_ref[0] = head_sum(r * kmod * rk_ref[...]) * v
    v_ref[0] = v
    cum = _mm_exact_lhs(tri_ref[...], logd)
    rp_ref[0] = r * jnp.exp(cum)
    einv = jnp.exp(-cum)
    kp_ref[0] = kmod * einv
    bp_ref[0] = kk * a * einv
    ap_ref[0] = -kk * jnp.exp(cum - logd)
    for c in range(ROWS_R // CHUNK):
        last = c * CHUNK + CHUNK - 1
        pc_ref[0, c] = jnp.broadcast_to(jnp.exp(cum[last:last + 1]), (8, W))


def _rwkv_prep(u_r, mu, w0, w2p, a0, a2p, g2, k_k, k_a, r_k, hs, hb, tri):
    B, Lp, _ = u_r.shape
    nblk = Lp // ROWS_R
    W = RWKV_WIDTH
    row_spec = pl.BlockSpec((1, ROWS_R, W), lambda b_, i: (b_, i, 0))
    row_shape = jax.ShapeDtypeStruct((B, Lp, W), F32)
    cpb = ROWS_R // CHUNK
    return pl.pallas_call(
        _rwkv_prep_body,
        grid=(B, nblk),
        in_specs=[
            pl.BlockSpec((1, ROWS_R, RWKV_COLS), lambda b_, i: (b_, i, 0)),
            pl.BlockSpec((1, 8, RWKV_COLS), lambda b_, i: (b_, jnp.maximum(i * (ROWS_R // 8) - 1, 0), 0)),
            _const_spec((1, RWKV_COLS)),
            _const_spec((1, W)),
            _const_spec((128, W)),
            _const_spec((1, W)),
            _const_spec((128, W)),
            _const_spec((128, W)),
            _const_spec((1, W)),
            _const_spec((1, W)),
            _const_spec((1, W)),
            _const_spec((W, LANES)),
            _const_spec((LANES, W)),
            _const_spec((ROWS_R, ROWS_R)),
        ],
        out_specs=[row_spec] * 5 + [pl.BlockSpec((1, cpb, 8, W), lambda b_, i: (b_, i, 0, 0))] + [row_spec] * 2,
        out_shape=[row_shape] * 5 + [jax.ShapeDtypeStruct((B, Lp // CHUNK, 8, W), F32)] + [row_shape] * 2,
        compiler_params=_params("parallel", "arbitrary"),
        name="rwkv_prep",
    )(u_r, u_r, mu, w0, w2p, a0, a2p, g2, k_k, k_a, r_k, hs, hb, tri)


def _rwkv_scan_body(rp_ref, kp_ref, bp_ref, ap_ref, v_ref, pc_ref, g_ref, bonus_ref, lg_ref, lb_ref, o_ref, s_ref,
                    *, nb):
    c = pl.program_id(1)

    @pl.when(c == 0)
    def _():
        s_ref[...] = jnp.zeros_like(s_ref)

    @pl.when(c < OFF // CHUNK)
    def _():
        o_ref[...] = jnp.zeros_like(o_ref)

    @pl.when(c >= OFF // CHUNK)
    def _():
        _rwkv_chunk(rp_ref, kp_ref, bp_ref, ap_ref, v_ref, pc_ref, g_ref, bonus_ref, lg_ref, lb_ref, o_ref, s_ref, nb)


def _split(x):
    hi = x.astype(BF16)
    return hi, (x - hi.astype(F32)).astype(BF16)


def _dot3(a, b, dims):
    dg = lambda p, q: lax.dot_general(p, q, (dims, ((), ())), preferred_element_type=F32)
    return dg(a[0], b[0]) + dg(a[0], b[1]) + dg(a[1], b[0])


_NN = ((1,), (0,))
_NT = ((1,), (1,))
_TN = ((0,), (0,))


def _rwkv_chunk(rp_ref, kp_ref, bp_ref, ap_ref, v_ref, pc_ref, g_ref, bonus_ref, lg_ref, lb_ref, o_ref, s_ref, nb):
    C, N = CHUNK, HEAD_DIM
    ri = lax.broadcasted_iota(I32, (C, C), 0)
    ci = lax.broadcasted_iota(I32, (C, C), 1)
    strict = ri > ci
    incl = ri >= ci
    eye = jnp.where(ri == ci, 1.0, 0.0)
    units = [(b, slice(h * N, (h + 1) * N)) for b in range(nb) for h in range(N_HEADS)]
    ids = range(len(units))
    rows2 = lambda top, bottom: jnp.concatenate([top, bottom], axis=0)
    pcs = [pc_ref[b, 0, 0:1, sl] for b, sl in units]
    ar = [_split(rows2(ap_ref[b, :, sl], rp_ref[b, :, sl])) for b, sl in units]
    bk = [_split(rows2(bp_ref[b, :, sl], kp_ref[b, :, sl])) for b, sl in units]
    v_ = [_split(v_ref[b, :, sl]) for b, sl in units]
    s0 = [s_ref[i] for i in ids]
    s0s = [_split(s) for s in s0]
    gram = [_dot3(ar[i], bk[i], _NT) for i in ids]
    a_ab = [jnp.where(strict, gram[i][0:C, 0:C], 0.0) for i in ids]
    a_ak = [jnp.where(strict, gram[i][0:C, C:2 * C], 0.0) for i in ids]
    a_rb = [_split(jnp.where(incl, gram[i][C:2 * C, 0:C], 0.0)) for i in ids]
    a_rk = [jnp.where(incl, gram[i][C:2 * C, C:2 * C], 0.0) for i in ids]
    t = [eye + a_ab[i] for i in ids]
    pb = [a_ab[i].astype(BF16) for i in ids]
    for _ in range(C.bit_length() - 2):
        pb = [_mm(pb[i], pb[i]).astype(BF16) for i in ids]
        t = [t[i] + _mm(t[i].astype(BF16), pb[i]) for i in ids]
    ts = [_split(t[i]) for i in ids]
    resid = [(eye - t[i]) + _dot3(_split(a_ab[i]), ts[i], _NN) for i in ids]
    t = [t[i] + _mm(ts[i][0], resid[i].astype(BF16)) for i in ids]
    sp = [_dot3(ar[i], s0s[i], _NT) for i in ids]
    av = [_dot3(_split(rows2(a_ak[i], a_rk[i])), v_[i], _NN) for i in ids]
    u_ = [_dot3(_split(t[i]), _split(sp[i][0:C] + av[i][0:C]), _NN) for i in ids]
    for i, (b, sl) in enumerate(units):
        o = sp[i][C:2 * C] + av[i][C:2 * C] + _dot3(a_rb[i], _split(u_[i]), _NN)
        oc = o - jnp.mean(o, axis=1, keepdims=True)
        var = jnp.mean(oc * oc, axis=1, keepdims=True)
        y = oc * lax.rsqrt(var + GN_EPS) * lg_ref[:, sl] + lb_ref[:, sl]
        o_ref[b, :, sl] = (y + bonus_ref[b, :, sl]) * g_ref[b, :, sl]
    for i, (b, sl) in enumerate(units):
        vu = _split(rows2(v_ref[b, :, sl], u_[i]))
        kb = _split(rows2(kp_ref[b, :, sl], bp_ref[b, :, sl]) * pcs[i])
        s_ref[i] = s0[i] * pcs[i] + _dot3(vu, kb, _TN)


def _rwkv_scan(rp, kp, bp, ap, v, pc, g, bonus, lg, lb):
    B, Lp, W = rp.shape
    nch = Lp // CHUNK
    nb = SCAN_BATCH if B % SCAN_BATCH == 0 else 1
    row_spec = pl.BlockSpec((nb, CHUNK, W), lambda b_, c: (b_, c, 0))
    return pl.pallas_call(
        functools.partial(_rwkv_scan_body, nb=nb),
        grid=(B // nb, nch),
        in_specs=[row_spec] * 5 + [pl.BlockSpec((nb, 1, 8, W), lambda b_, c: (b_, c, 0, 0))] + [row_spec] * 2
        + [_const_spec((1, W))] * 2,
        out_specs=row_spec,
        out_shape=jax.ShapeDtypeStruct((B, Lp, W), F32),
        scratch_shapes=[pltpu.VMEM((nb * N_HEADS, HEAD_DIM, HEAD_DIM), F32)],
        compiler_params=_params("parallel", "arbitrary"),
        name="rwkv_scan",
    )(rp, kp, bp, ap, v, pc, g, bonus, lg, lb)


def _rope(x, cos, sin, half, first):
    width = x.shape[1]
    rot = jnp.where(first, pltpu.roll(x, width - half, 1), pltpu.roll(x, half, 1))
    return x * cos + rot * sin


def _dsa_prep_body(u_ref, qg_ref, wuq_ref, wiq_ref, kng_ref, knb_ref, cf_ref, sf_ref, cp_ref, sp_ref,
                   qt_ref, k_ref, vt_ref, qit_ref, kx_ref, wt_ref):
    u = u_ref[0]
    cq = u[:, 0:Q_LORA_RANK]
    k = u[:, Q_LORA_RANK:Q_LORA_RANK + ATT_WIDTH]
    v = u[:, Q_LORA_RANK + ATT_WIDTH:Q_LORA_RANK + 2 * ATT_WIDTH]
    tail = u[:, Q_LORA_RANK + 2 * ATT_WIDTH:]
    cqn = (cq * lax.rsqrt(jnp.mean(cq * cq, -1, keepdims=True) + RMS_EPS) * qg_ref[...]).astype(BF16)
    reps = ATT_WIDTH // LANES
    cf = jnp.concatenate([cf_ref[...]] * reps, axis=1)
    sf = jnp.concatenate([sf_ref[...]] * reps, axis=1)
    cp = jnp.concatenate([cp_ref[...]] * reps, axis=1)
    sp = jnp.concatenate([sp_ref[...]] * reps, axis=1)
    lane_w = lax.broadcasted_iota(I32, (1, ATT_WIDTH), 1) % HEAD_DIM
    first_f = lane_w < HEAD_DIM // 2
    first_p = lane_w < IDX_ROPE_DIM // 2
    q = _rope(_mm(cqn, wuq_ref[...]), cf, sf, HEAD_DIM // 2, first_f)
    qt_ref[0] = (q * (HEAD_DIM ** -0.5)).T.astype(BF16)
    qi = _rope(_mm(cqn, wiq_ref[...]), cp, sp, IDX_ROPE_DIM // 2, first_p)
    qit_ref[0] = qi.T.astype(BF16)
    k_ref[0] = _rope(k, cf, sf, HEAD_DIM // 2, first_f).astype(BF16)
    vt_ref[0] = v.T.astype(BF16)
    lane = lax.broadcasted_iota(I32, (1, LANES), 1)
    is_key = lane < IDX_DIM
    mu = jnp.sum(jnp.where(is_key, tail, 0.0), -1, keepdims=True) * (1.0 / IDX_DIM)
    tc = jnp.where(is_key, tail - mu, 0.0)
    var = jnp.sum(tc * tc, -1, keepdims=True) * (1.0 / IDX_DIM)
    kn = tc * lax.rsqrt(var + LN_EPS) * kng_ref[...] + knb_ref[...]
    kn = _rope(kn, cp_ref[...], sp_ref[...], IDX_ROPE_DIM // 2, (lane % HEAD_DIM) < IDX_ROPE_DIM // 2)
    kn = kn * (IDX_DIM ** -0.5)
    kx_ref[0] = jnp.where(is_key, kn, pltpu.roll(kn, IDX_DIM, 1)).astype(BF16)
    wt_ref[0] = (tail * (IDX_HEADS ** -0.5)).T


def _dsa_prep(u_a, qg, wuq, wiq, kng, knb, cf, sf, cp, sp):
    B, Lp, _ = u_a.shape
    W = ATT_WIDTH
    row = lambda width: pl.BlockSpec((1, ROWS_A, width), lambda b_, i: (b_, i, 0))
    col = lambda height: pl.BlockSpec((1, height, ROWS_A), lambda b_, i: (b_, 0, i))
    tab = pl.BlockSpec((ROWS_A, LANES), lambda b_, i: (i, 0))
    return pl.pallas_call(
        _dsa_prep_body,
        grid=(B, Lp // ROWS_A),
        in_specs=[row(ATT_COLS_PAD), _const_spec((1, Q_LORA_RANK)), _const_spec((Q_LORA_RANK, W)),
                  _const_spec((Q_LORA_RANK, W)), _const_spec((1, LANES)), _const_spec((1, LANES)),
                  tab, tab, tab, tab],
        out_specs=[col(W), row(W), col(W), col(W), row(LANES), col(LANES)],
        out_shape=[jax.ShapeDtypeStruct((B, W, Lp), BF16), jax.ShapeDtypeStruct((B, Lp, W), BF16),
                   jax.ShapeDtypeStruct((B, W, Lp), BF16), jax.ShapeDtypeStruct((B, W, Lp), BF16),
                   jax.ShapeDtypeStruct((B, Lp, LANES), BF16), jax.ShapeDtypeStruct((B, LANES, Lp), F32)],
        compiler_params=_params("parallel", "arbitrary"),
        name="dsa_prep",
    )(u_a, qg, wuq, wiq, kng, knb, cf, sf, cp, sp)


def _fold_rows(x, op=jnp.add):
    parts = [x[8 * r:8 * r + 8] for r in range(x.shape[0] // 8)]
    while len(parts) > 1:
        parts = [op(a, b) for a, b in zip(parts[0::2], parts[1::2])] + parts[len(parts) & ~1:]
    return parts[0]


def _dsa_attn_body(qt_ref, qit_ref, wt_ref, k_ref, vt_ref, kx_ref, o_ref, khi_ref, klo_ref, sc_ref, acc_ref,
                   j_ref, s_ref, pe_ref, *, ksel):
    i = pl.program_id(1)
    nkc = (i * TQ + TQ - 1) // KC + 1
    kf = float(ksel)
    tcol = i * TQ + lax.broadcasted_iota(I32, (1, TQ), 1)
    int_min = jnp.int32(INT_MIN)
    row_in_pair = lax.broadcasted_iota(I32, (LANES, 1), 0)

    def head_operands(ref):
        out = []
        for h in range(N_HEADS):
            pair = ref[0, LANES * (h // 2):LANES * (h // 2 + 1), :]
            keep = (row_in_pair < HEAD_DIM) if h % 2 == 0 else (row_in_pair >= HEAD_DIM)
            out.append(jnp.where(keep, pair, jnp.zeros_like(pair)))
        return out

    def key_rows(ks):
        return ks + lax.broadcasted_iota(I32, (KC, 1), 0)

    qis = head_operands(qit_ref)
    wrows = [wt_ref[0, IDX_DIM + h:IDX_DIM + h + 1, :] for h in range(IDX_HEADS)]

    def score_chunk(kc, carry):
        lo8, hi8 = carry
        ks = pl.multiple_of(kc * KC, KC)
        kx = kx_ref[0, pl.ds(ks, KC), :]
        sc = jnp.zeros((KC, TQ), F32)
        for h in range(IDX_HEADS):
            sc = sc + jnp.maximum(_mm(kx, qis[h]), 0.0) * wrows[h]
        sc = sc + 0.0
        krow = key_rows(ks)
        sc = jnp.where(krow >= OFF, sc, MASKED)
        sc = jnp.where(krow <= tcol, sc, MASKED)
        sc_ref[pl.ds(ks, KC), :] = sc
        bits = lax.bitcast_convert_type(sc, I32)
        key = jnp.where(bits >= 0, bits, bits ^ jnp.int32(0x7FFFFFFF))
        khi_ref[pl.ds(ks, KC), :] = lax.shift_right_arithmetic(key, 16).astype(I16)
        klo_ref[pl.ds(ks, KC), :] = ((key & jnp.int32(0xFFFF)) - HALF16).astype(I16)
        lo8 = jnp.minimum(lo8, _fold_rows(jnp.where(sc <= MASKED, ABOVE_ALL, sc), jnp.minimum))
        hi8 = jnp.maximum(hi8, _fold_rows(sc, jnp.maximum))
        return lo8, hi8

    lo8, hi8 = lax.fori_loop(0, nkc, score_chunk,
                             (jnp.full((8, TQ), ABOVE_ALL, F32), jnp.full((8, TQ), MASKED, F32)))
    smin = jnp.min(lo8, axis=0, keepdims=True)
    smax = jnp.max(hi8, axis=0, keepdims=True)

    def scan_chunks(fn, init):
        def body(kc, carry):
            ks = pl.multiple_of(kc * KC, KC)
            return fn(carry, ks)
        return lax.fori_loop(0, nkc, body, init)

    zeros8 = jnp.zeros((8, TQ), F32)

    def count_where(ref, pred):
        cnt = scan_chunks(lambda c, ks: c + _fold_rows(pred(ref[pl.ds(ks, KC), :], ks)), zeros8)
        return jnp.sum(cnt, axis=0, keepdims=True)

    def count16(ref, pred):
        def fn(cnt, ks):
            m = pred(ref[pl.ds(ks, KC), :])
            parts = [m[16 * r:16 * r + 16] for r in range(KC // 16)]
            while len(parts) > 1:
                parts = [a + b for a, b in zip(parts[0::2], parts[1::2])]
            return cnt + parts[0]
        cnt = scan_chunks(fn, jnp.zeros((16, TQ), I16))
        return jnp.sum(cnt.astype(I32), axis=0, keepdims=True)

    one16, zero16 = jnp.int16(1), jnp.int16(0)

    def radix16(ref, target):
        def bit(bi, prefix):
            cand = prefix | lax.shift_left(jnp.int32(1), 15 - bi)
            cand16 = (cand - HALF16).astype(I16)
            cnt = count16(ref, lambda x: jnp.where(x >= cand16, one16, zero16))
            return jnp.where(cnt >= target, cand, prefix)
        return lax.fori_loop(0, 16, bit, jnp.zeros((1, TQ), I32))

    k_int = jnp.full((1, TQ), ksel, I32)
    thr_hi = radix16(khi_ref, k_int) - HALF16
    thr_hi16 = thr_hi.astype(I16)
    above = count16(khi_ref, lambda x: jnp.where(x > thr_hi16, one16, zero16))

    def keep_low_of_ties(carry, ks):
        rows = pl.ds(ks, KC)
        klo_ref[rows, :] = jnp.where(khi_ref[rows, :] == thr_hi16, klo_ref[rows, :], jnp.int16(-HALF16))
        return carry

    scan_chunks(keep_low_of_ties, 0)
    thr_lo = radix16(klo_ref, k_int - above)
    thr_key = lax.shift_left(thr_hi, 16) | thr_lo
    cand0 = lax.bitcast_convert_type(jnp.where(thr_key >= 0, thr_key, thr_key ^ jnp.int32(0x7FFFFFFF)), F32)

    n_adm = jnp.maximum(tcol - (OFF - 1), 0).astype(F32)
    searching = n_adm > kf

    def probe(mid):
        def fn(carry, ks):
            cnt, vmin = carry
            s = sc_ref[pl.ds(ks, KC), :]
            ge = s >= mid
            return (cnt + _fold_rows(jnp.where(ge, 1.0, 0.0)),
                    jnp.minimum(vmin, _fold_rows(jnp.where(ge, s, ABOVE_ALL), jnp.minimum)))
        cnt, vmin = scan_chunks(fn, (zeros8, jnp.full((8, TQ), ABOVE_ALL, F32)))
        return jnp.sum(cnt, axis=0, keepdims=True), jnp.min(vmin, axis=0, keepdims=True)

    def refine(state):
        it, lo, hi, c_lo, c_gt, done, _ = state
        mid = jnp.where(it == 0, jnp.where(searching, cand0, lo), lo + 0.5 * (hi - lo))
        c_mid, v_mid = probe(mid)
        up = c_mid >= kf
        lo_n = jnp.where(up, v_mid, lo)
        hi_n = jnp.where(up, hi, mid)
        c_lo_n = jnp.where(up, c_mid, c_lo)
        c_gt_n = count_where(sc_ref, lambda s, ks: jnp.where(s > lo_n, 1.0, 0.0))
        stalled = jnp.where(it > 0, jnp.where(mid <= lo, 1.0, jnp.where(mid >= hi, 1.0, 0.0)), 0.0)
        fin = jnp.maximum(jnp.where(c_gt_n < kf, 1.0, 0.0), stalled)
        frozen = done > 0.0
        keep = lambda old, new_: jnp.where(frozen, old, new_)
        done_n = jnp.maximum(done, fin)
        return (it + 1, keep(lo, lo_n), keep(hi, hi_n), keep(c_lo, c_lo_n), keep(c_gt, c_gt_n), done_n,
                jnp.max(1.0 - done_n))

    done0 = jnp.where(searching, 0.0, 1.0)
    state0 = (jnp.int32(0), smin, smax + (jnp.abs(smax) + 1.0) * 1e-6, n_adm, n_adm, done0, jnp.max(1.0 - done0))
    state = lax.while_loop(lambda st: jnp.logical_and(st[6] > 0.0, st[0] < MAX_REFINE), refine, state0)
    thr = jnp.where(searching, state[1], BELOW_ALL)
    cnt_gt = jnp.where(searching, state[4], n_adm)
    cnt_eq = jnp.where(searching, state[3] - state[4], 0.0)
    need = kf - cnt_gt

    j_ref[...] = jnp.full(j_ref.shape, 2 ** 30, I32)

    @pl.when(jnp.max(cnt_eq - need) > 0.0)
    def _():
        def index_bit(bi, prefix):
            cand = prefix | lax.shift_left(jnp.int32(1), 12 - bi)
            before = count_where(
                sc_ref, lambda s, ks: jnp.where(s == thr, jnp.where(key_rows(ks) < cand, 1.0, 0.0), 0.0))
            return jnp.where(before < need, cand, prefix)
        jst = lax.fori_loop(0, 13, index_bit, jnp.zeros((1, TQ), I32))
        j_ref[...] = jnp.broadcast_to(jst, j_ref.shape)

    jstar = j_ref[0:1, :]

    qs = head_operands(qt_ref)
    acc_ref[...] = jnp.zeros_like(acc_ref)

    ones_rows = jnp.ones((16, KC), BF16)

    def attend_chunk(kc, carry):
        ms, ls = carry
        ks = pl.multiple_of(kc * KC, KC)
        sc = sc_ref[pl.ds(ks, KC), :]
        tie = jnp.where(sc == thr, jnp.where(key_rows(ks) <= jstar, 0.0, NEG), NEG)
        bias = jnp.where(sc > thr, 0.0, tie)
        chunk_max = []
        for h in range(N_HEADS):
            p = h // 2
            kp = k_ref[0, pl.ds(ks, KC), LANES * p:LANES * (p + 1)]
            s = _mm(kp, qs[h]) + bias
            s_ref[h] = s
            chunk_max.append(jnp.max(s, axis=0, keepdims=True))
        new_ms, new_ls, alphas = [], [], []
        for h in range(N_HEADS):
            m_new = jnp.maximum(ms[h], chunk_max[h])
            alphas.append(jnp.exp(ms[h] - m_new))
            new_ms.append(m_new)
            pe_ref[h] = jnp.exp(s_ref[h] - m_new).astype(BF16)
        for h in range(N_HEADS):
            vt = vt_ref[0, HEAD_DIM * h:HEAD_DIM * (h + 1), pl.ds(ks, KC)]
            pv = _mm(jnp.concatenate([vt, ones_rows], axis=0), pe_ref[h])
            rows = slice(HEAD_DIM * h, HEAD_DIM * (h + 1))
            acc_ref[rows, :] = acc_ref[rows, :] * alphas[h] + pv[0:HEAD_DIM]
            new_ls.append(alphas[h] * ls[h] + pv[HEAD_DIM:HEAD_DIM + 1])
        return tuple(new_ms), tuple(new_ls)

    init = (tuple(jnp.full((1, TQ), NEG, F32) for _ in range(N_HEADS)),
            tuple(jnp.zeros((1, TQ), F32) for _ in range(N_HEADS)))
    _, ls = lax.fori_loop(0, nkc, attend_chunk, init)
    for p in range(N_HEADS // 2):
        parts = [acc_ref[HEAD_DIM * h:HEAD_DIM * (h + 1), :] / ls[h] for h in (2 * p, 2 * p + 1)]
        o_ref[0, :, LANES * p:LANES * (p + 1)] = jnp.concatenate(parts, axis=0).T


def _dsa_attn(qt, k, vt, qit, kx, wt, ksel):
    B, Lp, W = k.shape
    assert Lp % KC == 0 and Lp <= 8192
    qcol = lambda height: pl.BlockSpec((1, height, TQ), lambda b_, i: (b_, 0, i))
    full = lambda shape: pl.BlockSpec((1,) + shape, lambda b_, i: (b_, 0, 0))
    return pl.pallas_call(
        functools.partial(_dsa_attn_body, ksel=ksel),
        grid=(B, Lp // TQ),
        in_specs=[qcol(W), qcol(W), qcol(LANES), full((Lp, W)), full((W, Lp)), full((Lp, LANES))],
        out_specs=pl.BlockSpec((1, TQ, W), lambda b_, i: (b_, i, 0)),
        out_shape=jax.ShapeDtypeStruct((B, Lp, W), F32),
        scratch_shapes=[
            pltpu.VMEM((Lp, TQ), I16),
            pltpu.VMEM((Lp, TQ), I16),
            pltpu.VMEM((Lp, TQ), F32),
            pltpu.VMEM((W, TQ), F32),
            pltpu.VMEM((8, TQ), I32),
            pltpu.VMEM((N_HEADS, KC, TQ), F32),
            pltpu.VMEM((N_HEADS, KC, TQ), BF16),
        ],
        compiler_params=_params("parallel", "arbitrary"),
        name="dsa_attn",
    )(qt, qit, wt, k, vt, kx)


def _outproj_router_body(x_ref, yr_ref, ya_ref, eg_ref, eb_ref, wo_ref, g1_ref, b1_ref, wrh_ref, wrl_ref, br_ref,
                         h_ref, grp_ref):
    h0 = _layer_norm(x_ref[0], eg_ref[...], eb_ref[...])
    mix = (_mm(yr_ref[0].astype(BF16), wo_ref[0:RWKV_WIDTH, :])
           + _mm(ya_ref[0].astype(BF16), wo_ref[RWKV_WIDTH:, :]))
    h1 = _layer_norm(DN_ALPHA * h0 + mix, g1_ref[...], b1_ref[...])
    h_ref[0, :, 0:D_MODEL] = h1
    logits = _dot3(_split(h1), (wrh_ref[...], wrl_ref[...]), _NN) + br_ref[...]
    lane = lax.broadcasted_iota(I32, (1, LANES), 1)
    lanef = lane.astype(F32)
    low = -3e38
    lgm = jnp.where(lane < N_GROUPS, logits, low)
    gmax = jnp.max(lgm, axis=1, keepdims=True)
    gsel = jnp.min(jnp.where(lgm == gmax, lanef, 1e9), axis=1, keepdims=True)
    gsum = jnp.sum(jnp.where(lane < N_GROUPS, jnp.exp(lgm - gmax), 0.0), axis=1, keepdims=True)
    group_of_lane = ((lane - EXPERT_LANE0) // EXPERTS_PER_GROUP).astype(F32)
    lem = jnp.where(group_of_lane == gsel, logits, low)
    m1 = jnp.max(lem, axis=1, keepdims=True)
    i1 = jnp.min(jnp.where(lem == m1, lanef, 1e9), axis=1, keepdims=True)
    lem2 = jnp.where(lanef == i1, low, lem)
    m2 = jnp.max(lem2, axis=1, keepdims=True)
    i2 = jnp.min(jnp.where(lem2 == m2, lanef, 1e9), axis=1, keepdims=True)
    e2 = jnp.exp(m2 - m1)
    w1 = 1.0 / (1.0 + e2)
    w2 = e2 / (1.0 + e2)
    gates = jnp.where(lanef == i1, w1, jnp.where(lanef == i2, w2, 0.0)) / gsum
    h_ref[0, :, D_MODEL:] = gates
    grp_ref[0] = jnp.broadcast_to(gsel, (ROWS_A, LANES))


def _outproj_router(x, y_r, y_a, eg, eb, wo, g1, b1, wrh, wrl, br):
    B, S, D = x.shape
    skip = PAD_ROWS // ROWS_A
    xrow = pl.BlockSpec((1, ROWS_A, D), lambda b_, i: (b_, i, 0))
    yrow = pl.BlockSpec((1, ROWS_A, RWKV_WIDTH), lambda b_, i: (b_, i + skip, 0))
    vec = _const_spec((1, D))
    return pl.pallas_call(
        _outproj_router_body,
        grid=(B, S // ROWS_A),
        in_specs=[xrow, yrow, yrow, vec, vec, _const_spec((D, D)), vec, vec,
                  _const_spec((D, LANES)), _const_spec((D, LANES)), _const_spec((1, LANES))],
        out_specs=[pl.BlockSpec((1, ROWS_A, D + LANES), lambda b_, i: (b_, i, 0)),
                   pl.BlockSpec((1, ROWS_A, LANES), lambda b_, i: (b_, i, 0))],
        out_shape=[jax.ShapeDtypeStruct((B, S, D + LANES), F32), jax.ShapeDtypeStruct((B, S, LANES), F32)],
        compiler_params=_params("parallel", "arbitrary"),
        name="outproj_router",
    )(x, y_r, y_a, eg, eb, wo, g1, b1, wrh, wrl, br)


def _moe_body(tgrp_ref, tcnt_ref, idx_ref, idx_next_ref, idx_prev_ref, hx_ref, wg_ref, wu_ref, wd_ref, g2_ref, b2_ref,
              out_ref, xg_ref, acc_ref, hb_ref, ob_ref, gsem, ssem):
    i = pl.program_id(0)
    e = pl.program_id(1)
    n_tiles = pl.num_programs(0)
    n = tcnt_ref[i]
    slot = i % 2
    D = D_MODEL

    def row_gather(rows_ref, r, to_slot):
        return pltpu.make_async_copy(hx_ref.at[pl.ds(rows_ref[0, 0, r], 1)], xg_ref.at[to_slot, pl.ds(r, 1)],
                                     gsem.at[to_slot])

    def row_scatter(rows_ref, r, from_slot):
        return pltpu.make_async_copy(ob_ref.at[from_slot, pl.ds(r, 1)], out_ref.at[pl.ds(rows_ref[0, 0, r], 1)],
                                     ssem.at[from_slot])

    def for_rows(count, fn):
        lax.fori_loop(0, count, lambda r, c: (fn(r), c)[1], 0)

    @pl.when(jnp.logical_and(i == 0, e == 0))
    def _():
        for_rows(n, lambda r: row_gather(idx_ref, r, slot).start())

    has_next = i + 1 < n_tiles
    n_next = tcnt_ref[jnp.minimum(i + 1, n_tiles - 1)]
    spread = jnp.logical_and(has_next, n_next == TMG)
    slice_rows = TMG // EXPERTS_PER_GROUP

    @pl.when(jnp.logical_and(jnp.logical_and(e == 1, has_next), n_next < TMG))
    def _():
        for_rows(n_next, lambda r: row_gather(idx_next_ref, r, 1 - slot).start())

    @pl.when(jnp.logical_and(e == 0, n == TMG))
    def _():
        pltpu.make_async_copy(hx_ref.at[pl.ds(0, TMG)], xg_ref.at[slot], gsem.at[slot]).wait()

    @pl.when(jnp.logical_and(e == 0, n < TMG))
    def _():
        for_rows(n, lambda r: row_gather(idx_ref, r, slot).wait())

    @pl.when(jnp.logical_and(e == 0, n > 0))
    def _():
        acc_ref[...] = jnp.zeros_like(acc_ref)
        hb_ref[...] = xg_ref[slot, :, 0:D].astype(BF16)

    def expert_step():
        t = hb_ref[...]
        lane = lax.broadcasted_iota(I32, (1, LANES), 1)
        gate_lane = EXPERT_LANE0 + tgrp_ref[i] * EXPERTS_PER_GROUP + e
        gcol = jnp.sum(jnp.where(lane == gate_lane, xg_ref[slot, :, D:], 0.0), axis=1, keepdims=True)
        a = _mm(t, wg_ref[0])
        hid = a * _sigmoid(a) * _mm(t, wu_ref[0]) * gcol
        acc_ref[...] += _mm(hid.astype(BF16), wd_ref[0])

    n_prev = tcnt_ref[jnp.maximum(i - 1, 0)]
    drain_prev = jnp.logical_and(jnp.logical_and(i > 0, n_prev == TMG), n > 0)
    deferred = jnp.logical_and(jnp.logical_and(n == TMG, has_next), n_next > 0)

    def next_gather_slice():
        for j in range(slice_rows):
            row_gather(idx_next_ref, e * slice_rows + j, 1 - slot).start()

    def prev_scatter_slice():
        for j in range(slice_rows):
            row_scatter(idx_prev_ref, e * slice_rows + j, 1 - slot).start()

    for do_gather in (False, True):
        for do_scatter in (False, True):
            cond = jnp.logical_and(n > 0, jnp.logical_and(spread == do_gather, drain_prev == do_scatter))

            @pl.when(cond)
            def _(do_gather=do_gather, do_scatter=do_scatter):
                if do_gather:
                    next_gather_slice()
                if do_scatter:
                    prev_scatter_slice()
                expert_step()

    last = e == EXPERTS_PER_GROUP - 1

    @pl.when(jnp.logical_and(last, drain_prev))
    def _():
        pltpu.make_async_copy(ob_ref.at[1 - slot], out_ref.at[pl.ds(0, TMG)], ssem.at[1 - slot]).wait()

    @pl.when(jnp.logical_and(last, n > 0))
    def _():
        ob_ref[slot] = _layer_norm(DN_ALPHA * xg_ref[slot, :, 0:D] + acc_ref[...], g2_ref[...], b2_ref[...])

    @pl.when(jnp.logical_and(last, jnp.logical_and(n == TMG, jnp.logical_not(deferred))))
    def _():
        lax.fori_loop(0, TMG, lambda r, c: (row_scatter(idx_ref, r, slot).start(), c)[1], 0, unroll=8)
        pltpu.make_async_copy(ob_ref.at[slot], out_ref.at[pl.ds(0, TMG)], ssem.at[slot]).wait()

    @pl.when(jnp.logical_and(last, n < TMG))
    def _():
        for_rows(n, lambda r: row_scatter(idx_ref, r, slot).start())
        for_rows(n, lambda r: row_scatter(idx_ref, r, slot).wait())


def _moe(hx, tile_group, tile_count, tile_rows, wg, wu, wd, g2, b2):
    T, DX = hx.shape
    D = D_MODEL
    n_tiles = tile_rows.shape[0]
    wspec = lambda shape: pl.BlockSpec(
        (1,) + shape, lambda i, e, tg, tc: (tg[i] * EXPERTS_PER_GROUP + e, 0, 0))
    vec = pl.BlockSpec((1, D), lambda i, e, tg, tc: (0, 0))
    rows_of = lambda step: pl.BlockSpec(
        (1, 1, TMG), lambda i, e, tg, tc: (jnp.clip(i + step, 0, n_tiles - 1), 0, 0), memory_space=pltpu.SMEM)
    return pl.pallas_call(
        _moe_body,
        grid_spec=pltpu.PrefetchScalarGridSpec(
            num_scalar_prefetch=2,
            grid=(n_tiles, EXPERTS_PER_GROUP),
            in_specs=[rows_of(0), rows_of(1), rows_of(-1), pl.BlockSpec(memory_space=pl.ANY),
                      wspec((D, D_EXPERT)), wspec((D, D_EXPERT)), wspec((D_EXPERT, D)), vec, vec],
            out_specs=pl.BlockSpec(memory_space=pl.ANY),
            scratch_shapes=[pltpu.VMEM((2, TMG, DX), F32), pltpu.VMEM((TMG, D), F32), pltpu.VMEM((TMG, D), BF16),
                            pltpu.VMEM((2, TMG, D), F32), pltpu.SemaphoreType.DMA((2,)),
                            pltpu.SemaphoreType.DMA((2,))],
        ),
        out_shape=jax.ShapeDtypeStruct((T, D), F32),
        compiler_params=_params("arbitrary", "arbitrary"),
        name="moe",
    )(tile_group, tile_count, tile_rows, tile_rows, tile_rows, hx, wg, wu, wd, g2, b2)


def _group_tiles(grp, n_tiles):
    T = grp.shape[0]
    onehot = (grp[:, None] == jnp.arange(N_GROUPS)[None, :]).astype(I32)
    rank = jnp.cumsum(onehot, axis=0) - onehot
    count = jnp.sum(onehot, axis=0)
    tiles_per_group = (count + TMG - 1) // TMG
    first_tile = jnp.cumsum(tiles_per_group) - tiles_per_group
    pos = first_tile[grp] * TMG + jnp.sum(rank * onehot, axis=1)
    tile_rows = jnp.zeros((n_tiles * TMG,), I32).at[pos].set(jnp.arange(T, dtype=I32)).reshape(n_tiles, 1, TMG)
    tile = jnp.arange(n_tiles)
    tile_group = jnp.clip(jnp.sum((tile[:, None] >= first_tile[None, :]).astype(I32), axis=1) - 1, 0, N_GROUPS - 1)
    in_group = tile - first_tile[tile_group]
    tile_count = jnp.clip(count[tile_group] - in_group * TMG, 0, TMG)
    tile_count = jnp.where(in_group < tiles_per_group[tile_group], tile_count, 0)
    return tile_group.astype(I32), tile_count.astype(I32), tile_rows


def _rope_tables(Lp):
    pos = jnp.maximum(jnp.arange(Lp, dtype=I32) - OFF, 0).astype(F32)
    j = jnp.arange(LANES) % HEAD_DIM

    def table(half, rot_dim):
        inv = 1.0 / (ROPE_THETA ** (jnp.arange(half, dtype=F32) / half))
        ang = pos[:, None] * inv[None, :]
        cos, sin = jnp.cos(ang)[:, j % half], jnp.sin(ang)[:, j % half]
        rotated = (j < rot_dim)[None, :]
        sign = jnp.where(j < half, -1.0, 1.0)[None, :]
        return jnp.where(rotated, cos, 1.0), jnp.where(rotated, sin * sign, 0.0)

    cf, sf = table(HEAD_DIM // 2, HEAD_DIM)
    cp, sp = table(IDX_ROPE_DIM // 2, IDX_ROPE_DIM)
    return cf, sf, cp, sp


def _block_ones(n, block):
    idx = jnp.arange(n) // block
    return (idx[:, None] == idx[None, :]).astype(F32)


def kernel(x, meta_tokens, ln_emb_g, ln_emb_b, w_in, rw_mu, rw_w0, rw_w2, rw_a0, rw_a2, rw_g2, rw_kk, rw_ka,
           rw_rk, rw_lnx_g, rw_lnx_b, att_qnorm_g, att_wuq, idx_wq, idx_knorm_g, idx_knorm_b, w_out, ln1_g,
           ln1_b, rt_grp_w, rt_grp_b, rt_exp_w, rt_exp_b, ex_w_gate, ex_w_up, ex_w_down, ln2_g, ln2_b):
    B, S, D = x.shape
    assert w_in.shape[0] == 1 and D == D_MODEL and (B * S) % TMG == 0
    Lp = S + PAD_ROWS
    ksel = min(INDEX_TOPK, S // 4)
    row = lambda t: t.reshape(1, -1)
    W = RWKV_WIDTH

    meta_pad = jnp.zeros((PAD_ROWS, D), F32).at[OFF:].set(meta_tokens)
    w_in_p = jnp.pad(w_in[0], ((0, 0), (0, ATT_COLS_PAD - ATT_COLS))).astype(BF16)
    u_r, u_a = _ln_inproj(x, meta_pad, row(ln_emb_g), row(ln_emb_b), w_in_p)

    w2p = jnp.concatenate([rw_w2[0], jnp.zeros((ICLR_RANK, W), F32)], 0).astype(BF16)
    a2p = jnp.concatenate([jnp.zeros((DECAY_RANK, W), F32), rw_a2[0]], 0).astype(BF16)
    head_of_lane = jnp.arange(W) // HEAD_DIM
    hs = (head_of_lane[:, None] == jnp.arange(LANES)[None, :]).astype(BF16)
    tri = (jnp.arange(ROWS_R)[:, None] >= jnp.arange(ROWS_R)[None, :]).astype(F32)
    tri = (tri * _block_ones(ROWS_R, CHUNK)).astype(BF16)
    rp, kp, bp, ap, v, pc, g, bonus = _rwkv_prep(
        u_r, row(rw_mu[0]), row(rw_w0[0]), w2p, row(rw_a0[0]), a2p, rw_g2[0].astype(BF16), row(rw_kk[0]),
        row(rw_ka[0]), row(rw_rk[0]), hs, hs.T, tri)
    y_r = _rwkv_scan(rp, kp, bp, ap, v, pc, g, bonus, row(rw_lnx_g[0]), row(rw_lnx_b[0]))

    pad_lanes = lambda t: jnp.pad(t, (0, LANES - t.shape[0])).reshape(1, LANES)
    cf, sf, cp, sp = _rope_tables(Lp)
    qt, k, vt, qit, kx, wt = _dsa_prep(
        u_a, row(att_qnorm_g[0]), att_wuq[0].astype(BF16), idx_wq[0].astype(BF16),
        pad_lanes(idx_knorm_g[0]), pad_lanes(idx_knorm_b[0]), cf, sf, cp, sp)
    y_a = _dsa_attn(qt, k, vt, qit, kx, wt, ksel)

    wr = jnp.zeros((D, LANES), F32).at[:, :N_GROUPS].set(rt_grp_w[0])
    wr = wr.at[:, EXPERT_LANE0:EXPERT_LANE0 + N_EXPERTS].set(rt_exp_w[0])
    br = jnp.zeros((1, LANES), F32).at[0, :N_GROUPS].set(rt_grp_b[0])
    br = br.at[0, EXPERT_LANE0:EXPERT_LANE0 + N_EXPERTS].set(rt_exp_b[0])
    wrh = wr.astype(BF16)
    wrl = (wr - wrh.astype(F32)).astype(BF16)
    h1, grp = _outproj_router(x, y_r, y_a, row(ln_emb_g), row(ln_emb_b), w_out[0].astype(BF16),
                                row(ln1_g[0]), row(ln1_b[0]), wrh, wrl, br)
    T = B * S
    n_tiles = T // TMG + N_GROUPS
    tile_group, tile_count, tile_rows = _group_tiles(grp.reshape(T, LANES)[:, 0].astype(I32), n_tiles)
    out = _moe(h1.reshape(T, D + LANES), tile_group, tile_count, tile_rows, ex_w_gate[0].astype(BF16),
               ex_w_up[0].astype(BF16), ex_w_down[0].astype(BF16), row(ln2_g[0]), row(ln2_b[0]))
    return out.reshape(B, S, D)
```

```python
import functools

import jax
import jax.numpy as jnp
from jax import lax
from jax.experimental import pallas as pl
from jax.experimental.pallas import tpu as pltpu

F32 = jnp.float32
BF16 = jnp.bfloat16
I32 = jnp.int32
I16 = jnp.int16

D_MODEL = 1024
N_META = 16
RWKV_WIDTH = 512
ATT_WIDTH = 512
HEAD_DIM = 64
N_HEADS = 8
DECAY_RANK = 64
ICLR_RANK = 64
GATE_RANK = 128
Q_LORA_RANK = 256
IDX_HEADS = 8
IDX_DIM = 64
IDX_ROPE_DIM = 32
INDEX_TOPK = 256
ROPE_THETA = 10000.0
N_GROUPS = 4
EXPERTS_PER_GROUP = 8
N_EXPERTS = N_GROUPS * EXPERTS_PER_GROUP
D_EXPERT = 256
DN_ALPHA = 2.0 ** 0.25
LN_EPS = 1e-5
RMS_EPS = 1e-6
GN_EPS = 64e-5
RWKV_COLS = 3 * RWKV_WIDTH + DECAY_RANK + ICLR_RANK + GATE_RANK
ATT_COLS = Q_LORA_RANK + 2 * ATT_WIDTH + IDX_DIM + IDX_HEADS
ATT_COLS_PAD = 1408

LANES = 128
PAD_ROWS = 256
OFF = PAD_ROWS - N_META
CHUNK = 64
ROWS_A = 256
ROWS_R = 256
TQ = 256
KC = 256
TMG = 512
SCAN_BATCH = 4
EXPERT_LANE0 = 64
NEG = -1e30
MASKED = -3e38
BELOW_ALL = -1e38
ABOVE_ALL = 3e38
MAX_REFINE = 400
HALF16 = 32768
VMEM_LIMIT = 56 * 1024 * 1024


def _mm(a, b):
    return jnp.dot(a, b, preferred_element_type=F32)


def _sigmoid(x):
    return 1.0 / (1.0 + jnp.exp(-x))


def _layer_norm(x, g, b):
    mu = jnp.mean(x, -1, keepdims=True)
    xc = x - mu
    var = jnp.mean(xc * xc, -1, keepdims=True)
    return xc * lax.rsqrt(var + LN_EPS) * g + b


def _params(*sem):
    return pltpu.CompilerParams(dimension_semantics=sem, vmem_limit_bytes=VMEM_LIMIT)


def _const_spec(shape):
    nd = len(shape)
    return pl.BlockSpec(shape, lambda *_: (0,) * nd)


def _ln_inproj_body(x_ref, meta_ref, g_ref, b_ref, w_ref, ur_ref, ua_ref):
    blk = pl.program_id(1)
    xin = jnp.where(blk == 0, meta_ref[...], x_ref[0])
    h = _layer_norm(xin, g_ref[...], b_ref[...])
    row = lax.broadcasted_iota(I32, (ROWS_A, 1), 0)
    h = jnp.where((blk > 0) | (row >= OFF), h, 0.0)
    hb = h.astype(BF16)
    step = 256
    for n0 in range(0, RWKV_COLS, step):
        n1 = min(n0 + step, RWKV_COLS)
        ur_ref[0, :, n0:n1] = _mm(hb, w_ref[:, n0:n1])
    for n0 in range(0, ATT_COLS_PAD, step):
        n1 = min(n0 + step, ATT_COLS_PAD)
        ua_ref[0, :, n0:n1] = _mm(hb, w_ref[:, RWKV_COLS + n0:RWKV_COLS + n1])


def _ln_inproj(x, meta_pad, g, b, w):
    B, S, D = x.shape
    nblk = (S + PAD_ROWS) // ROWS_A
    Lp = S + PAD_ROWS
    ncols = RWKV_COLS + ATT_COLS_PAD
    return pl.pallas_call(
        _ln_inproj_body,
        grid=(B, nblk),
        in_specs=[
            pl.BlockSpec((1, ROWS_A, D), lambda b_, i: (b_, jnp.maximum(i - 1, 0), 0)),
            _const_spec((ROWS_A, D)),
            _const_spec((1, D)),
            _const_spec((1, D)),
            _const_spec((D, ncols)),
        ],
        out_specs=[
            pl.BlockSpec((1, ROWS_A, RWKV_COLS), lambda b_, i: (b_, i, 0)),
            pl.BlockSpec((1, ROWS_A, ATT_COLS_PAD), lambda b_, i: (b_, i, 0)),
        ],
        out_shape=[
            jax.ShapeDtypeStruct((B, Lp, RWKV_COLS), F32),
            jax.ShapeDtypeStruct((B, Lp, ATT_COLS_PAD), F32),
        ],
        compiler_params=_params("parallel", "arbitrary"),
        name="ln_inproj",
    )(x, meta_pad, g, b, w)


def _split3(x):
    hi = x.astype(BF16)
    r1 = x - hi.astype(F32)
    mid = r1.astype(BF16)
    return hi, mid, (r1 - mid.astype(F32)).astype(BF16)


def _mm_exact_rhs(x, m):
    return sum(_mm(p, m) for p in _split3(x))


def _mm_exact_lhs(m, x):
    return sum(_mm(m, p) for p in _split3(x))


def _rwkv_prep_body(u_ref, prev_ref, mu_ref, w0_ref, w2_ref, a0_ref, a2_ref, g2_ref, kk_ref, ka_ref, rk_ref,
                    hs_ref, hb_ref, tri_ref,
                    rp_ref, kp_ref, bp_ref, ap_ref, v_ref, pc_ref, g_ref, bonus_ref):
    blk = pl.program_id(1)
    u = u_ref[0]
    prev = jnp.where(blk == 0, 0.0, prev_ref[0][7:8, :])
    row = lax.broadcasted_iota(I32, (ROWS_R, 1), 0)
    shifted = jnp.where(row == 0, prev, pltpu.roll(u, 1, 0))
    ul = u + (shifted - u) * mu_ref[...]
    W = RWKV_WIDTH
    r = ul[:, 0:W]
    k = ul[:, W:2 * W]
    v = ul[:, 2 * W:3 * W]
    wa = ul[:, 3 * W:3 * W + 128]
    gd = ul[:, 3 * W + 128:3 * W + 256]
    w = w0_ref[...] + _mm(jnp.tanh(wa).astype(BF16), w2_ref[...])
    softplus_neg_w = jnp.maximum(-w, 0.0) + jnp.log(1.0 + jnp.exp(-jnp.abs(w)))
    logd = -jnp.exp(-softplus_neg_w - 0.5)
    a = _sigmoid(a0_ref[...] + _mm(wa.astype(BF16), a2_ref[...]))
    g_ref[0] = _mm(_sigmoid(gd).astype(BF16), g2_ref[...])
    head_sum = lambda t: _mm_exact_rhs(_mm_exact_rhs(t, hs_ref[...]), hb_ref[...])
    kkr = k * kk_ref[...]
    kk = kkr / jnp.maximum(jnp.sqrt(head_sum(kkr * kkr)), 1e-12)
    kmod = k * (1.0 + (a - 1.0) * ka_ref[...])
    bonus_ref[0] = head_sum(r * kmod * rk_ref[...]) * v
    v_ref[0] = v
    cum = _mm_exact_lhs(tri_ref[...], logd)
    rp_ref[0] = r * jnp.exp(cum)
    einv = jnp.exp(-cum)
    kp_ref[0] = kmod * einv
    bp_ref[0] = kk * a * einv
    ap_ref[0] = -kk * jnp.exp(cum - logd)
    for c in range(ROWS_R // CHUNK):
        last = c * CHUNK + CHUNK - 1
        pc_ref[0, c] = jnp.broadcast_to(jnp.exp(cum[last:last + 1]), (8, W))


def _rwkv_prep(u_r, mu, w0, w2p, a0, a2p, g2, k_k, k_a, r_k, hs, hb, tri):
    B, Lp, _ = u_r.shape
    nblk = Lp // ROWS_R
    W = RWKV_WIDTH
    row_spec = pl.BlockSpec((1, ROWS_R, W), lambda b_, i: (b_, i, 0))
    row_shape = jax.ShapeDtypeStruct((B, Lp, W), F32)
    cpb = ROWS_R // CHUNK
    return pl.pallas_call(
        _rwkv_prep_body,
        grid=(B, nblk),
        in_specs=[
            pl.BlockSpec((1, ROWS_R, RWKV_COLS), lambda b_, i: (b_, i, 0)),
            pl.BlockSpec((1, 8, RWKV_COLS), lambda b_, i: (b_, jnp.maximum(i * (ROWS_R // 8) - 1, 0), 0)),
            _const_spec((1, RWKV_COLS)),
            _const_spec((1, W)),
            _const_spec((128, W)),
            _const_spec((1, W)),
            _const_spec((128, W)),
            _const_spec((128, W)),
            _const_spec((1, W)),
            _const_spec((1, W)),
            _const_spec((1, W)),
            _const_spec((W, LANES)),
            _const_spec((LANES, W)),
            _const_spec((ROWS_R, ROWS_R)),
        ],
        out_specs=[row_spec] * 5 + [pl.BlockSpec((1, cpb, 8, W), lambda b_, i: (b_, i, 0, 0))] + [row_spec] * 2,
        out_shape=[row_shape] * 5 + [jax.ShapeDtypeStruct((B, Lp // CHUNK, 8, W), F32)] + [row_shape] * 2,
        compiler_params=_params("parallel", "arbitrary"),
        name="rwkv_prep",
    )(u_r, u_r, mu, w0, w2p, a0, a2p, g2, k_k, k_a, r_k, hs, hb, tri)


def _rwkv_scan_body(rp_ref, kp_ref, bp_ref, ap_ref, v_ref, pc_ref, g_ref, bonus_ref, lg_ref, lb_ref, o_ref, s_ref,
                    *, nb):
    c = pl.program_id(1)

    @pl.when(c == 0)
    def _():
        s_ref[...] = jnp.zeros_like(s_ref)

    @pl.when(c < OFF // CHUNK)
    def _():
        o_ref[...] = jnp.zeros_like(o_ref)

    @pl.when(c >= OFF // CHUNK)
    def _():
        _rwkv_chunk(rp_ref, kp_ref, bp_ref, ap_ref, v_ref, pc_ref, g_ref, bonus_ref, lg_ref, lb_ref, o_ref, s_ref, nb)


def _split(x):
    hi = x.astype(BF16)
    return hi, (x - hi.astype(F32)).astype(BF16)


def _dot3(a, b, dims):
    dg = lambda p, q: lax.dot_general(p, q, (dims, ((), ())), preferred_element_type=F32)
    return dg(a[0], b[0]) + dg(a[0], b[1]) + dg(a[1], b[0])


_NN = ((1,), (0,))
_NT = ((1,), (1,))
_TN = ((0,), (0,))


def _rwkv_chunk(rp_ref, kp_ref, bp_ref, ap_ref, v_ref, pc_ref, g_ref, bonus_ref, lg_ref, lb_ref, o_ref, s_ref, nb):
    C, N = CHUNK, HEAD_DIM
    ri = lax.broadcasted_iota(I32, (C, C), 0)
    ci = lax.broadcasted_iota(I32, (C, C), 1)
    strict = ri > ci
    incl = ri >= ci
    eye = jnp.where(ri == ci, 1.0, 0.0)
    units = [(b, slice(h * N, (h + 1) * N)) for b in range(nb) for h in range(N_HEADS)]
    ids = range(len(units))
    rows2 = lambda top, bottom: jnp.concatenate([top, bottom], axis=0)
    pcs = [pc_ref[b, 0, 0:1, sl] for b, sl in units]
    ar = [_split(rows2(ap_ref[b, :, sl], rp_ref[b, :, sl])) for b, sl in units]
    bk = [_split(rows2(bp_ref[b, :, sl], kp_ref[b, :, sl])) for b, sl in units]
    v_ = [_split(v_ref[b, :, sl]) for b, sl in units]
    s0 = [s_ref[i] for i in ids]
    s0s = [_split(s) for s in s0]
    gram = [_dot3(ar[i], bk[i], _NT) for i in ids]
    a_ab = [jnp.where(strict, gram[i][0:C, 0:C], 0.0) for i in ids]
    a_ak = [jnp.where(strict, gram[i][0:C, C:2 * C], 0.0) for i in ids]
    a_rb = [_split(jnp.where(incl, gram[i][C:2 * C, 0:C], 0.0)) for i in ids]
    a_rk = [jnp.where(incl, gram[i][C:2 * C, C:2 * C], 0.0) for i in ids]
    t = [eye + a_ab[i] for i in ids]
    pb = [a_ab[i].astype(BF16) for i in ids]
    for _ in range(C.bit_length() - 2):
        pb = [_mm(pb[i], pb[i]).astype(BF16) for i in ids]
        t = [t[i] + _mm(t[i].astype(BF16), pb[i]) for i in ids]
    ts = [_split(t[i]) for i in ids]
    resid = [(eye - t[i]) + _dot3(_split(a_ab[i]), ts[i], _NN) for i in ids]
    t = [t[i] + _mm(ts[i][0], resid[i].astype(BF16)) for i in ids]
    sp = [_dot3(ar[i], s0s[i], _NT) for i in ids]
    av = [_dot3(_split(rows2(a_ak[i], a_rk[i])), v_[i], _NN) for i in ids]
    u_ = [_dot3(_split(t[i]), _split(sp[i][0:C] + av[i][0:C]), _NN) for i in ids]
    for i, (b, sl) in enumerate(units):
        o = sp[i][C:2 * C] + av[i][C:2 * C] + _dot3(a_rb[i], _split(u_[i]), _NN)
        oc = o - jnp.mean(o, axis=1, keepdims=True)
        var = jnp.mean(oc * oc, axis=1, keepdims=True)
        y = oc * lax.rsqrt(var + GN_EPS) * lg_ref[:, sl] + lb_ref[:, sl]
        o_ref[b, :, sl] = (y + bonus_ref[b, :, sl]) * g_ref[b, :, sl]
    for i, (b, sl) in enumerate(units):
        vu = _split(rows2(v_ref[b, :, sl], u_[i]))
        kb = _split(rows2(kp_ref[b, :, sl], bp_ref[b, :, sl]) * pcs[i])
        s_ref[i] = s0[i] * pcs[i] + _dot3(vu, kb, _TN)


def _rwkv_scan(rp, kp, bp, ap, v, pc, g, bonus, lg, lb):
    B, Lp, W = rp.shape
    nch = Lp // CHUNK
    nb = SCAN_BATCH if B % SCAN_BATCH == 0 else 1
    row_spec = pl.BlockSpec((nb, CHUNK, W), lambda b_, c: (b_, c, 0))
    return pl.pallas_call(
        functools.partial(_rwkv_scan_body, nb=nb),
        grid=(B // nb, nch),
        in_specs=[row_spec] * 5 + [pl.BlockSpec((nb, 1, 8, W), lambda b_, c: (b_, c, 0, 0))] + [row_spec] * 2
        + [_const_spec((1, W))] * 2,
        out_specs=row_spec,
        out_shape=jax.ShapeDtypeStruct((B, Lp, W), F32),
        scratch_shapes=[pltpu.VMEM((nb * N_HEADS, HEAD_DIM, HEAD_DIM), F32)],
        compiler_params=_params("parallel", "arbitrary"),
        name="rwkv_scan",
    )(rp, kp, bp, ap, v, pc, g, bonus, lg, lb)


def _rope(x, cos, sin, half, first):
    width = x.shape[1]
    rot = jnp.where(first, pltpu.roll(x, width - half, 1), pltpu.roll(x, half, 1))
    return x * cos + rot * sin


def _dsa_prep_body(u_ref, qg_ref, wuq_ref, wiq_ref, kng_ref, knb_ref, cf_ref, sf_ref, cp_ref, sp_ref,
                   qt_ref, k_ref, vt_ref, qit_ref, kx_ref, wt_ref):
    u = u_ref[0]
    cq = u[:, 0:Q_LORA_RANK]
    k = u[:, Q_LORA_RANK:Q_LORA_RANK + ATT_WIDTH]
    v = u[:, Q_LORA_RANK + ATT_WIDTH:Q_LORA_RANK + 2 * ATT_WIDTH]
    tail = u[:, Q_LORA_RANK + 2 * ATT_WIDTH:]
    cqn = (cq * lax.rsqrt(jnp.mean(cq * cq, -1, keepdims=True) + RMS_EPS) * qg_ref[...]).astype(BF16)
    reps = ATT_WIDTH // LANES
    cf = jnp.concatenate([cf_ref[...]] * reps, axis=1)
    sf = jnp.concatenate([sf_ref[...]] * reps, axis=1)
    cp = jnp.concatenate([cp_ref[...]] * reps, axis=1)
    sp = jnp.concatenate([sp_ref[...]] * reps, axis=1)
    lane_w = lax.broadcasted_iota(I32, (1, ATT_WIDTH), 1) % HEAD_DIM
    first_f = lane_w < HEAD_DIM // 2
    first_p = lane_w < IDX_ROPE_DIM // 2
    q = _rope(_mm(cqn, wuq_ref[...]), cf, sf, HEAD_DIM // 2, first_f)
    qt_ref[0] = (q * (HEAD_DIM ** -0.5)).T.astype(BF16)
    qi = _rope(_mm(cqn, wiq_ref[...]), cp, sp, IDX_ROPE_DIM // 2, first_p)
    qit_ref[0] = qi.T.astype(BF16)
    k_ref[0] = _rope(k, cf, sf, HEAD_DIM // 2, first_f).astype(BF16)
    vt_ref[0] = v.T.astype(BF16)
    lane = lax.broadcasted_iota(I32, (1, LANES), 1)
    is_key = lane < IDX_DIM
    mu = jnp.sum(jnp.where(is_key, tail, 0.0), -1, keepdims=True) * (1.0 / IDX_DIM)
    tc = jnp.where(is_key, tail - mu, 0.0)
    var = jnp.sum(tc * tc, -1, keepdims=True) * (1.0 / IDX_DIM)
    kn = tc * lax.rsqrt(var + LN_EPS) * kng_ref[...] + knb_ref[...]
    kn = _rope(kn, cp_ref[...], sp_ref[...], IDX_ROPE_DIM // 2, (lane % HEAD_DIM) < IDX_ROPE_DIM // 2)
    kn = kn * (IDX_DIM ** -0.5)
    kx_ref[0] = jnp.where(is_key, kn, pltpu.roll(kn, IDX_DIM, 1)).astype(BF16)
    wt_ref[0] = (tail * (IDX_HEADS ** -0.5)).T


def _dsa_prep(u_a, qg, wuq, wiq, kng, knb, cf, sf, cp, sp):
    B, Lp, _ = u_a.shape
    W = ATT_WIDTH
    row = lambda width: pl.BlockSpec((1, ROWS_A, width), lambda b_, i: (b_, i, 0))
    col = lambda height: pl.BlockSpec((1, height, ROWS_A), lambda b_, i: (b_, 0, i))
    tab = pl.BlockSpec((ROWS_A, LANES), lambda b_, i: (i, 0))
    return pl.pallas_call(
        _dsa_prep_body,
        grid=(B, Lp // ROWS_A),
        in_specs=[row(ATT_COLS_PAD), _const_spec((1, Q_LORA_RANK)), _const_spec((Q_LORA_RANK, W)),
                  _const_spec((Q_LORA_RANK, W)), _const_spec((1, LANES)), _const_spec((1, LANES)),
                  tab, tab, tab, tab],
        out_specs=[col(W), row(W), col(W), col(W), row(LANES), col(LANES)],
        out_shape=[jax.ShapeDtypeStruct((B, W, Lp), BF16), jax.ShapeDtypeStruct((B, Lp, W), BF16),
                   jax.ShapeDtypeStruct((B, W, Lp), BF16), jax.ShapeDtypeStruct((B, W, Lp), BF16),
                   jax.ShapeDtypeStruct((B, Lp, LANES), BF16), jax.ShapeDtypeStruct((B, LANES, Lp), F32)],
        compiler_params=_params("parallel", "arbitrary"),
        name="dsa_prep",
    )(u_a, qg, wuq, wiq, kng, knb, cf, sf, cp, sp)


def _fold_rows(x, op=jnp.add):
    parts = [x[8 * r:8 * r + 8] for r in range(x.shape[0] // 8)]
    while len(parts) > 1:
        parts = [op(a, b) for a, b in zip(parts[0::2], parts[1::2])] + parts[len(parts) & ~1:]
    return parts[0]


def _dsa_attn_body(qt_ref, qit_ref, wt_ref, k_ref, vt_ref, kx_ref, o_ref, khi_ref, klo_ref, sc_ref, acc_ref,
                   j_ref, s_ref, pe_ref, *, ksel):
    i = pl.program_id(1)
    nkc = (i * TQ + TQ - 1) // KC + 1
    kf = float(ksel)
    tcol = i * TQ + lax.broadcasted_iota(I32, (1, TQ), 1)
    row_in_pair = lax.broadcasted_iota(I32, (LANES, 1), 0)

    def head_operands(ref):
        out = []
        for h in range(N_HEADS):
            pair = ref[0, LANES * (h // 2):LANES * (h // 2 + 1), :]
            keep = (row_in_pair < HEAD_DIM) if h % 2 == 0 else (row_in_pair >= HEAD_DIM)
            out.append(jnp.where(keep, pair, jnp.zeros_like(pair)))
        return out

    def key_rows(ks):
        return ks + lax.broadcasted_iota(I32, (KC, 1), 0)

    qis = head_operands(qit_ref)
    wrows = [wt_ref[0, IDX_DIM + h:IDX_DIM + h + 1, :] for h in range(IDX_HEADS)]

    def score_chunk(kc, carry):
        lo8, hi8 = carry
        ks = pl.multiple_of(kc * KC, KC)
        kx = kx_ref[0, pl.ds(ks, KC), :]
        sc = jnp.zeros((KC, TQ), F32)
        for h in range(IDX_HEADS):
            sc = sc + jnp.maximum(_mm(kx, qis[h]), 0.0) * wrows[h]
        sc = sc + 0.0
        krow = key_rows(ks)
        sc = jnp.where(krow >= OFF, sc, MASKED)
        sc = jnp.where(krow <= tcol, sc, MASKED)
        sc_ref[pl.ds(ks, KC), :] = sc
        bits = lax.bitcast_convert_type(sc, I32)
        key = jnp.where(bits >= 0, bits, bits ^ jnp.int32(0x7FFFFFFF))
        khi_ref[pl.ds(ks, KC), :] = lax.shift_right_arithmetic(key, 16).astype(I16)
        klo_ref[pl.ds(ks, KC), :] = ((key & jnp.int32(0xFFFF)) - HALF16).astype(I16)
        lo8 = jnp.minimum(lo8, _fold_rows(jnp.where(sc <= MASKED, ABOVE_ALL, sc), jnp.minimum))
        hi8 = jnp.maximum(hi8, _fold_rows(sc, jnp.maximum))
        return lo8, hi8

    lo8, hi8 = lax.fori_loop(0, nkc, score_chunk,
                             (jnp.full((8, TQ), ABOVE_ALL, F32), jnp.full((8, TQ), MASKED, F32)))
    smin = jnp.min(lo8, axis=0, keepdims=True)
    smax = jnp.max(hi8, axis=0, keepdims=True)

    def scan_chunks(fn, init):
        def body(kc, carry):
            ks = pl.multiple_of(kc * KC, KC)
            return fn(carry, ks)
        return lax.fori_loop(0, nkc, body, init)

    zeros8 = jnp.zeros((8, TQ), F32)

    def count_where(ref, pred):
        cnt = scan_chunks(lambda c, ks: c + _fold_rows(pred(ref[pl.ds(ks, KC), :], ks)), zeros8)
        return jnp.sum(cnt, axis=0, keepdims=True)

    def count16(ref, pred):
        def fn(cnt, ks):
            m = pred(ref[pl.ds(ks, KC), :])
            parts = [m[16 * r:16 * r + 16] for r in range(KC // 16)]
            while len(parts) > 1:
                parts = [a + b for a, b in zip(parts[0::2], parts[1::2])]
            return cnt + parts[0]
        cnt = scan_chunks(fn, jnp.zeros((16, TQ), I16))
        return jnp.sum(cnt.astype(I32), axis=0, keepdims=True)

    one16, zero16 = jnp.int16(1), jnp.int16(0)

    def radix16(ref, target):
        def bit(bi, prefix):
            cand = prefix | lax.shift_left(jnp.int32(1), 15 - bi)
            cand16 = (cand - HALF16).astype(I16)
            cnt = count16(ref, lambda x: jnp.where(x >= cand16, one16, zero16))
            return jnp.where(cnt >= target, cand, prefix)
        return lax.fori_loop(0, 16, bit, jnp.zeros((1, TQ), I32))

    k_int = jnp.full((1, TQ), ksel, I32)
    thr_hi = radix16(khi_ref, k_int) - HALF16
    thr_hi16 = thr_hi.astype(I16)
    above = count16(khi_ref, lambda x: jnp.where(x > thr_hi16, one16, zero16))

    def keep_low_of_ties(carry, ks):
        rows = pl.ds(ks, KC)
        klo_ref[rows, :] = jnp.where(khi_ref[rows, :] == thr_hi16, klo_ref[rows, :], jnp.int16(-HALF16))
        return carry

    scan_chunks(keep_low_of_ties, 0)
    thr_lo = radix16(klo_ref, k_int - above)
    thr_key = lax.shift_left(thr_hi, 16) | thr_lo
    cand0 = lax.bitcast_convert_type(jnp.where(thr_key >= 0, thr_key, thr_key ^ jnp.int32(0x7FFFFFFF)), F32)

    n_adm = jnp.maximum(tcol - (OFF - 1), 0).astype(F32)
    searching = n_adm > kf

    def probe(mid):
        def fn(carry, ks):
            cnt, vmin = carry
            s = sc_ref[pl.ds(ks, KC), :]
            ge = s >= mid
            return (cnt + _fold_rows(jnp.where(ge, 1.0, 0.0)),
                    jnp.minimum(vmin, _fold_rows(jnp.where(ge, s, ABOVE_ALL), jnp.minimum)))
        cnt, vmin = scan_chunks(fn, (zeros8, jnp.full((8, TQ), ABOVE_ALL, F32)))
        return jnp.sum(cnt, axis=0, keepdims=True), jnp.min(vmin, axis=0, keepdims=True)

    def refine(state):
        it, lo, hi, c_lo, c_gt, done, _ = state
        mid = jnp.where(it == 0, jnp.where(searching, cand0, lo), lo + 0.5 * (hi - lo))
        c_mid, v_mid = probe(mid)
        up = c_mid >= kf
        lo_n = jnp.where(up, v_mid, lo)
        hi_n = jnp.where(up, hi, mid)
        c_lo_n = jnp.where(up, c_mid, c_lo)
        c_gt_n = count_where(sc_ref, lambda s, ks: jnp.where(s > lo_n, 1.0, 0.0))
        stalled = jnp.where(it > 0, jnp.where(mid <= lo, 1.0, jnp.where(mid >= hi, 1.0, 0.0)), 0.0)
        fin = jnp.maximum(jnp.where(c_gt_n < kf, 1.0, 0.0), stalled)
        frozen = done > 0.0
        keep = lambda old, new_: jnp.where(frozen, old, new_)
        done_n = jnp.maximum(done, fin)
        return (it + 1, keep(lo, lo_n), keep(hi, hi_n), keep(c_lo, c_lo_n), keep(c_gt, c_gt_n), done_n,
                jnp.max(1.0 - done_n))

    done0 = jnp.where(searching, 0.0, 1.0)
    state0 = (jnp.int32(0), smin, smax + (jnp.abs(smax) + 1.0) * 1e-6, n_adm, n_adm, done0, jnp.max(1.0 - done0))
    state = lax.while_loop(lambda st: jnp.logical_and(st[6] > 0.0, st[0] < MAX_REFINE), refine, state0)
    thr = jnp.where(searching, state[1], BELOW_ALL)
    cnt_gt = jnp.where(searching, state[4], n_adm)
    cnt_eq = jnp.where(searching, state[3] - state[4], 0.0)
    need = kf - cnt_gt

    j_ref[...] = jnp.full(j_ref.shape, 2 ** 30, I32)

    @pl.when(jnp.max(cnt_eq - need) > 0.0)
    def _():
        def index_bit(bi, prefix):
            cand = prefix | lax.shift_left(jnp.int32(1), 12 - bi)
            before = count_where(
                sc_ref, lambda s, ks: jnp.where(s == thr, jnp.where(key_rows(ks) < cand, 1.0, 0.0), 0.0))
            return jnp.where(before < need, cand, prefix)
        jst = lax.fori_loop(0, 13, index_bit, jnp.zeros((1, TQ), I32))
        j_ref[...] = jnp.broadcast_to(jst, j_ref.shape)

    jstar = j_ref[0:1, :]

    qs = head_operands(qt_ref)
    acc_ref[...] = jnp.zeros_like(acc_ref)

    ones_rows = jnp.ones((16, KC), BF16)

    def attend_chunk(kc, carry):
        ms, ls = carry
        ks = pl.multiple_of(kc * KC, KC)
        sc = sc_ref[pl.ds(ks, KC), :]
        tie = jnp.where(sc == thr, jnp.where(key_rows(ks) <= jstar, 0.0, NEG), NEG)
        bias = jnp.where(sc > thr, 0.0, tie)
        chunk_max = []
        for h in range(N_HEADS):
            p = h // 2
            kp = k_ref[0, pl.ds(ks, KC), LANES * p:LANES * (p + 1)]
            s = _mm(kp, qs[h]) + bias
            s_ref[h] = s
            chunk_max.append(jnp.max(s, axis=0, keepdims=True))
        new_ms, new_ls, alphas = [], [], []
        for h in range(N_HEADS):
            m_new = jnp.maximum(ms[h], chunk_max[h])
            alphas.append(jnp.exp(ms[h] - m_new))
            new_ms.append(m_new)
            pe_ref[h] = jnp.exp(s_ref[h] - m_new).astype(BF16)
        for h in range(N_HEADS):
            vt = vt_ref[0, HEAD_DIM * h:HEAD_DIM * (h + 1), pl.ds(ks, KC)]
            pv = _mm(jnp.concatenate([vt, ones_rows], axis=0), pe_ref[h])
            rows = slice(HEAD_DIM * h, HEAD_DIM * (h + 1))
            acc_ref[rows, :] = acc_ref[rows, :] * alphas[h] + pv[0:HEAD_DIM]
            new_ls.append(alphas[h] * ls[h] + pv[HEAD_DIM:HEAD_DIM + 1])
        return tuple(new_ms), tuple(new_ls)

    init = (tuple(jnp.full((1, TQ), NEG, F32) for _ in range(N_HEADS)),
            tuple(jnp.zeros((1, TQ), F32) for _ in range(N_HEADS)))
    _, ls = lax.fori_loop(0, nkc, attend_chunk, init)
    for p in range(N_HEADS // 2):
        parts = [acc_ref[HEAD_DIM * h:HEAD_DIM * (h + 1), :] / ls[h] for h in (2 * p, 2 * p + 1)]
        o_ref[0, :, LANES * p:LANES * (p + 1)] = jnp.concatenate(parts, axis=0).T


def _dsa_attn(qt, k, vt, qit, kx, wt, ksel):
    B, Lp, W = k.shape
    assert Lp % KC == 0 and Lp <= 8192
    qcol = lambda height: pl.BlockSpec((1, height, TQ), lambda b_, i: (b_, 0, i))
    full = lambda shape: pl.BlockSpec((1,) + shape, lambda b_, i: (b_, 0, 0))
    return pl.pallas_call(
        functools.partial(_dsa_attn_body, ksel=ksel),
        grid=(B, Lp // TQ),
        in_specs=[qcol(W), qcol(W), qcol(LANES), full((Lp, W)), full((W, Lp)), full((Lp, LANES))],
        out_specs=pl.BlockSpec((1, TQ, W), lambda b_, i: (b_, i, 0)),
        out_shape=jax.ShapeDtypeStruct((B, Lp, W), F32),
        scratch_shapes=[
            pltpu.VMEM((Lp, TQ), I16),
            pltpu.VMEM((Lp, TQ), I16),
            pltpu.VMEM((Lp, TQ), F32),
            pltpu.VMEM((W, TQ), F32),
            pltpu.VMEM((8, TQ), I32),
            pltpu.VMEM((N_HEADS, KC, TQ), F32),
            pltpu.VMEM((N_HEADS, KC, TQ), BF16),
        ],
        compiler_params=_params("parallel", "arbitrary"),
        name="dsa_attn",
    )(qt, qit, wt, k, vt, kx)


def _outproj_router_body(x_ref, yr_ref, ya_ref, eg_ref, eb_ref, wo_ref, g1_ref, b1_ref, wrh_ref, wrl_ref, br_ref,
                         h_ref, grp_ref):
    h0 = _layer_norm(x_ref[0], eg_ref[...], eb_ref[...])
    mix = (_mm(yr_ref[0].astype(BF16), wo_ref[0:RWKV_WIDTH, :])
           + _mm(ya_ref[0].astype(BF16), wo_ref[RWKV_WIDTH:, :]))
    h1 = _layer_norm(DN_ALPHA * h0 + mix, g1_ref[...], b1_ref[...])
    h_ref[0, :, 0:D_MODEL] = h1
    logits = _dot3(_split(h1), (wrh_ref[...], wrl_ref[...]), _NN) + br_ref[...]
    lane = lax.broadcasted_iota(I32, (1, LANES), 1)
    lanef = lane.astype(F32)
    low = -3e38
    lgm = jnp.where(lane < N_GROUPS, logits, low)
    gmax = jnp.max(lgm, axis=1, keepdims=True)
    gsel = jnp.min(jnp.where(lgm == gmax, lanef, 1e9), axis=1, keepdims=True)
    gsum = jnp.sum(jnp.where(lane < N_GROUPS, jnp.exp(lgm - gmax), 0.0), axis=1, keepdims=True)
    group_of_lane = ((lane - EXPERT_LANE0) // EXPERTS_PER_GROUP).astype(F32)
    lem = jnp.where(group_of_lane == gsel, logits, low)
    m1 = jnp.max(lem, axis=1, keepdims=True)
    i1 = jnp.min(jnp.where(lem == m1, lanef, 1e9), axis=1, keepdims=True)
    lem2 = jnp.where(lanef == i1, low, lem)
    m2 = jnp.max(lem2, axis=1, keepdims=True)
    i2 = jnp.min(jnp.where(lem2 == m2, lanef, 1e9), axis=1, keepdims=True)
    e2 = jnp.exp(m2 - m1)
    w1 = 1.0 / (1.0 + e2)
    w2 = e2 / (1.0 + e2)
    gates = jnp.where(lanef == i1, w1, jnp.where(lanef == i2, w2, 0.0)) / gsum
    h_ref[0, :, D_MODEL:] = gates
    grp_ref[0] = jnp.broadcast_to(gsel, (ROWS_A, LANES))


def _outproj_router(x, y_r, y_a, eg, eb, wo, g1, b1, wrh, wrl, br):
    B, S, D = x.shape
    skip = PAD_ROWS // ROWS_A
    xrow = pl.BlockSpec((1, ROWS_A, D), lambda b_, i: (b_, i, 0))
    yrow = pl.BlockSpec((1, ROWS_A, RWKV_WIDTH), lambda b_, i: (b_, i + skip, 0))
    vec = _const_spec((1, D))
    return pl.pallas_call(
        _outproj_router_body,
        grid=(B, S // ROWS_A),
        in_specs=[xrow, yrow, yrow, vec, vec, _const_spec((D, D)), vec, vec,
                  _const_spec((D, LANES)), _const_spec((D, LANES)), _const_spec((1, LANES))],
        out_specs=[pl.BlockSpec((1, ROWS_A, D + LANES), lambda b_, i: (b_, i, 0)),
                   pl.BlockSpec((1, ROWS_A, LANES), lambda b_, i: (b_, i, 0))],
        out_shape=[jax.ShapeDtypeStruct((B, S, D + LANES), F32), jax.ShapeDtypeStruct((B, S, LANES), F32)],
        compiler_params=_params("parallel", "arbitrary"),
        name="outproj_router",
    )(x, y_r, y_a, eg, eb, wo, g1, b1, wrh, wrl, br)


def _moe_body(tgrp_ref, tcnt_ref, idx_ref, idx_next_ref, idx_prev_ref, hx_ref, wg_ref, wu_ref, wd_ref, g2_ref, b2_ref,
              out_ref, xg_ref, acc_ref, hb_ref, ob_ref, gsem, ssem):
    i = pl.program_id(0)
    e = pl.program_id(1)
    n_tiles = pl.num_programs(0)
    n = tcnt_ref[i]
    slot = i % 2
    D = D_MODEL

    def row_gather(rows_ref, r, to_slot):
        return pltpu.make_async_copy(hx_ref.at[pl.ds(rows_ref[0, 0, r], 1)], xg_ref.at[to_slot, pl.ds(r, 1)],
                                     gsem.at[to_slot])

    def row_scatter(rows_ref, r, from_slot):
        return pltpu.make_async_copy(ob_ref.at[from_slot, pl.ds(r, 1)], out_ref.at[pl.ds(rows_ref[0, 0, r], 1)],
                                     ssem.at[from_slot])

    def for_rows(count, fn):
        lax.fori_loop(0, count, lambda r, c: (fn(r), c)[1], 0)

    @pl.when(jnp.logical_and(i == 0, e == 0))
    def _():
        for_rows(n, lambda r: row_gather(idx_ref, r, slot).start())

    has_next = i + 1 < n_tiles
    n_next = tcnt_ref[jnp.minimum(i + 1, n_tiles - 1)]
    spread = jnp.logical_and(has_next, n_next == TMG)
    slice_rows = TMG // EXPERTS_PER_GROUP

    @pl.when(jnp.logical_and(jnp.logical_and(e == 1, has_next), n_next < TMG))
    def _():
        for_rows(n_next, lambda r: row_gather(idx_next_ref, r, 1 - slot).start())

    @pl.when(jnp.logical_and(e == 0, n == TMG))
    def _():
        pltpu.make_async_copy(hx_ref.at[pl.ds(0, TMG)], xg_ref.at[slot], gsem.at[slot]).wait()

    @pl.when(jnp.logical_and(e == 0, n < TMG))
    def _():
        for_rows(n, lambda r: row_gather(idx_ref, r, slot).wait())

    @pl.when(jnp.logical_and(e == 0, n > 0))
    def _():
        acc_ref[...] = jnp.zeros_like(acc_ref)
        hb_ref[...] = xg_ref[slot, :, 0:D].astype(BF16)

    def expert_step():
        t = hb_ref[...]
        lane = lax.broadcasted_iota(I32, (1, LANES), 1)
        gate_lane = EXPERT_LANE0 + tgrp_ref[i] * EXPERTS_PER_GROUP + e
        gcol = jnp.sum(jnp.where(lane == gate_lane, xg_ref[slot, :, D:], 0.0), axis=1, keepdims=True)
        a = _mm(t, wg_ref[0])
        hid = a * _sigmoid(a) * _mm(t, wu_ref[0]) * gcol
        acc_ref[...] += _mm(hid.astype(BF16), wd_ref[0])

    n_prev = tcnt_ref[jnp.maximum(i - 1, 0)]
    drain_prev = jnp.logical_and(jnp.logical_and(i > 0, n_prev == TMG), n > 0)
    deferred = jnp.logical_and(jnp.logical_and(n == TMG, has_next), n_next > 0)

    def next_gather_slice():
        for j in range(slice_rows):
            row_gather(idx_next_ref, e * slice_rows + j, 1 - slot).start()

    def prev_scatter_slice():
        for j in range(slice_rows):
            row_scatter(idx_prev_ref, e * slice_rows + j, 1 - slot).start()

    for do_gather in (False, True):
        for do_scatter in (False, True):
            cond = jnp.logical_and(n > 0, jnp.logical_and(spread == do_gather, drain_prev == do_scatter))

            @pl.when(cond)
            def _(do_gather=do_gather, do_scatter=do_scatter):
                if do_gather:
                    next_gather_slice()
                if do_scatter:
                    prev_scatter_slice()
                expert_step()

    last = e == EXPERTS_PER_GROUP - 1

    @pl.when(jnp.logical_and(last, drain_prev))
    def _():
        pltpu.make_async_copy(ob_ref.at[1 - slot], out_ref.at[pl.ds(0, TMG)], ssem.at[1 - slot]).wait()

    @pl.when(jnp.logical_and(last, n > 0))
    def _():
        ob_ref[slot] = _layer_norm(DN_ALPHA * xg_ref[slot, :, 0:D] + acc_ref[...], g2_ref[...], b2_ref[...])

    @pl.when(jnp.logical_and(last, jnp.logical_and(n == TMG, jnp.logical_not(deferred))))
    def _():
        lax.fori_loop(0, TMG, lambda r, c: (row_scatter(idx_ref, r, slot).start(), c)[1], 0, unroll=8)
        pltpu.make_async_copy(ob_ref.at[slot], out_ref.at[pl.ds(0, TMG)], ssem.at[slot]).wait()

    @pl.when(jnp.logical_and(last, n < TMG))
    def _():
        for_rows(n, lambda r: row_scatter(idx_ref, r, slot).start())
        for_rows(n, lambda r: row_scatter(idx_ref, r, slot).wait())


def _moe(hx, tile_group, tile_count, tile_rows, wg, wu, wd, g2, b2):
    T, DX = hx.shape
    D = D_MODEL
    n_tiles = tile_rows.shape[0]
    wspec = lambda shape: pl.BlockSpec(
        (1,) + shape, lambda i, e, tg, tc: (tg[i] * EXPERTS_PER_GROUP + e, 0, 0))
    vec = pl.BlockSpec((1, D), lambda i, e, tg, tc: (0, 0))
    rows_of = lambda step: pl.BlockSpec(
        (1, 1, TMG), lambda i, e, tg, tc: (jnp.clip(i + step, 0, n_tiles - 1), 0, 0), memory_space=pltpu.SMEM)
    return pl.pallas_call(
        _moe_body,
        grid_spec=pltpu.PrefetchScalarGridSpec(
            num_scalar_prefetch=2,
            grid=(n_tiles, EXPERTS_PER_GROUP),
            in_specs=[rows_of(0), rows_of(1), rows_of(-1), pl.BlockSpec(memory_space=pl.ANY),
                      wspec((D, D_EXPERT)), wspec((D, D_EXPERT)), wspec((D_EXPERT, D)), vec, vec],
            out_specs=pl.BlockSpec(memory_space=pl.ANY),
            scratch_shapes=[pltpu.VMEM((2, TMG, DX), F32), pltpu.VMEM((TMG, D), F32), pltpu.VMEM((TMG, D), BF16),
                            pltpu.VMEM((2, TMG, D), F32), pltpu.SemaphoreType.DMA((2,)),
                            pltpu.SemaphoreType.DMA((2,))],
        ),
        out_shape=jax.ShapeDtypeStruct((T, D), F32),
        compiler_params=_params("arbitrary", "arbitrary"),
        name="moe",
    )(tile_group, tile_count, tile_rows, tile_rows, tile_rows, hx, wg, wu, wd, g2, b2)


def _group_tiles(grp, n_tiles):
    T = grp.shape[0]
    onehot = (grp[:, None] == jnp.arange(N_GROUPS)[None, :]).astype(I32)
    rank = jnp.cumsum(onehot, axis=0) - onehot
    count = jnp.sum(onehot, axis=0)
    tiles_per_group = (count + TMG - 1) // TMG
    first_tile = jnp.cumsum(tiles_per_group) - tiles_per_group
    pos = first_tile[grp] * TMG + jnp.sum(rank * onehot, axis=1)
    tile_rows = jnp.zeros((n_tiles * TMG,), I32).at[pos].set(jnp.arange(T, dtype=I32)).reshape(n_tiles, 1, TMG)
    tile = jnp.arange(n_tiles)
    tile_group = jnp.clip(jnp.sum((tile[:, None] >= first_tile[None, :]).astype(I32), axis=1) - 1, 0, N_GROUPS - 1)
    in_group = tile - first_tile[tile_group]
    tile_count = jnp.clip(count[tile_group] - in_group * TMG, 0, TMG)
    tile_count = jnp.where(in_group < tiles_per_group[tile_group], tile_count, 0)
    return tile_group.astype(I32), tile_count.astype(I32), tile_rows


def _rope_tables(Lp):
    pos = jnp.maximum(jnp.arange(Lp, dtype=I32) - OFF, 0).astype(F32)
    j = jnp.arange(LANES) % HEAD_DIM

    def table(half, rot_dim):
        inv = 1.0 / (ROPE_THETA ** (jnp.arange(half, dtype=F32) / half))
        ang = pos[:, None] * inv[None, :]
        cos, sin = jnp.cos(ang)[:, j % half], jnp.sin(ang)[:, j % half]
        rotated = (j < rot_dim)[None, :]
        sign = jnp.where(j < half, -1.0, 1.0)[None, :]
        return jnp.where(rotated, cos, 1.0), jnp.where(rotated, sin * sign, 0.0)

    cf, sf = table(HEAD_DIM // 2, HEAD_DIM)
    cp, sp = table(IDX_ROPE_DIM // 2, IDX_ROPE_DIM)
    return cf, sf, cp, sp


def _block_ones(n, block):
    idx = jnp.arange(n) // block
    return (idx[:, None] == idx[None, :]).astype(F32)


def kernel(x, meta_tokens, ln_emb_g, ln_emb_b, w_in, rw_mu, rw_w0, rw_w2, rw_a0, rw_a2, rw_g2, rw_kk, rw_ka,
           rw_rk, rw_lnx_g, rw_lnx_b, att_qnorm_g, att_wuq, idx_wq, idx_knorm_g, idx_knorm_b, w_out, ln1_g,
           ln1_b, rt_grp_w, rt_grp_b, rt_exp_w, rt_exp_b, ex_w_gate, ex_w_up, ex_w_down, ln2_g, ln2_b):
    B, S, D = x.shape
    assert w_in.shape[0] == 1 and D == D_MODEL and (B * S) % TMG == 0
    Lp = S + PAD_ROWS
    ksel = min(INDEX_TOPK, S // 4)
    row = lambda t: t.reshape(1, -1)
    W = RWKV_WIDTH

    meta_pad = jnp.zeros((PAD_ROWS, D), F32).at[OFF:].set(meta_tokens)
    w_in_p = jnp.pad(w_in[0], ((0, 0), (0, ATT_COLS_PAD - ATT_COLS))).astype(BF16)
    u_r, u_a = _ln_inproj(x, meta_pad, row(ln_emb_g), row(ln_emb_b), w_in_p)

    w2p = jnp.concatenate([rw_w2[0], jnp.zeros((ICLR_RANK, W), F32)], 0).astype(BF16)
    a2p = jnp.concatenate([jnp.zeros((DECAY_RANK, W), F32), rw_a2[0]], 0).astype(BF16)
    head_of_lane = jnp.arange(W) // HEAD_DIM
    hs = (head_of_lane[:, None] == jnp.arange(LANES)[None, :]).astype(BF16)
    tri = (jnp.arange(ROWS_R)[:, None] >= jnp.arange(ROWS_R)[None, :]).astype(F32)
    tri = (tri * _block_ones(ROWS_R, CHUNK)).astype(BF16)
    rp, kp, bp, ap, v, pc, g, bonus = _rwkv_prep(
        u_r, row(rw_mu[0]), row(rw_w0[0]), w2p, row(rw_a0[0]), a2p, rw_g2[0].astype(BF16), row(rw_kk[0]),
        row(rw_ka[0]), row(rw_rk[0]), hs, hs.T, tri)
    y_r = _rwkv_scan(rp, kp, bp, ap, v, pc, g, bonus, row(rw_lnx_g[0]), row(rw_lnx_b[0]))

    pad_lanes = lambda t: jnp.pad(t, (0, LANES - t.shape[0])).reshape(1, LANES)
    cf, sf, cp, sp = _rope_tables(Lp)
    qt, k, vt, qit, kx, wt = _dsa_prep(
        u_a, row(att_qnorm_g[0]), att_wuq[0].astype(BF16), idx_wq[0].astype(BF16),
        pad_lanes(idx_knorm_g[0]), pad_lanes(idx_knorm_b[0]), cf, sf, cp, sp)
    y_a = _dsa_attn(qt, k, vt, qit, kx, wt, ksel)

    wr = jnp.zeros((D, LANES), F32).at[:, :N_GROUPS].set(rt_grp_w[0])
    wr = wr.at[:, EXPERT_LANE0:EXPERT_LANE0 + N_EXPERTS].set(rt_exp_w[0])
    br = jnp.zeros((1, LANES), F32).at[0, :N_GROUPS].set(rt_grp_b[0])
    br = br.at[0, EXPERT_LANE0:EXPERT_LANE0 + N_EXPERTS].set(rt_exp_b[0])
    wrh = wr.astype(BF16)
    wrl = (wr - wrh.astype(F32)).astype(BF16)
    h1, grp = _outproj_router(x, y_r, y_a, row(ln_emb_g), row(ln_emb_b), w_out[0].astype(BF16),
                                row(ln1_g[0]), row(ln1_b[0]), wrh, wrl, br)
    T = B * S
    n_tiles = T // TMG + N_GROUPS
    tile_group, tile_count, tile_rows = _group_tiles(grp.reshape(T, LANES)[:, 0].astype(I32), n_tiles)
    out = _moe(h1.reshape(T, D + LANES), tile_group, tile_count, tile_rows, ex_w_gate[0].astype(BF16),
               ex_w_up[0].astype(BF16), ex_w_down[0].astype(BF16), row(ln2_g[0]), row(ln2_b[0]))
    return out.reshape(B, S, D)
```

```python
import functools

import jax
import jax.numpy as jnp
from jax import lax
from jax.experimental import pallas as pl
from jax.experimental.pallas import tpu as pltpu

F32 = jnp.float32
BF16 = jnp.bfloat16
I32 = jnp.int32
I16 = jnp.int16

D_MODEL = 1024
N_META = 16
RWKV_WIDTH = 512
ATT_WIDTH = 512
HEAD_DIM = 64
N_HEADS = 8
DECAY_RANK = 64
ICLR_RANK = 64
GATE_RANK = 128
Q_LORA_RANK = 256
IDX_HEADS = 8
IDX_DIM = 64
IDX_ROPE_DIM = 32
INDEX_TOPK = 256
ROPE_THETA = 10000.0
N_GROUPS = 4
EXPERTS_PER_GROUP = 8
N_EXPERTS = N_GROUPS * EXPERTS_PER_GROUP
D_EXPERT = 256
DN_ALPHA = 2.0 ** 0.25
LN_EPS = 1e-5
RMS_EPS = 1e-6
GN_EPS = 64e-5
RWKV_COLS = 3 * RWKV_WIDTH + DECAY_RANK + ICLR_RANK + GATE_RANK
ATT_COLS = Q_LORA_RANK + 2 * ATT_WIDTH + IDX_DIM + IDX_HEADS
ATT_COLS_PAD = 1408

LANES = 128
PAD_ROWS = 256
OFF = PAD_ROWS - N_META
CHUNK = 64
ROWS_A = 256
ROWS_R = 256
TQ = 256
KC = 256
TMG = 512
SCAN_BATCH = 4
EXPERT_LANE0 = 64
NEG = -1e30
LOG2_E = 1.4426950408889634
MASKED = -3e38
BELOW_ALL = -1e38
ABOVE_ALL = 3e38
MAX_REFINE = 400
HALF16 = 32768
VMEM_LIMIT = 56 * 1024 * 1024


def _mm(a, b):
    return jnp.dot(a, b, preferred_element_type=F32)


def _sigmoid(x):
    return 1.0 / (1.0 + jnp.exp(-x))


def _layer_norm(x, g, b):
    mu = jnp.mean(x, -1, keepdims=True)
    xc = x - mu
    var = jnp.mean(xc * xc, -1, keepdims=True)
    return xc * lax.rsqrt(var + LN_EPS) * g + b


def _params(*sem):
    return pltpu.CompilerParams(dimension_semantics=sem, vmem_limit_bytes=VMEM_LIMIT)


def _const_spec(shape):
    nd = len(shape)
    return pl.BlockSpec(shape, lambda *_: (0,) * nd)


def _ln_inproj_body(x_ref, meta_ref, g_ref, b_ref, w_ref, ur_ref, ua_ref):
    blk = pl.program_id(1)
    xin = jnp.where(blk == 0, meta_ref[...], x_ref[0])
    h = _layer_norm(xin, g_ref[...], b_ref[...])
    row = lax.broadcasted_iota(I32, (ROWS_A, 1), 0)
    h = jnp.where((blk > 0) | (row >= OFF), h, 0.0)
    hb = h.astype(BF16)
    step = 256
    for n0 in range(0, RWKV_COLS, step):
        n1 = min(n0 + step, RWKV_COLS)
        ur_ref[0, :, n0:n1] = _mm(hb, w_ref[:, n0:n1])
    for n0 in range(0, ATT_COLS_PAD, step):
        n1 = min(n0 + step, ATT_COLS_PAD)
        ua_ref[0, :, n0:n1] = _mm(hb, w_ref[:, RWKV_COLS + n0:RWKV_COLS + n1])


def _ln_inproj(x, meta_pad, g, b, w):
    B, S, D = x.shape
    nblk = (S + PAD_ROWS) // ROWS_A
    Lp = S + PAD_ROWS
    ncols = RWKV_COLS + ATT_COLS_PAD
    return pl.pallas_call(
        _ln_inproj_body,
        grid=(B, nblk),
        in_specs=[
            pl.BlockSpec((1, ROWS_A, D), lambda b_, i: (b_, jnp.maximum(i - 1, 0), 0)),
            _const_spec((ROWS_A, D)),
            _const_spec((1, D)),
            _const_spec((1, D)),
            _const_spec((D, ncols)),
        ],
        out_specs=[
            pl.BlockSpec((1, ROWS_A, RWKV_COLS), lambda b_, i: (b_, i, 0)),
            pl.BlockSpec((1, ROWS_A, ATT_COLS_PAD), lambda b_, i: (b_, i, 0)),
        ],
        out_shape=[
            jax.ShapeDtypeStruct((B, Lp, RWKV_COLS), F32),
            jax.ShapeDtypeStruct((B, Lp, ATT_COLS_PAD), F32),
        ],
        compiler_params=_params("parallel", "arbitrary"),
        name="ln_inproj",
    )(x, meta_pad, g, b, w)


def _split3(x):
    hi = x.astype(BF16)
    r1 = x - hi.astype(F32)
    mid = r1.astype(BF16)
    return hi, mid, (r1 - mid.astype(F32)).astype(BF16)


def _mm_exact_rhs(x, m):
    return sum(_mm(p, m) for p in _split3(x))


def _mm_exact_lhs(m, x):
    return sum(_mm(m, p) for p in _split3(x))


def _rwkv_prep_body(u_ref, prev_ref, mu_ref, w0_ref, w2_ref, a0_ref, a2_ref, g2_ref, kk_ref, ka_ref, rk_ref,
                    hs_ref, hb_ref, tri_ref,
                    rp_ref, kp_ref, bp_ref, ap_ref, v_ref, pc_ref, g_ref, bonus_ref):
    blk = pl.program_id(1)
    u = u_ref[0]
    prev = jnp.where(blk == 0, 0.0, prev_ref[0][7:8, :])
    row = lax.broadcasted_iota(I32, (ROWS_R, 1), 0)
    shifted = jnp.where(row == 0, prev, pltpu.roll(u, 1, 0))
    ul = u + (shifted - u) * mu_ref[...]
    W = RWKV_WIDTH
    r = ul[:, 0:W]
    k = ul[:, W:2 * W]
    v = ul[:, 2 * W:3 * W]
    wa = ul[:, 3 * W:3 * W + 128]
    gd = ul[:, 3 * W + 128:3 * W + 256]
    w = w0_ref[...] + _mm(jnp.tanh(wa).astype(BF16), w2_ref[...])
    softplus_neg_w = jnp.maximum(-w, 0.0) + jnp.log(1.0 + jnp.exp(-jnp.abs(w)))
    logd = -jnp.exp(-softplus_neg_w - 0.5)
    a = _sigmoid(a0_ref[...] + _mm(wa.astype(BF16), a2_ref[...]))
    g_ref[0] = _mm(_sigmoid(gd).astype(BF16), g2_ref[...])
    head_sum = lambda t: _mm_exact_rhs(_mm_exact_rhs(t, hs_ref[...]), hb_ref[...])
    kkr = k * kk_ref[...]
    kk = kkr / jnp.maximum(jnp.sqrt(head_sum(kkr * kkr)), 1e-12)
    kmod = k * (1.0 + (a - 1.0) * ka_ref[...])
    bonus_ref[0] = head_sum(r * kmod * rk_ref[...]) * v
    v_ref[0] = v
    cum = _mm_exact_lhs(tri_ref[...], logd)
    rp_ref[0] = r * jnp.exp(cum)
    einv = jnp.exp(-cum)
    kp_ref[0] = kmod * einv
    bp_ref[0] = kk * a * einv
    ap_ref[0] = -kk * jnp.exp(cum - logd)
    for c in range(ROWS_R // CHUNK):
        last = c * CHUNK + CHUNK - 1
        pc_ref[0, c] = jnp.broadcast_to(jnp.exp(cum[last:last + 1]), (8, W))


def _rwkv_prep(u_r, mu, w0, w2p, a0, a2p, g2, k_k, k_a, r_k, hs, hb, tri):
    B, Lp, _ = u_r.shape
    nblk = Lp // ROWS_R
    W = RWKV_WIDTH
    row_spec = pl.BlockSpec((1, ROWS_R, W), lambda b_, i: (b_, i, 0))
    row_shape = jax.ShapeDtypeStruct((B, Lp, W), F32)
    cpb = ROWS_R // CHUNK
    return pl.pallas_call(
        _rwkv_prep_body,
        grid=(B, nblk),
        in_specs=[
            pl.BlockSpec((1, ROWS_R, RWKV_COLS), lambda b_, i: (b_, i, 0)),
            pl.BlockSpec((1, 8, RWKV_COLS), lambda b_, i: (b_, jnp.maximum(i * (ROWS_R // 8) - 1, 0), 0)),
            _const_spec((1, RWKV_COLS)),
            _const_spec((1, W)),
            _const_spec((128, W)),
            _const_spec((1, W)),
            _const_spec((128, W)),
            _const_spec((128, W)),
            _const_spec((1, W)),
            _const_spec((1, W)),
            _const_spec((1, W)),
            _const_spec((W, LANES)),
            _const_spec((LANES, W)),
            _const_spec((ROWS_R, ROWS_R)),
        ],
        out_specs=[row_spec] * 5 + [pl.BlockSpec((1, cpb, 8, W), lambda b_, i: (b_, i, 0, 0))] + [row_spec] * 2,
        out_shape=[row_shape] * 5 + [jax.ShapeDtypeStruct((B, Lp // CHUNK, 8, W), F32)] + [row_shape] * 2,
        compiler_params=_params("parallel", "arbitrary"),
        name="rwkv_prep",
    )(u_r, u_r, mu, w0, w2p, a0, a2p, g2, k_k, k_a, r_k, hs, hb, tri)


def _rwkv_scan_body(rp_ref, kp_ref, bp_ref, ap_ref, v_ref, pc_ref, g_ref, bonus_ref, lg_ref, lb_ref, o_ref, s_ref,
                    *, nb):
    c = pl.program_id(1)

    @pl.when(c == 0)
    def _():
        s_ref[...] = jnp.zeros_like(s_ref)

    @pl.when(c < OFF // CHUNK)
    def _():
        o_ref[...] = jnp.zeros_like(o_ref)

    @pl.when(c >= OFF // CHUNK)
    def _():
        _rwkv_chunk(rp_ref, kp_ref, bp_ref, ap_ref, v_ref, pc_ref, g_ref, bonus_ref, lg_ref, lb_ref, o_ref, s_ref, nb)


def _split(x):
    hi = x.astype(BF16)
    return hi, (x - hi.astype(F32)).astype(BF16)


def _dot3(a, b, dims):
    dg = lambda p, q: lax.dot_general(p, q, (dims, ((), ())), preferred_element_type=F32)
    return dg(a[0], b[0]) + dg(a[0], b[1]) + dg(a[1], b[0])


_NN = ((1,), (0,))
_NT = ((1,), (1,))
_TN = ((0,), (0,))


def _rwkv_chunk(rp_ref, kp_ref, bp_ref, ap_ref, v_ref, pc_ref, g_ref, bonus_ref, lg_ref, lb_ref, o_ref, s_ref, nb):
    C, N = CHUNK, HEAD_DIM
    ri = lax.broadcasted_iota(I32, (C, C), 0)
    ci = lax.broadcasted_iota(I32, (C, C), 1)
    strict = ri > ci
    incl = ri >= ci
    eye = jnp.where(ri == ci, 1.0, 0.0)
    units = [(b, slice(h * N, (h + 1) * N)) for b in range(nb) for h in range(N_HEADS)]
    ids = range(len(units))
    rows2 = lambda top, bottom: jnp.concatenate([top, bottom], axis=0)
    pcs = [pc_ref[b, 0, 0:1, sl] for b, sl in units]
    ar = [_split(rows2(ap_ref[b, :, sl], rp_ref[b, :, sl])) for b, sl in units]
    bk = [_split(rows2(bp_ref[b, :, sl], kp_ref[b, :, sl])) for b, sl in units]
    v_ = [_split(v_ref[b, :, sl]) for b, sl in units]
    s0 = [s_ref[i] for i in ids]
    s0s = [_split(s) for s in s0]
    gram = [_dot3(ar[i], bk[i], _NT) for i in ids]
    a_ab = [jnp.where(strict, gram[i][0:C, 0:C], 0.0) for i in ids]
    a_ak = [jnp.where(strict, gram[i][0:C, C:2 * C], 0.0) for i in ids]
    a_rb = [_split(jnp.where(incl, gram[i][C:2 * C, 0:C], 0.0)) for i in ids]
    a_rk = [jnp.where(incl, gram[i][C:2 * C, C:2 * C], 0.0) for i in ids]
    t = [eye + a_ab[i] for i in ids]
    pb = [a_ab[i].astype(BF16) for i in ids]
    for _ in range(C.bit_length() - 2):
        pb = [_mm(pb[i], pb[i]).astype(BF16) for i in ids]
        t = [t[i] + _mm(t[i].astype(BF16), pb[i]) for i in ids]
    ts = [_split(t[i]) for i in ids]
    resid = [(eye - t[i]) + _dot3(_split(a_ab[i]), ts[i], _NN) for i in ids]
    t = [t[i] + _mm(ts[i][0], resid[i].astype(BF16)) for i in ids]
    sp = [_dot3(ar[i], s0s[i], _NT) for i in ids]
    av = [_dot3(_split(rows2(a_ak[i], a_rk[i])), v_[i], _NN) for i in ids]
    u_ = [_dot3(_split(t[i]), _split(sp[i][0:C] + av[i][0:C]), _NN) for i in ids]
    for i, (b, sl) in enumerate(units):
        o = sp[i][C:2 * C] + av[i][C:2 * C] + _dot3(a_rb[i], _split(u_[i]), _NN)
        oc = o - jnp.mean(o, axis=1, keepdims=True)
        var = jnp.mean(oc * oc, axis=1, keepdims=True)
        y = oc * lax.rsqrt(var + GN_EPS) * lg_ref[:, sl] + lb_ref[:, sl]
        o_ref[b, :, sl] = (y + bonus_ref[b, :, sl]) * g_ref[b, :, sl]
    for i, (b, sl) in enumerate(units):
        vu = _split(rows2(v_ref[b, :, sl], u_[i]))
        kb = _split(rows2(kp_ref[b, :, sl], bp_ref[b, :, sl]) * pcs[i])
        s_ref[i] = s0[i] * pcs[i] + _dot3(vu, kb, _TN)


def _rwkv_scan(rp, kp, bp, ap, v, pc, g, bonus, lg, lb):
    B, Lp, W = rp.shape
    nch = Lp // CHUNK
    nb = SCAN_BATCH if B % SCAN_BATCH == 0 else 1
    row_spec = pl.BlockSpec((nb, CHUNK, W), lambda b_, c: (b_, c, 0))
    return pl.pallas_call(
        functools.partial(_rwkv_scan_body, nb=nb),
        grid=(B // nb, nch),
        in_specs=[row_spec] * 5 + [pl.BlockSpec((nb, 1, 8, W), lambda b_, c: (b_, c, 0, 0))] + [row_spec] * 2
        + [_const_spec((1, W))] * 2,
        out_specs=row_spec,
        out_shape=jax.ShapeDtypeStruct((B, Lp, W), F32),
        scratch_shapes=[pltpu.VMEM((nb * N_HEADS, HEAD_DIM, HEAD_DIM), F32)],
        compiler_params=_params("parallel", "arbitrary"),
        name="rwkv_scan",
    )(rp, kp, bp, ap, v, pc, g, bonus, lg, lb)


def _rope(x, cos, sin, half, first):
    width = x.shape[1]
    rot = jnp.where(first, pltpu.roll(x, width - half, 1), pltpu.roll(x, half, 1))
    return x * cos + rot * sin


def _dsa_prep_body(u_ref, qg_ref, wuq_ref, wiq_ref, kng_ref, knb_ref, cf_ref, sf_ref, cp_ref, sp_ref,
                   qt_ref, k_ref, vt_ref, qit_ref, kx_ref, wt_ref):
    u = u_ref[0]
    cq = u[:, 0:Q_LORA_RANK]
    k = u[:, Q_LORA_RANK:Q_LORA_RANK + ATT_WIDTH]
    v = u[:, Q_LORA_RANK + ATT_WIDTH:Q_LORA_RANK + 2 * ATT_WIDTH]
    tail = u[:, Q_LORA_RANK + 2 * ATT_WIDTH:]
    cqn = (cq * lax.rsqrt(jnp.mean(cq * cq, -1, keepdims=True) + RMS_EPS) * qg_ref[...]).astype(BF16)
    reps = ATT_WIDTH // LANES
    cf = jnp.concatenate([cf_ref[...]] * reps, axis=1)
    sf = jnp.concatenate([sf_ref[...]] * reps, axis=1)
    cp = jnp.concatenate([cp_ref[...]] * reps, axis=1)
    sp = jnp.concatenate([sp_ref[...]] * reps, axis=1)
    lane_w = lax.broadcasted_iota(I32, (1, ATT_WIDTH), 1) % HEAD_DIM
    first_f = lane_w < HEAD_DIM // 2
    first_p = lane_w < IDX_ROPE_DIM // 2
    q = _rope(_mm(cqn, wuq_ref[...]), cf, sf, HEAD_DIM // 2, first_f)
    qt_ref[0] = (q * (HEAD_DIM ** -0.5 * LOG2_E)).T.astype(BF16)
    qi = _rope(_mm(cqn, wiq_ref[...]), cp, sp, IDX_ROPE_DIM // 2, first_p)
    qit_ref[0] = qi.T.astype(BF16)
    k_ref[0] = _rope(k, cf, sf, HEAD_DIM // 2, first_f).astype(BF16)
    vt_ref[0] = v.T.astype(BF16)
    lane = lax.broadcasted_iota(I32, (1, LANES), 1)
    is_key = lane < IDX_DIM
    mu = jnp.sum(jnp.where(is_key, tail, 0.0), -1, keepdims=True) * (1.0 / IDX_DIM)
    tc = jnp.where(is_key, tail - mu, 0.0)
    var = jnp.sum(tc * tc, -1, keepdims=True) * (1.0 / IDX_DIM)
    kn = tc * lax.rsqrt(var + LN_EPS) * kng_ref[...] + knb_ref[...]
    kn = _rope(kn, cp_ref[...], sp_ref[...], IDX_ROPE_DIM // 2, (lane % HEAD_DIM) < IDX_ROPE_DIM // 2)
    kn = kn * (IDX_DIM ** -0.5)
    kx_ref[0] = jnp.where(is_key, kn, pltpu.roll(kn, IDX_DIM, 1)).astype(BF16)
    wt_ref[0] = (tail * (IDX_HEADS ** -0.5)).T


def _dsa_prep(u_a, qg, wuq, wiq, kng, knb, cf, sf, cp, sp):
    B, Lp, _ = u_a.shape
    W = ATT_WIDTH
    row = lambda width: pl.BlockSpec((1, ROWS_A, width), lambda b_, i: (b_, i, 0))
    col = lambda height: pl.BlockSpec((1, height, ROWS_A), lambda b_, i: (b_, 0, i))
    tab = pl.BlockSpec((ROWS_A, LANES), lambda b_, i: (i, 0))
    return pl.pallas_call(
        _dsa_prep_body,
        grid=(B, Lp // ROWS_A),
        in_specs=[row(ATT_COLS_PAD), _const_spec((1, Q_LORA_RANK)), _const_spec((Q_LORA_RANK, W)),
                  _const_spec((Q_LORA_RANK, W)), _const_spec((1, LANES)), _const_spec((1, LANES)),
                  tab, tab, tab, tab],
        out_specs=[col(W), row(W), col(W), col(W), row(LANES), col(LANES)],
        out_shape=[jax.ShapeDtypeStruct((B, W, Lp), BF16), jax.ShapeDtypeStruct((B, Lp, W), BF16),
                   jax.ShapeDtypeStruct((B, W, Lp), BF16), jax.ShapeDtypeStruct((B, W, Lp), BF16),
                   jax.ShapeDtypeStruct((B, Lp, LANES), BF16), jax.ShapeDtypeStruct((B, LANES, Lp), F32)],
        compiler_params=_params("parallel", "arbitrary"),
        name="dsa_prep",
    )(u_a, qg, wuq, wiq, kng, knb, cf, sf, cp, sp)


def _fold_rows(x, op=jnp.add):
    parts = [x[8 * r:8 * r + 8] for r in range(x.shape[0] // 8)]
    while len(parts) > 1:
        parts = [op(a, b) for a, b in zip(parts[0::2], parts[1::2])] + parts[len(parts) & ~1:]
    return parts[0]


def _dsa_attn_body(qt_ref, qit_ref, wt_ref, k_ref, vt_ref, kx_ref, o_ref, khi_ref, klo_ref, sc_ref, acc_ref,
                   j_ref, s_ref, pe_ref, *, ksel):
    i = pl.program_id(1)
    nkc = (i * TQ + TQ - 1) // KC + 1
    kf = float(ksel)
    tcol = i * TQ + lax.broadcasted_iota(I32, (1, TQ), 1)
    row_in_pair = lax.broadcasted_iota(I32, (LANES, 1), 0)

    def head_operands(ref):
        out = []
        for h in range(N_HEADS):
            pair = ref[0, LANES * (h // 2):LANES * (h // 2 + 1), :]
            keep = (row_in_pair < HEAD_DIM) if h % 2 == 0 else (row_in_pair >= HEAD_DIM)
            out.append(jnp.where(keep, pair, jnp.zeros_like(pair)))
        return out

    def key_rows(ks):
        return ks + lax.broadcasted_iota(I32, (KC, 1), 0)

    qis = head_operands(qit_ref)
    wrows = [wt_ref[0, IDX_DIM + h:IDX_DIM + h + 1, :] for h in range(IDX_HEADS)]

    def score_chunk(kc, carry):
        lo8, hi8 = carry
        ks = pl.multiple_of(kc * KC, KC)
        kx = kx_ref[0, pl.ds(ks, KC), :]
        sc = jnp.zeros((KC, TQ), F32)
        for h in range(IDX_HEADS):
            sc = sc + jnp.maximum(_mm(kx, qis[h]), 0.0) * wrows[h]
        sc = sc + 0.0
        krow = key_rows(ks)
        sc = jnp.where(krow >= OFF, sc, MASKED)
        sc = jnp.where(krow <= tcol, sc, MASKED)
        sc_ref[pl.ds(ks, KC), :] = sc
        bits = lax.bitcast_convert_type(sc, I32)
        key = jnp.where(bits >= 0, bits, bits ^ jnp.int32(0x7FFFFFFF))
        khi_ref[pl.ds(ks, KC), :] = lax.shift_right_arithmetic(key, 16).astype(I16)
        klo_ref[pl.ds(ks, KC), :] = ((key & jnp.int32(0xFFFF)) - HALF16).astype(I16)
        lo8 = jnp.minimum(lo8, _fold_rows(jnp.where(sc <= MASKED, ABOVE_ALL, sc), jnp.minimum))
        hi8 = jnp.maximum(hi8, _fold_rows(sc, jnp.maximum))
        return lo8, hi8

    lo8, hi8 = lax.fori_loop(0, nkc, score_chunk,
                             (jnp.full((8, TQ), ABOVE_ALL, F32), jnp.full((8, TQ), MASKED, F32)))
    smin = jnp.min(lo8, axis=0, keepdims=True)
    smax = jnp.max(hi8, axis=0, keepdims=True)

    def scan_chunks(fn, init):
        def body(kc, carry):
            ks = pl.multiple_of(kc * KC, KC)
            return fn(carry, ks)
        return lax.fori_loop(0, nkc, body, init)

    zeros8 = jnp.zeros((8, TQ), F32)

    def count_where(ref, pred):
        cnt = scan_chunks(lambda c, ks: c + _fold_rows(pred(ref[pl.ds(ks, KC), :], ks)), zeros8)
        return jnp.sum(cnt, axis=0, keepdims=True)

    def count16(ref, pred):
        def fn(cnt, ks):
            m = pred(ref[pl.ds(ks, KC), :])
            parts = [m[16 * r:16 * r + 16] for r in range(KC // 16)]
            while len(parts) > 1:
                parts = [a + b for a, b in zip(parts[0::2], parts[1::2])]
            return cnt + parts[0]
        cnt = scan_chunks(fn, jnp.zeros((16, TQ), I16))
        return jnp.sum(cnt.astype(I32), axis=0, keepdims=True)

    one16, zero16 = jnp.int16(1), jnp.int16(0)

    def radix16(ref, target):
        def bit(bi, prefix):
            cand = prefix | lax.shift_left(jnp.int32(1), 15 - bi)
            cand16 = (cand - HALF16).astype(I16)
            cnt = count16(ref, lambda x: jnp.where(x >= cand16, one16, zero16))
            return jnp.where(cnt >= target, cand, prefix)
        return lax.fori_loop(0, 16, bit, jnp.zeros((1, TQ), I32))

    k_int = jnp.full((1, TQ), ksel, I32)
    thr_hi = radix16(khi_ref, k_int) - HALF16
    thr_hi16 = thr_hi.astype(I16)
    above = count16(khi_ref, lambda x: jnp.where(x > thr_hi16, one16, zero16))

    def keep_low_of_ties(carry, ks):
        rows = pl.ds(ks, KC)
        klo_ref[rows, :] = jnp.where(khi_ref[rows, :] == thr_hi16, klo_ref[rows, :], jnp.int16(-HALF16))
        return carry

    scan_chunks(keep_low_of_ties, 0)
    thr_lo = radix16(klo_ref, k_int - above)
    thr_key = lax.shift_left(thr_hi, 16) | thr_lo
    cand0 = lax.bitcast_convert_type(jnp.where(thr_key >= 0, thr_key, thr_key ^ jnp.int32(0x7FFFFFFF)), F32)

    n_adm = jnp.maximum(tcol - (OFF - 1), 0).astype(F32)
    searching = n_adm > kf

    def probe(mid):
        def fn(carry, ks):
            cnt, vmin = carry
            s = sc_ref[pl.ds(ks, KC), :]
            ge = s >= mid
            return (cnt + _fold_rows(jnp.where(ge, 1.0, 0.0)),
                    jnp.minimum(vmin, _fold_rows(jnp.where(ge, s, ABOVE_ALL), jnp.minimum)))
        cnt, vmin = scan_chunks(fn, (zeros8, jnp.full((8, TQ), ABOVE_ALL, F32)))
        return jnp.sum(cnt, axis=0, keepdims=True), jnp.min(vmin, axis=0, keepdims=True)

    def refine(state):
        it, lo, hi, c_lo, c_gt, done, _ = state
        mid = jnp.where(it == 0, jnp.where(searching, cand0, lo), lo + 0.5 * (hi - lo))
        c_mid, v_mid = probe(mid)
        up = c_mid >= kf
        lo_n = jnp.where(up, v_mid, lo)
        hi_n = jnp.where(up, hi, mid)
        c_lo_n = jnp.where(up, c_mid, c_lo)
        c_gt_n = count_where(sc_ref, lambda s, ks: jnp.where(s > lo_n, 1.0, 0.0))
        stalled = jnp.where(it > 0, jnp.where(mid <= lo, 1.0, jnp.where(mid >= hi, 1.0, 0.0)), 0.0)
        fin = jnp.maximum(jnp.where(c_gt_n < kf, 1.0, 0.0), stalled)
        frozen = done > 0.0
        keep = lambda old, new_: jnp.where(frozen, old, new_)
        done_n = jnp.maximum(done, fin)
        return (it + 1, keep(lo, lo_n), keep(hi, hi_n), keep(c_lo, c_lo_n), keep(c_gt, c_gt_n), done_n,
                jnp.max(1.0 - done_n))

    done0 = jnp.where(searching, 0.0, 1.0)
    state0 = (jnp.int32(0), smin, smax + (jnp.abs(smax) + 1.0) * 1e-6, n_adm, n_adm, done0, jnp.max(1.0 - done0))
    state = lax.while_loop(lambda st: jnp.logical_and(st[6] > 0.0, st[0] < MAX_REFINE), refine, state0)
    thr = jnp.where(searching, state[1], BELOW_ALL)
    cnt_gt = jnp.where(searching, state[4], n_adm)
    cnt_eq = jnp.where(searching, state[3] - state[4], 0.0)
    need = kf - cnt_gt

    j_ref[...] = jnp.full(j_ref.shape, 2 ** 30, I32)

    @pl.when(jnp.max(cnt_eq - need) > 0.0)
    def _():
        def index_bit(bi, prefix):
            cand = prefix | lax.shift_left(jnp.int32(1), 12 - bi)
            before = count_where(
                sc_ref, lambda s, ks: jnp.where(s == thr, jnp.where(key_rows(ks) < cand, 1.0, 0.0), 0.0))
            return jnp.where(before < need, cand, prefix)
        jst = lax.fori_loop(0, 13, index_bit, jnp.zeros((1, TQ), I32))
        j_ref[...] = jnp.broadcast_to(jst, j_ref.shape)

    jstar = j_ref[0:1, :]

    qs = head_operands(qt_ref)
    acc_ref[...] = jnp.zeros_like(acc_ref)

    ones_rows = jnp.ones((16, KC), BF16)

    def attend_chunk(kc, carry):
        ms, ls = carry
        ks = pl.multiple_of(kc * KC, KC)
        sc = sc_ref[pl.ds(ks, KC), :]
        tie = jnp.where(sc == thr, jnp.where(key_rows(ks) <= jstar, 0.0, NEG), NEG)
        bias = jnp.where(sc > thr, 0.0, tie)
        chunk_max = []
        for h in range(N_HEADS):
            p = h // 2
            kp = k_ref[0, pl.ds(ks, KC), LANES * p:LANES * (p + 1)]
            s = _mm(kp, qs[h]) + bias
            s_ref[h] = s
            chunk_max.append(jnp.max(s, axis=0, keepdims=True))
        new_ms, new_ls, alphas = [], [], []
        for h in range(N_HEADS):
            m_new = jnp.maximum(ms[h], chunk_max[h])
            alphas.append(jnp.exp2(ms[h] - m_new))
            new_ms.append(m_new)
            pe_ref[h] = jnp.exp2(s_ref[h] - m_new).astype(BF16)
        for h in range(N_HEADS):
            vt = vt_ref[0, HEAD_DIM * h:HEAD_DIM * (h + 1), pl.ds(ks, KC)]
            pv = _mm(jnp.concatenate([vt, ones_rows], axis=0), pe_ref[h])
            rows = slice(HEAD_DIM * h, HEAD_DIM * (h + 1))
            acc_ref[rows, :] = acc_ref[rows, :] * alphas[h] + pv[0:HEAD_DIM]
            new_ls.append(alphas[h] * ls[h] + pv[HEAD_DIM:HEAD_DIM + 1])
        return tuple(new_ms), tuple(new_ls)

    init = (tuple(jnp.full((1, TQ), NEG, F32) for _ in range(N_HEADS)),
            tuple(jnp.zeros((1, TQ), F32) for _ in range(N_HEADS)))
    _, ls = lax.fori_loop(0, nkc, attend_chunk, init)
    for p in range(N_HEADS // 2):
        parts = [acc_ref[HEAD_DIM * h:HEAD_DIM * (h + 1), :] / ls[h] for h in (2 * p, 2 * p + 1)]
        o_ref[0, :, LANES * p:LANES * (p + 1)] = jnp.concatenate(parts, axis=0).T


def _dsa_attn(qt, k, vt, qit, kx, wt, ksel):
    B, Lp, W = k.shape
    assert Lp % KC == 0 and Lp <= 8192
    qcol = lambda height: pl.BlockSpec((1, height, TQ), lambda b_, i: (b_, 0, i))
    full = lambda shape: pl.BlockSpec((1,) + shape, lambda b_, i: (b_, 0, 0))
    return pl.pallas_call(
        functools.partial(_dsa_attn_body, ksel=ksel),
        grid=(B, Lp // TQ),
        in_specs=[qcol(W), qcol(W), qcol(LANES), full((Lp, W)), full((W, Lp)), full((Lp, LANES))],
        out_specs=pl.BlockSpec((1, TQ, W), lambda b_, i: (b_, i, 0)),
        out_shape=jax.ShapeDtypeStruct((B, Lp, W), F32),
        scratch_shapes=[
            pltpu.VMEM((Lp, TQ), I16),
            pltpu.VMEM((Lp, TQ), I16),
            pltpu.VMEM((Lp, TQ), F32),
            pltpu.VMEM((W, TQ), F32),
            pltpu.VMEM((8, TQ), I32),
            pltpu.VMEM((N_HEADS, KC, TQ), F32),
            pltpu.VMEM((N_HEADS, KC, TQ), BF16),
        ],
        compiler_params=_params("parallel", "arbitrary"),
        name="dsa_attn",
    )(qt, qit, wt, k, vt, kx)


def _outproj_router_body(x_ref, yr_ref, ya_ref, eg_ref, eb_ref, wo_ref, g1_ref, b1_ref, wrh_ref, wrl_ref, br_ref,
                         h_ref, grp_ref):
    h0 = _layer_norm(x_ref[0], eg_ref[...], eb_ref[...])
    mix = (_mm(yr_ref[0].astype(BF16), wo_ref[0:RWKV_WIDTH, :])
           + _mm(ya_ref[0].astype(BF16), wo_ref[RWKV_WIDTH:, :]))
    h1 = _layer_norm(DN_ALPHA * h0 + mix, g1_ref[...], b1_ref[...])
    h_ref[0, :, 0:D_MODEL] = h1
    logits = _dot3(_split(h1), (wrh_ref[...], wrl_ref[...]), _NN) + br_ref[...]
    lane = lax.broadcasted_iota(I32, (1, LANES), 1)
    lanef = lane.astype(F32)
    low = -3e38
    lgm = jnp.where(lane < N_GROUPS, logits, low)
    gmax = jnp.max(lgm, axis=1, keepdims=True)
    gsel = jnp.min(jnp.where(lgm == gmax, lanef, 1e9), axis=1, keepdims=True)
    gsum = jnp.sum(jnp.where(lane < N_GROUPS, jnp.exp(lgm - gmax), 0.0), axis=1, keepdims=True)
    group_of_lane = ((lane - EXPERT_LANE0) // EXPERTS_PER_GROUP).astype(F32)
    lem = jnp.where(group_of_lane == gsel, logits, low)
    m1 = jnp.max(lem, axis=1, keepdims=True)
    i1 = jnp.min(jnp.where(lem == m1, lanef, 1e9), axis=1, keepdims=True)
    lem2 = jnp.where(lanef == i1, low, lem)
    m2 = jnp.max(lem2, axis=1, keepdims=True)
    i2 = jnp.min(jnp.where(lem2 == m2, lanef, 1e9), axis=1, keepdims=True)
    e2 = jnp.exp(m2 - m1)
    w1 = 1.0 / (1.0 + e2)
    w2 = e2 / (1.0 + e2)
    gates = jnp.where(lanef == i1, w1, jnp.where(lanef == i2, w2, 0.0)) / gsum
    h_ref[0, :, D_MODEL:] = gates
    grp_ref[0] = jnp.broadcast_to(gsel, (ROWS_A, LANES))


def _outproj_router(x, y_r, y_a, eg, eb, wo, g1, b1, wrh, wrl, br):
    B, S, D = x.shape
    skip = PAD_ROWS // ROWS_A
    xrow = pl.BlockSpec((1, ROWS_A, D), lambda b_, i: (b_, i, 0))
    yrow = pl.BlockSpec((1, ROWS_A, RWKV_WIDTH), lambda b_, i: (b_, i + skip, 0))
    vec = _const_spec((1, D))
    return pl.pallas_call(
        _outproj_router_body,
        grid=(B, S // ROWS_A),
        in_specs=[xrow, yrow, yrow, vec, vec, _const_spec((D, D)), vec, vec,
                  _const_spec((D, LANES)), _const_spec((D, LANES)), _const_spec((1, LANES))],
        out_specs=[pl.BlockSpec((1, ROWS_A, D + LANES), lambda b_, i: (b_, i, 0)),
                   pl.BlockSpec((1, ROWS_A, LANES), lambda b_, i: (b_, i, 0))],
        out_shape=[jax.ShapeDtypeStruct((B, S, D + LANES), F32), jax.ShapeDtypeStruct((B, S, LANES), F32)],
        compiler_params=_params("parallel", "arbitrary"),
        name="outproj_router",
    )(x, y_r, y_a, eg, eb, wo, g1, b1, wrh, wrl, br)


def _moe_body(tgrp_ref, tcnt_ref, idx_ref, idx_next_ref, idx_prev_ref, hx_ref, wg_ref, wu_ref, wd_ref, g2_ref, b2_ref,
              out_ref, xg_ref, acc_ref, hb_ref, ob_ref, gsem, ssem):
    i = pl.program_id(0)
    e = pl.program_id(1)
    n_tiles = pl.num_programs(0)
    n = tcnt_ref[i]
    slot = i % 2
    D = D_MODEL

    def row_gather(rows_ref, r, to_slot):
        return pltpu.make_async_copy(hx_ref.at[pl.ds(rows_ref[0, 0, r], 1)], xg_ref.at[to_slot, pl.ds(r, 1)],
                                     gsem.at[to_slot])

    def row_scatter(rows_ref, r, from_slot):
        return pltpu.make_async_copy(ob_ref.at[from_slot, pl.ds(r, 1)], out_ref.at[pl.ds(rows_ref[0, 0, r], 1)],
                                     ssem.at[from_slot])

    def for_rows(count, fn):
        lax.fori_loop(0, count, lambda r, c: (fn(r), c)[1], 0)

    @pl.when(jnp.logical_and(i == 0, e == 0))
    def _():
        for_rows(n, lambda r: row_gather(idx_ref, r, slot).start())

    has_next = i + 1 < n_tiles
    n_next = tcnt_ref[jnp.minimum(i + 1, n_tiles - 1)]
    spread = jnp.logical_and(has_next, n_next == TMG)
    slice_rows = TMG // EXPERTS_PER_GROUP

    @pl.when(jnp.logical_and(jnp.logical_and(e == 1, has_next), n_next < TMG))
    def _():
        for_rows(n_next, lambda r: row_gather(idx_next_ref, r, 1 - slot).start())

    @pl.when(jnp.logical_and(e == 0, n == TMG))
    def _():
        pltpu.make_async_copy(hx_ref.at[pl.ds(0, TMG)], xg_ref.at[slot], gsem.at[slot]).wait()

    @pl.when(jnp.logical_and(e == 0, n < TMG))
    def _():
        for_rows(n, lambda r: row_gather(idx_ref, r, slot).wait())

    @pl.when(jnp.logical_and(e == 0, n > 0))
    def _():
        acc_ref[...] = jnp.zeros_like(acc_ref)
        hb_ref[...] = xg_ref[slot, :, 0:D].astype(BF16)

    def expert_step():
        t = hb_ref[...]
        lane = lax.broadcasted_iota(I32, (1, LANES), 1)
        gate_lane = EXPERT_LANE0 + tgrp_ref[i] * EXPERTS_PER_GROUP + e
        gcol = jnp.sum(jnp.where(lane == gate_lane, xg_ref[slot, :, D:], 0.0), axis=1, keepdims=True)
        a = _mm(t, wg_ref[0])
        hid = a * _sigmoid(a) * _mm(t, wu_ref[0]) * gcol
        acc_ref[...] += _mm(hid.astype(BF16), wd_ref[0])

    n_prev = tcnt_ref[jnp.maximum(i - 1, 0)]
    drain_prev = jnp.logical_and(jnp.logical_and(i > 0, n_prev == TMG), n > 0)
    deferred = jnp.logical_and(jnp.logical_and(n == TMG, has_next), n_next > 0)

    def next_gather_slice():
        for j in range(slice_rows):
            row_gather(idx_next_ref, e * slice_rows + j, 1 - slot).start()

    def prev_scatter_slice():
        for j in range(slice_rows):
            row_scatter(idx_prev_ref, e * slice_rows + j, 1 - slot).start()

    for do_gather in (False, True):
        for do_scatter in (False, True):
            cond = jnp.logical_and(n > 0, jnp.logical_and(spread == do_gather, drain_prev == do_scatter))

            @pl.when(cond)
            def _(do_gather=do_gather, do_scatter=do_scatter):
                if do_gather:
                    next_gather_slice()
                if do_scatter:
                    prev_scatter_slice()
                expert_step()

    last = e == EXPERTS_PER_GROUP - 1

    @pl.when(jnp.logical_and(last, drain_prev))
    def _():
        pltpu.make_async_copy(ob_ref.at[1 - slot], out_ref.at[pl.ds(0, TMG)], ssem.at[1 - slot]).wait()

    @pl.when(jnp.logical_and(last, n > 0))
    def _():
        ob_ref[slot] = _layer_norm(DN_ALPHA * xg_ref[slot, :, 0:D] + acc_ref[...], g2_ref[...], b2_ref[...])

    @pl.when(jnp.logical_and(last, jnp.logical_and(n == TMG, jnp.logical_not(deferred))))
    def _():
        lax.fori_loop(0, TMG, lambda r, c: (row_scatter(idx_ref, r, slot).start(), c)[1], 0, unroll=8)
        pltpu.make_async_copy(ob_ref.at[slot], out_ref.at[pl.ds(0, TMG)], ssem.at[slot]).wait()

    @pl.when(jnp.logical_and(last, n < TMG))
    def _():
        for_rows(n, lambda r: row_scatter(idx_ref, r, slot).start())
        for_rows(n, lambda r: row_scatter(idx_ref, r, slot).wait())


def _moe(hx, tile_group, tile_count, tile_rows, wg, wu, wd, g2, b2):
    T, DX = hx.shape
    D = D_MODEL
    n_tiles = tile_rows.shape[0]
    wspec = lambda shape: pl.BlockSpec(
        (1,) + shape, lambda i, e, tg, tc: (tg[i] * EXPERTS_PER_GROUP + e, 0, 0))
    vec = pl.BlockSpec((1, D), lambda i, e, tg, tc: (0, 0))
    rows_of = lambda step: pl.BlockSpec(
        (1, 1, TMG), lambda i, e, tg, tc: (jnp.clip(i + step, 0, n_tiles - 1), 0, 0), memory_space=pltpu.SMEM)
    return pl.pallas_call(
        _moe_body,
        grid_spec=pltpu.PrefetchScalarGridSpec(
            num_scalar_prefetch=2,
            grid=(n_tiles, EXPERTS_PER_GROUP),
            in_specs=[rows_of(0), rows_of(1), rows_of(-1), pl.BlockSpec(memory_space=pl.ANY),
                      wspec((D, D_EXPERT)), wspec((D, D_EXPERT)), wspec((D_EXPERT, D)), vec, vec],
            out_specs=pl.BlockSpec(memory_space=pl.ANY),
            scratch_shapes=[pltpu.VMEM((2, TMG, DX), F32), pltpu.VMEM((TMG, D), F32), pltpu.VMEM((TMG, D), BF16),
                            pltpu.VMEM((2, TMG, D), F32), pltpu.SemaphoreType.DMA((2,)),
                            pltpu.SemaphoreType.DMA((2,))],
        ),
        out_shape=jax.ShapeDtypeStruct((T, D), F32),
        compiler_params=_params("arbitrary", "arbitrary"),
        name="moe",
    )(tile_group, tile_count, tile_rows, tile_rows, tile_rows, hx, wg, wu, wd, g2, b2)


def _group_tiles(grp, n_tiles):
    T = grp.shape[0]
    onehot = (grp[:, None] == jnp.arange(N_GROUPS)[None, :]).astype(I32)
    rank = jnp.cumsum(onehot, axis=0) - onehot
    count = jnp.sum(onehot, axis=0)
    tiles_per_group = (count + TMG - 1) // TMG
    first_tile = jnp.cumsum(tiles_per_group) - tiles_per_group
    pos = first_tile[grp] * TMG + jnp.sum(rank * onehot, axis=1)
    tile_rows = jnp.zeros((n_tiles * TMG,), I32).at[pos].set(jnp.arange(T, dtype=I32)).reshape(n_tiles, 1, TMG)
    tile = jnp.arange(n_tiles)
    tile_group = jnp.clip(jnp.sum((tile[:, None] >= first_tile[None, :]).astype(I32), axis=1) - 1, 0, N_GROUPS - 1)
    in_group = tile - first_tile[tile_group]
    tile_count = jnp.clip(count[tile_group] - in_group * TMG, 0, TMG)
    tile_count = jnp.where(in_group < tiles_per_group[tile_group], tile_count, 0)
    return tile_group.astype(I32), tile_count.astype(I32), tile_rows


def _rope_tables(Lp):
    pos = jnp.maximum(jnp.arange(Lp, dtype=I32) - OFF, 0).astype(F32)
    j = jnp.arange(LANES) % HEAD_DIM

    def table(half, rot_dim):
        inv = 1.0 / (ROPE_THETA ** (jnp.arange(half, dtype=F32) / half))
        ang = pos[:, None] * inv[None, :]
        cos, sin = jnp.cos(ang)[:, j % half], jnp.sin(ang)[:, j % half]
        rotated = (j < rot_dim)[None, :]
        sign = jnp.where(j < half, -1.0, 1.0)[None, :]
        return jnp.where(rotated, cos, 1.0), jnp.where(rotated, sin * sign, 0.0)

    cf, sf = table(HEAD_DIM // 2, HEAD_DIM)
    cp, sp = table(IDX_ROPE_DIM // 2, IDX_ROPE_DIM)
    return cf, sf, cp, sp


def _block_ones(n, block):
    idx = jnp.arange(n) // block
    return (idx[:, None] == idx[None, :]).astype(F32)


def kernel(x, meta_tokens, ln_emb_g, ln_emb_b, w_in, rw_mu, rw_w0, rw_w2, rw_a0, rw_a2, rw_g2, rw_kk, rw_ka,
           rw_rk, rw_lnx_g, rw_lnx_b, att_qnorm_g, att_wuq, idx_wq, idx_knorm_g, idx_knorm_b, w_out, ln1_g,
           ln1_b, rt_grp_w, rt_grp_b, rt_exp_w, rt_exp_b, ex_w_gate, ex_w_up, ex_w_down, ln2_g, ln2_b):
    B, S, D = x.shape
    assert w_in.shape[0] == 1 and D == D_MODEL and (B * S) % TMG == 0
    Lp = S + PAD_ROWS
    ksel = min(INDEX_TOPK, S // 4)
    row = lambda t: t.reshape(1, -1)
    W = RWKV_WIDTH

    meta_pad = jnp.zeros((PAD_ROWS, D), F32).at[OFF:].set(meta_tokens)
    w_in_p = jnp.pad(w_in[0], ((0, 0), (0, ATT_COLS_PAD - ATT_COLS))).astype(BF16)
    u_r, u_a = _ln_inproj(x, meta_pad, row(ln_emb_g), row(ln_emb_b), w_in_p)

    w2p = jnp.concatenate([rw_w2[0], jnp.zeros((ICLR_RANK, W), F32)], 0).astype(BF16)
    a2p = jnp.concatenate([jnp.zeros((DECAY_RANK, W), F32), rw_a2[0]], 0).astype(BF16)
    head_of_lane = jnp.arange(W) // HEAD_DIM
    hs = (head_of_lane[:, None] == jnp.arange(LANES)[None, :]).astype(BF16)
    tri = (jnp.arange(ROWS_R)[:, None] >= jnp.arange(ROWS_R)[None, :]).astype(F32)
    tri = (tri * _block_ones(ROWS_R, CHUNK)).astype(BF16)
    rp, kp, bp, ap, v, pc, g, bonus = _rwkv_prep(
        u_r, row(rw_mu[0]), row(rw_w0[0]), w2p, row(rw_a0[0]), a2p, rw_g2[0].astype(BF16), row(rw_kk[0]),
        row(rw_ka[0]), row(rw_rk[0]), hs, hs.T, tri)
    y_r = _rwkv_scan(rp, kp, bp, ap, v, pc, g, bonus, row(rw_lnx_g[0]), row(rw_lnx_b[0]))

    pad_lanes = lambda t: jnp.pad(t, (0, LANES - t.shape[0])).reshape(1, LANES)
    cf, sf, cp, sp = _rope_tables(Lp)
    qt, k, vt, qit, kx, wt = _dsa_prep(
        u_a, row(att_qnorm_g[0]), att_wuq[0].astype(BF16), idx_wq[0].astype(BF16),
        pad_lanes(idx_knorm_g[0]), pad_lanes(idx_knorm_b[0]), cf, sf, cp, sp)
    y_a = _dsa_attn(qt, k, vt, qit, kx, wt, ksel)

    wr = jnp.zeros((D, LANES), F32).at[:, :N_GROUPS].set(rt_grp_w[0])
    wr = wr.at[:, EXPERT_LANE0:EXPERT_LANE0 + N_EXPERTS].set(rt_exp_w[0])
    br = jnp.zeros((1, LANES), F32).at[0, :N_GROUPS].set(rt_grp_b[0])
    br = br.at[0, EXPERT_LANE0:EXPERT_LANE0 + N_EXPERTS].set(rt_exp_b[0])
    wrh = wr.astype(BF16)
    wrl = (wr - wrh.astype(F32)).astype(BF16)
    h1, grp = _outproj_router(x, y_r, y_a, row(ln_emb_g), row(ln_emb_b), w_out[0].astype(BF16),
                                row(ln1_g[0]), row(ln1_b[0]), wrh, wrl, br)
    T = B * S
    n_tiles = T // TMG + N_GROUPS
    tile_group, tile_count, tile_rows = _group_tiles(grp.reshape(T, LANES)[:, 0].astype(I32), n_tiles)
    out = _moe(h1.reshape(T, D + LANES), tile_group, tile_count, tile_rows, ex_w_gate[0].astype(BF16),
               ex_w_up[0].astype(BF16), ex_w_down[0].astype(BF16), row(ln2_g[0]), row(ln2_b[0]))
    return out.reshape(B, S, D)
```

```python
import functools

import jax
import jax.numpy as jnp
from jax import lax
from jax.experimental import pallas as pl
from jax.experimental.pallas import tpu as pltpu

F32 = jnp.float32
BF16 = jnp.bfloat16
I32 = jnp.int32
I16 = jnp.int16

D_MODEL = 1024
N_META = 16
RWKV_WIDTH = 512
ATT_WIDTH = 512
HEAD_DIM = 64
N_HEADS = 8
DECAY_RANK = 64
ICLR_RANK = 64
GATE_RANK = 128
Q_LORA_RANK = 256
IDX_HEADS = 8
IDX_DIM = 64
IDX_ROPE_DIM = 32
INDEX_TOPK = 256
ROPE_THETA = 10000.0
N_GROUPS = 4
EXPERTS_PER_GROUP = 8
N_EXPERTS = N_GROUPS * EXPERTS_PER_GROUP
D_EXPERT = 256
DN_ALPHA = 2.0 ** 0.25
LN_EPS = 1e-5
RMS_EPS = 1e-6
GN_EPS = 64e-5
RWKV_COLS = 3 * RWKV_WIDTH + DECAY_RANK + ICLR_RANK + GATE_RANK
ATT_COLS = Q_LORA_RANK + 2 * ATT_WIDTH + IDX_DIM + IDX_HEADS
ATT_COLS_PAD = 1408

LANES = 128
PAD_ROWS = 256
OFF = PAD_ROWS - N_META
CHUNK = 64
ROWS_A = 256
ROWS_R = 256
TQ = 256
KC = 256
TMG = 512
SCAN_BATCH = 4
EXPERT_LANE0 = 64
NEG = -1e30
LOG2_E = 1.4426950408889634
MASKED = -3e38
BELOW_ALL = -1e38
ABOVE_ALL = 3e38
MAX_REFINE = 400
HALF16 = 32768
VMEM_LIMIT = 56 * 1024 * 1024


def _mm(a, b):
    return jnp.dot(a, b, preferred_element_type=F32)


def _sigmoid(x):
    return 1.0 / (1.0 + jnp.exp(-x))


def _layer_norm(x, g, b):
    mu = jnp.mean(x, -1, keepdims=True)
    xc = x - mu
    var = jnp.mean(xc * xc, -1, keepdims=True)
    return xc * lax.rsqrt(var + LN_EPS) * g + b


def _params(*sem):
    return pltpu.CompilerParams(dimension_semantics=sem, vmem_limit_bytes=VMEM_LIMIT)


def _const_spec(shape):
    nd = len(shape)
    return pl.BlockSpec(shape, lambda *_: (0,) * nd)


def _ln_inproj_body(x_ref, meta_ref, g_ref, b_ref, w_ref, ur_ref, ua_ref):
    blk = pl.program_id(1)
    xin = jnp.where(blk == 0, meta_ref[...], x_ref[0])
    h = _layer_norm(xin, g_ref[...], b_ref[...])
    row = lax.broadcasted_iota(I32, (ROWS_A, 1), 0)
    h = jnp.where((blk > 0) | (row >= OFF), h, 0.0)
    hb = h.astype(BF16)
    step = 256
    for n0 in range(0, RWKV_COLS, step):
        n1 = min(n0 + step, RWKV_COLS)
        ur_ref[0, :, n0:n1] = _mm(hb, w_ref[:, n0:n1])
    for n0 in range(0, ATT_COLS_PAD, step):
        n1 = min(n0 + step, ATT_COLS_PAD)
        ua_ref[0, :, n0:n1] = _mm(hb, w_ref[:, RWKV_COLS + n0:RWKV_COLS + n1])


def _ln_inproj(x, meta_pad, g, b, w):
    B, S, D = x.shape
    nblk = (S + PAD_ROWS) // ROWS_A
    Lp = S + PAD_ROWS
    ncols = RWKV_COLS + ATT_COLS_PAD
    return pl.pallas_call(
        _ln_inproj_body,
        grid=(B, nblk),
        in_specs=[
            pl.BlockSpec((1, ROWS_A, D), lambda b_, i: (b_, jnp.maximum(i - 1, 0), 0)),
            _const_spec((ROWS_A, D)),
            _const_spec((1, D)),
            _const_spec((1, D)),
            _const_spec((D, ncols)),
        ],
        out_specs=[
            pl.BlockSpec((1, ROWS_A, RWKV_COLS), lambda b_, i: (b_, i, 0)),
            pl.BlockSpec((1, ROWS_A, ATT_COLS_PAD), lambda b_, i: (b_, i, 0)),
        ],
        out_shape=[
            jax.ShapeDtypeStruct((B, Lp, RWKV_COLS), F32),
            jax.ShapeDtypeStruct((B, Lp, ATT_COLS_PAD), F32),
        ],
        compiler_params=_params("parallel", "arbitrary"),
        name="ln_inproj",
    )(x, meta_pad, g, b, w)


def _split3(x):
    hi = x.astype(BF16)
    r1 = x - hi.astype(F32)
    mid = r1.astype(BF16)
    return hi, mid, (r1 - mid.astype(F32)).astype(BF16)


def _mm_exact_rhs(x, m):
    return sum(_mm(p, m) for p in _split3(x))


def _mm_exact_lhs(m, x):
    return sum(_mm(m, p) for p in _split3(x))


def _rwkv_prep_body(u_ref, prev_ref, mu_ref, w0_ref, w2_ref, a0_ref, a2_ref, g2_ref, kk_ref, ka_ref, rk_ref,
                    hs_ref, hb_ref, tri_ref,
                    rp_ref, kp_ref, bp_ref, ap_ref, v_ref, pc_ref, g_ref, bonus_ref):
    blk = pl.program_id(1)
    u = u_ref[0]
    prev = jnp.where(blk == 0, 0.0, prev_ref[0][7:8, :])
    row = lax.broadcasted_iota(I32, (ROWS_R, 1), 0)
    shifted = jnp.where(row == 0, prev, pltpu.roll(u, 1, 0))
    ul = u + (shifted - u) * mu_ref[...]
    W = RWKV_WIDTH
    r = ul[:, 0:W]
    k = ul[:, W:2 * W]
    v = ul[:, 2 * W:3 * W]
    wa = ul[:, 3 * W:3 * W + 128]
    gd = ul[:, 3 * W + 128:3 * W + 256]
    w = w0_ref[...] + _mm(jnp.tanh(wa).astype(BF16), w2_ref[...])
    softplus_neg_w = jnp.maximum(-w, 0.0) + jnp.log(1.0 + jnp.exp(-jnp.abs(w)))
    logd = -jnp.exp(-softplus_neg_w - 0.5)
    a = _sigmoid(a0_ref[...] + _mm(wa.astype(BF16), a2_ref[...]))
    g_ref[0] = _mm(_sigmoid(gd).astype(BF16), g2_ref[...])
    head_sum = lambda t: _mm_exact_rhs(_mm_exact_rhs(t, hs_ref[...]), hb_ref[...])
    kkr = k * kk_ref[...]
    kk = kkr / jnp.maximum(jnp.sqrt(head_sum(kkr * kkr)), 1e-12)
    kmod = k * (1.0 + (a - 1.0) * ka_ref[...])
    bonus_ref[0] = head_sum(r * kmod * rk_ref[...]) * v
    v_ref[0] = v
    cum = _mm_exact_lhs(tri_ref[...], logd)
    rp_ref[0] = r * jnp.exp(cum)
    einv = jnp.exp(-cum)
    kp_ref[0] = kmod * einv
    bp_ref[0] = kk * a * einv
    ap_ref[0] = -kk * jnp.exp(cum - logd)
    for c in range(ROWS_R // CHUNK):
        last = c * CHUNK + CHUNK - 1
        pc_ref[0, c] = jnp.broadcast_to(jnp.exp(cum[last:last + 1]), (8, W))


def _rwkv_prep(u_r, mu, w0, w2p, a0, a2p, g2, k_k, k_a, r_k, hs, hb, tri):
    B, Lp, _ = u_r.shape
    nblk = Lp // ROWS_R
    W = RWKV_WIDTH
    row_spec = pl.BlockSpec((1, ROWS_R, W), lambda b_, i: (b_, i, 0))
    row_shape = jax.ShapeDtypeStruct((B, Lp, W), F32)
    cpb = ROWS_R // CHUNK
    return pl.pallas_call(
        _rwkv_prep_body,
        grid=(B, nblk),
        in_specs=[
            pl.BlockSpec((1, ROWS_R, RWKV_COLS), lambda b_, i: (b_, i, 0)),
            pl.BlockSpec((1, 8, RWKV_COLS), lambda b_, i: (b_, jnp.maximum(i * (ROWS_R // 8) - 1, 0), 0)),
            _const_spec((1, RWKV_COLS)),
            _const_spec((1, W)),
            _const_spec((128, W)),
            _const_spec((1, W)),
            _const_spec((128, W)),
            _const_spec((128, W)),
            _const_spec((1, W)),
            _const_spec((1, W)),
            _const_spec((1, W)),
            _const_spec((W, LANES)),
            _const_spec((LANES, W)),
            _const_spec((ROWS_R, ROWS_R)),
        ],
        out_specs=[row_spec] * 5 + [pl.BlockSpec((1, cpb, 8, W), lambda b_, i: (b_, i, 0, 0))] + [row_spec] * 2,
        out_shape=[row_shape] * 5 + [jax.ShapeDtypeStruct((B, Lp // CHUNK, 8, W), F32)] + [row_shape] * 2,
        compiler_params=_params("parallel", "arbitrary"),
        name="rwkv_prep",
    )(u_r, u_r, mu, w0, w2p, a0, a2p, g2, k_k, k_a, r_k, hs, hb, tri)


def _rwkv_scan_body(rp_ref, kp_ref, bp_ref, ap_ref, v_ref, pc_ref, g_ref, bonus_ref, lg_ref, lb_ref, o_ref, s_ref,
                    *, nb):
    c = pl.program_id(1)

    @pl.when(c == 0)
    def _():
        s_ref[...] = jnp.zeros_like(s_ref)

    @pl.when(c < OFF // CHUNK)
    def _():
        o_ref[...] = jnp.zeros_like(o_ref)

    @pl.when(c >= OFF // CHUNK)
    def _():
        _rwkv_chunk(rp_ref, kp_ref, bp_ref, ap_ref, v_ref, pc_ref, g_ref, bonus_ref, lg_ref, lb_ref, o_ref, s_ref, nb)


def _split(x):
    hi = x.astype(BF16)
    return hi, (x - hi.astype(F32)).astype(BF16)


def _dot3(a, b, dims):
    dg = lambda p, q: lax.dot_general(p, q, (dims, ((), ())), preferred_element_type=F32)
    return dg(a[0], b[0]) + dg(a[0], b[1]) + dg(a[1], b[0])


_NN = ((1,), (0,))
_NT = ((1,), (1,))
_TN = ((0,), (0,))


def _rwkv_chunk(rp_ref, kp_ref, bp_ref, ap_ref, v_ref, pc_ref, g_ref, bonus_ref, lg_ref, lb_ref, o_ref, s_ref, nb):
    C, N = CHUNK, HEAD_DIM
    ri = lax.broadcasted_iota(I32, (C, C), 0)
    ci = lax.broadcasted_iota(I32, (C, C), 1)
    strict = ri > ci
    incl = ri >= ci
    eye = jnp.where(ri == ci, 1.0, 0.0)
    units = [(b, slice(h * N, (h + 1) * N)) for b in range(nb) for h in range(N_HEADS)]
    ids = range(len(units))
    rows2 = lambda top, bottom: jnp.concatenate([top, bottom], axis=0)
    pcs = [pc_ref[b, 0, 0:1, sl] for b, sl in units]
    ar = [_split(rows2(ap_ref[b, :, sl], rp_ref[b, :, sl])) for b, sl in units]
    bk = [_split(rows2(bp_ref[b, :, sl], kp_ref[b, :, sl])) for b, sl in units]
    v_ = [_split(v_ref[b, :, sl]) for b, sl in units]
    s0 = [s_ref[i] for i in ids]
    s0s = [_split(s) for s in s0]
    gram = [_dot3(ar[i], bk[i], _NT) for i in ids]
    a_ab = [jnp.where(strict, gram[i][0:C, 0:C], 0.0) for i in ids]
    a_ak = [jnp.where(strict, gram[i][0:C, C:2 * C], 0.0) for i in ids]
    a_rb = [_split(jnp.where(incl, gram[i][C:2 * C, 0:C], 0.0)) for i in ids]
    a_rk = [jnp.where(incl, gram[i][C:2 * C, C:2 * C], 0.0) for i in ids]
    t = [eye + a_ab[i] for i in ids]
    pb = [a_ab[i].astype(BF16) for i in ids]
    for _ in range(C.bit_length() - 2):
        pb = [_mm(pb[i], pb[i]).astype(BF16) for i in ids]
        t = [t[i] + _mm(t[i].astype(BF16), pb[i]) for i in ids]
    ts = [_split(t[i]) for i in ids]
    resid = [(eye - t[i]) + _dot3(_split(a_ab[i]), ts[i], _NN) for i in ids]
    t = [t[i] + _mm(ts[i][0], resid[i].astype(BF16)) for i in ids]
    sp = [_dot3(ar[i], s0s[i], _NT) for i in ids]
    av = [_dot3(_split(rows2(a_ak[i], a_rk[i])), v_[i], _NN) for i in ids]
    u_ = [_dot3(_split(t[i]), _split(sp[i][0:C] + av[i][0:C]), _NN) for i in ids]
    for i, (b, sl) in enumerate(units):
        o = sp[i][C:2 * C] + av[i][C:2 * C] + _dot3(a_rb[i], _split(u_[i]), _NN)
        oc = o - jnp.mean(o, axis=1, keepdims=True)
        var = jnp.mean(oc * oc, axis=1, keepdims=True)
        y = oc * lax.rsqrt(var + GN_EPS) * lg_ref[:, sl] + lb_ref[:, sl]
        o_ref[b, :, sl] = (y + bonus_ref[b, :, sl]) * g_ref[b, :, sl]
    for i, (b, sl) in enumerate(units):
        vu = _split(rows2(v_ref[b, :, sl], u_[i]))
        kb = _split(rows2(kp_ref[b, :, sl], bp_ref[b, :, sl]) * pcs[i])
        s_ref[i] = s0[i] * pcs[i] + _dot3(vu, kb, _TN)


def _rwkv_scan(rp, kp, bp, ap, v, pc, g, bonus, lg, lb):
    B, Lp, W = rp.shape
    nch = Lp // CHUNK
    nb = SCAN_BATCH if B % SCAN_BATCH == 0 else 1
    row_spec = pl.BlockSpec((nb, CHUNK, W), lambda b_, c: (b_, c, 0))
    return pl.pallas_call(
        functools.partial(_rwkv_scan_body, nb=nb),
        grid=(B // nb, nch),
        in_specs=[row_spec] * 5 + [pl.BlockSpec((nb, 1, 8, W), lambda b_, c: (b_, c, 0, 0))] + [row_spec] * 2
        + [_const_spec((1, W))] * 2,
        out_specs=row_spec,
        out_shape=jax.ShapeDtypeStruct((B, Lp, W), F32),
        scratch_shapes=[pltpu.VMEM((nb * N_HEADS, HEAD_DIM, HEAD_DIM), F32)],
        compiler_params=_params("parallel", "arbitrary"),
        name="rwkv_scan",
    )(rp, kp, bp, ap, v, pc, g, bonus, lg, lb)


def _rope(x, cos, sin, half, first):
    width = x.shape[1]
    rot = jnp.where(first, pltpu.roll(x, width - half, 1), pltpu.roll(x, half, 1))
    return x * cos + rot * sin


def _dsa_prep_body(u_ref, qg_ref, wuq_ref, wiq_ref, kng_ref, knb_ref, cf_ref, sf_ref, cp_ref, sp_ref,
                   qt_ref, k_ref, vt_ref, qit_ref, kx_ref, wt_ref):
    u = u_ref[0]
    cq = u[:, 0:Q_LORA_RANK]
    k = u[:, Q_LORA_RANK:Q_LORA_RANK + ATT_WIDTH]
    v = u[:, Q_LORA_RANK + ATT_WIDTH:Q_LORA_RANK + 2 * ATT_WIDTH]
    tail = u[:, Q_LORA_RANK + 2 * ATT_WIDTH:]
    cqn = (cq * lax.rsqrt(jnp.mean(cq * cq, -1, keepdims=True) + RMS_EPS) * qg_ref[...]).astype(BF16)
    reps = ATT_WIDTH // LANES
    cf = jnp.concatenate([cf_ref[...]] * reps, axis=1)
    sf = jnp.concatenate([sf_ref[...]] * reps, axis=1)
    cp = jnp.concatenate([cp_ref[...]] * reps, axis=1)
    sp = jnp.concatenate([sp_ref[...]] * reps, axis=1)
    lane_w = lax.broadcasted_iota(I32, (1, ATT_WIDTH), 1) % HEAD_DIM
    first_f = lane_w < HEAD_DIM // 2
    first_p = lane_w < IDX_ROPE_DIM // 2
    q = _rope(_mm(cqn, wuq_ref[...]), cf, sf, HEAD_DIM // 2, first_f)
    qt_ref[0] = (q * (HEAD_DIM ** -0.5 * LOG2_E)).T.astype(BF16)
    qi = _rope(_mm(cqn, wiq_ref[...]), cp, sp, IDX_ROPE_DIM // 2, first_p)
    qit_ref[0] = qi.T.astype(BF16)
    k_ref[0] = _rope(k, cf, sf, HEAD_DIM // 2, first_f).astype(BF16)
    vt_ref[0] = v.T.astype(BF16)
    lane = lax.broadcasted_iota(I32, (1, LANES), 1)
    is_key = lane < IDX_DIM
    mu = jnp.sum(jnp.where(is_key, tail, 0.0), -1, keepdims=True) * (1.0 / IDX_DIM)
    tc = jnp.where(is_key, tail - mu, 0.0)
    var = jnp.sum(tc * tc, -1, keepdims=True) * (1.0 / IDX_DIM)
    kn = tc * lax.rsqrt(var + LN_EPS) * kng_ref[...] + knb_ref[...]
    kn = _rope(kn, cp_ref[...], sp_ref[...], IDX_ROPE_DIM // 2, (lane % HEAD_DIM) < IDX_ROPE_DIM // 2)
    kn = kn * (IDX_DIM ** -0.5)
    kx_ref[0] = jnp.where(is_key, kn, pltpu.roll(kn, IDX_DIM, 1)).astype(BF16)
    wt_ref[0] = (tail * (IDX_HEADS ** -0.5)).T


def _dsa_prep(u_a, qg, wuq, wiq, kng, knb, cf, sf, cp, sp):
    B, Lp, _ = u_a.shape
    W = ATT_WIDTH
    row = lambda width: pl.BlockSpec((1, ROWS_A, width), lambda b_, i: (b_, i, 0))
    col = lambda height: pl.BlockSpec((1, height, ROWS_A), lambda b_, i: (b_, 0, i))
    tab = pl.BlockSpec((ROWS_A, LANES), lambda b_, i: (i, 0))
    return pl.pallas_call(
        _dsa_prep_body,
        grid=(B, Lp // ROWS_A),
        in_specs=[row(ATT_COLS_PAD), _const_spec((1, Q_LORA_RANK)), _const_spec((Q_LORA_RANK, W)),
                  _const_spec((Q_LORA_RANK, W)), _const_spec((1, LANES)), _const_spec((1, LANES)),
                  tab, tab, tab, tab],
        out_specs=[col(W), row(W), col(W), col(W), row(LANES), col(LANES)],
        out_shape=[jax.ShapeDtypeStruct((B, W, Lp), BF16), jax.ShapeDtypeStruct((B, Lp, W), BF16),
                   jax.ShapeDtypeStruct((B, W, Lp), BF16), jax.ShapeDtypeStruct((B, W, Lp), BF16),
                   jax.ShapeDtypeStruct((B, Lp, LANES), BF16), jax.ShapeDtypeStruct((B, LANES, Lp), F32)],
        compiler_params=_params("parallel", "arbitrary"),
        name="dsa_prep",
    )(u_a, qg, wuq, wiq, kng, knb, cf, sf, cp, sp)


def _fold_rows(x, op=jnp.add):
    parts = [x[8 * r:8 * r + 8] for r in range(x.shape[0] // 8)]
    while len(parts) > 1:
        parts = [op(a, b) for a, b in zip(parts[0::2], parts[1::2])] + parts[len(parts) & ~1:]
    return parts[0]


def _dsa_attn_body(qt_ref, qit_ref, wt_ref, k_ref, vt_ref, kx_ref, o_ref, khi_ref, klo_ref, sc_ref, acc_ref,
                   j_ref, s_ref, pe_ref, *, ksel):
    i = pl.program_id(1)
    nkc = (i * TQ + TQ - 1) // KC + 1
    kf = float(ksel)
    tcol = i * TQ + lax.broadcasted_iota(I32, (1, TQ), 1)
    row_in_pair = lax.broadcasted_iota(I32, (LANES, 1), 0)

    def head_operands(ref):
        out = []
        for h in range(N_HEADS):
            pair = ref[0, LANES * (h // 2):LANES * (h // 2 + 1), :]
            keep = (row_in_pair < HEAD_DIM) if h % 2 == 0 else (row_in_pair >= HEAD_DIM)
            out.append(jnp.where(keep, pair, jnp.zeros_like(pair)))
        return out

    def key_rows(ks):
        return ks + lax.broadcasted_iota(I32, (KC, 1), 0)

    qis = head_operands(qit_ref)
    wrows = [wt_ref[0, IDX_DIM + h:IDX_DIM + h + 1, :] for h in range(IDX_HEADS)]

    def score_chunk(kc, carry):
        lo8, hi8 = carry
        ks = pl.multiple_of(kc * KC, KC)
        kx = kx_ref[0, pl.ds(ks, KC), :]
        sc = jnp.zeros((KC, TQ), F32)
        for h in range(IDX_HEADS):
            sc = sc + jnp.maximum(_mm(kx, qis[h]), 0.0) * wrows[h]
        sc = sc + 0.0
        krow = key_rows(ks)
        sc = jnp.where(krow >= OFF, sc, MASKED)
        sc = jnp.where(krow <= tcol, sc, MASKED)
        sc_ref[pl.ds(ks, KC), :] = sc
        bits = lax.bitcast_convert_type(sc, I32)
        key = jnp.where(bits >= 0, bits, bits ^ jnp.int32(0x7FFFFFFF))
        khi_ref[pl.ds(ks, KC), :] = lax.shift_right_arithmetic(key, 16).astype(I16)
        klo_ref[pl.ds(ks, KC), :] = ((key & jnp.int32(0xFFFF)) - HALF16).astype(I16)
        lo8 = jnp.minimum(lo8, _fold_rows(jnp.where(sc <= MASKED, ABOVE_ALL, sc), jnp.minimum))
        hi8 = jnp.maximum(hi8, _fold_rows(sc, jnp.maximum))
        return lo8, hi8

    lo8, hi8 = lax.fori_loop(0, nkc, score_chunk,
                             (jnp.full((8, TQ), ABOVE_ALL, F32), jnp.full((8, TQ), MASKED, F32)))
    smin = jnp.min(lo8, axis=0, keepdims=True)
    smax = jnp.max(hi8, axis=0, keepdims=True)

    def scan_chunks(fn, init):
        def body(kc, carry):
            ks = pl.multiple_of(kc * KC, KC)
            return fn(carry, ks)
        return lax.fori_loop(0, nkc, body, init)

    zeros8 = jnp.zeros((8, TQ), F32)

    def count_where(ref, pred):
        cnt = scan_chunks(lambda c, ks: c + _fold_rows(pred(ref[pl.ds(ks, KC), :], ks)), zeros8)
        return jnp.sum(cnt, axis=0, keepdims=True)

    def count16(ref, pred):
        def fn(cnt, ks):
            m = pred(ref[pl.ds(ks, KC), :])
            parts = [m[16 * r:16 * r + 16] for r in range(KC // 16)]
            while len(parts) > 1:
                parts = [a + b for a, b in zip(parts[0::2], parts[1::2])]
            return cnt + parts[0]
        cnt = scan_chunks(fn, jnp.zeros((16, TQ), I16))
        return jnp.sum(cnt.astype(I32), axis=0, keepdims=True)

    one16, zero16 = jnp.int16(1), jnp.int16(0)

    def radix16(ref, target):
        def bit(bi, prefix):
            cand = prefix | lax.shift_left(jnp.int32(1), 15 - bi)
            cand16 = (cand - HALF16).astype(I16)
            cnt = count16(ref, lambda x: jnp.where(x >= cand16, one16, zero16))
            return jnp.where(cnt >= target, cand, prefix)
        return lax.fori_loop(0, 16, bit, jnp.zeros((1, TQ), I32))

    k_int = jnp.full((1, TQ), ksel, I32)
    thr_hi = radix16(khi_ref, k_int) - HALF16
    thr_hi16 = thr_hi.astype(I16)
    above = count16(khi_ref, lambda x: jnp.where(x > thr_hi16, one16, zero16))

    def keep_low_of_ties(carry, ks):
        rows = pl.ds(ks, KC)
        klo_ref[rows, :] = jnp.where(khi_ref[rows, :] == thr_hi16, klo_ref[rows, :], jnp.int16(-HALF16))
        return carry

    scan_chunks(keep_low_of_ties, 0)
    thr_lo = radix16(klo_ref, k_int - above)
    thr_key = lax.shift_left(thr_hi, 16) | thr_lo
    cand0 = lax.bitcast_convert_type(jnp.where(thr_key >= 0, thr_key, thr_key ^ jnp.int32(0x7FFFFFFF)), F32)

    n_adm = jnp.maximum(tcol - (OFF - 1), 0).astype(F32)
    searching = n_adm > kf

    def probe(mid):
        def fn(carry, ks):
            cnt, vmin = carry
            s = sc_ref[pl.ds(ks, KC), :]
            ge = s >= mid
            return (cnt + _fold_rows(jnp.where(ge, 1.0, 0.0)),
                    jnp.minimum(vmin, _fold_rows(jnp.where(ge, s, ABOVE_ALL), jnp.minimum)))
        cnt, vmin = scan_chunks(fn, (zeros8, jnp.full((8, TQ), ABOVE_ALL, F32)))
        return jnp.sum(cnt, axis=0, keepdims=True), jnp.min(vmin, axis=0, keepdims=True)

    def refine(state):
        it, lo, hi, c_lo, c_gt, done, _ = state
        mid = jnp.where(it == 0, jnp.where(searching, cand0, lo), lo + 0.5 * (hi - lo))
        c_mid, v_mid = probe(mid)
        up = c_mid >= kf
        lo_n = jnp.where(up, v_mid, lo)
        hi_n = jnp.where(up, hi, mid)
        c_lo_n = jnp.where(up, c_mid, c_lo)
        c_gt_n = count_where(sc_ref, lambda s, ks: jnp.where(s > lo_n, 1.0, 0.0))
        stalled = jnp.where(it > 0, jnp.where(mid <= lo, 1.0, jnp.where(mid >= hi, 1.0, 0.0)), 0.0)
        fin = jnp.maximum(jnp.where(c_gt_n < kf, 1.0, 0.0), stalled)
        frozen = done > 0.0
        keep = lambda old, new_: jnp.where(frozen, old, new_)
        done_n = jnp.maximum(done, fin)
        return (it + 1, keep(lo, lo_n), keep(hi, hi_n), keep(c_lo, c_lo_n), keep(c_gt, c_gt_n), done_n,
                jnp.max(1.0 - done_n))

    done0 = jnp.where(searching, 0.0, 1.0)
    state0 = (jnp.int32(0), smin, smax + (jnp.abs(smax) + 1.0) * 1e-6, n_adm, n_adm, done0, jnp.max(1.0 - done0))
    state = lax.while_loop(lambda st: jnp.logical_and(st[6] > 0.0, st[0] < MAX_REFINE), refine, state0)
    thr = jnp.where(searching, state[1], BELOW_ALL)
    cnt_gt = jnp.where(searching, state[4], n_adm)
    cnt_eq = jnp.where(searching, state[3] - state[4], 0.0)
    need = kf - cnt_gt

    j_ref[...] = jnp.full(j_ref.shape, 2 ** 30, I32)

    @pl.when(jnp.max(cnt_eq - need) > 0.0)
    def _():
        def index_bit(bi, prefix):
            cand = prefix | lax.shift_left(jnp.int32(1), 12 - bi)
            before = count_where(
                sc_ref, lambda s, ks: jnp.where(s == thr, jnp.where(key_rows(ks) < cand, 1.0, 0.0), 0.0))
            return jnp.where(before < need, cand, prefix)
        jst = lax.fori_loop(0, 13, index_bit, jnp.zeros((1, TQ), I32))
        j_ref[...] = jnp.broadcast_to(jst, j_ref.shape)

    jstar = j_ref[0:1, :]

    qs = head_operands(qt_ref)
    acc_ref[...] = jnp.zeros_like(acc_ref)

    ones_rows = jnp.ones((16, KC), BF16)

    def attend_chunk(kc, carry):
        ms, ls = carry
        ks = pl.multiple_of(kc * KC, KC)
        sc = sc_ref[pl.ds(ks, KC), :]
        tie = jnp.where(sc == thr, jnp.where(key_rows(ks) <= jstar, 0.0, NEG), NEG)
        bias = jnp.where(sc > thr, 0.0, tie)
        chunk_max = []
        for h in range(N_HEADS):
            p = h // 2
            kp = k_ref[0, pl.ds(ks, KC), LANES * p:LANES * (p + 1)]
            s = _mm(kp, qs[h]) + bias
            s_ref[h] = s
            chunk_max.append(jnp.max(s, axis=0, keepdims=True))
        new_ms, new_ls, alphas = [], [], []
        for h in range(N_HEADS):
            m_new = jnp.maximum(ms[h], chunk_max[h])
            alphas.append(jnp.exp2(ms[h] - m_new))
            new_ms.append(m_new)
            pe_ref[h] = jnp.exp2(s_ref[h] - m_new).astype(BF16)
        for h in range(N_HEADS):
            vt = vt_ref[0, HEAD_DIM * h:HEAD_DIM * (h + 1), pl.ds(ks, KC)]
            pv = _mm(jnp.concatenate([vt, ones_rows], axis=0), pe_ref[h])
            rows = slice(HEAD_DIM * h, HEAD_DIM * (h + 1))
            acc_ref[rows, :] = acc_ref[rows, :] * alphas[h] + pv[0:HEAD_DIM]
            new_ls.append(alphas[h] * ls[h] + pv[HEAD_DIM:HEAD_DIM + 1])
        return tuple(new_ms), tuple(new_ls)

    init = (tuple(jnp.full((1, TQ), NEG, F32) for _ in range(N_HEADS)),
            tuple(jnp.zeros((1, TQ), F32) for _ in range(N_HEADS)))
    _, ls = lax.fori_loop(0, nkc, attend_chunk, init)
    for p in range(N_HEADS // 2):
        parts = [acc_ref[HEAD_DIM * h:HEAD_DIM * (h + 1), :] / ls[h] for h in (2 * p, 2 * p + 1)]
        o_ref[0, :, LANES * p:LANES * (p + 1)] = jnp.concatenate(parts, axis=0).T


def _dsa_attn(qt, k, vt, qit, kx, wt, ksel):
    B, Lp, W = k.shape
    assert Lp % KC == 0 and Lp <= 8192
    qcol = lambda height: pl.BlockSpec((1, height, TQ), lambda b_, i: (b_, 0, i))
    full = lambda shape: pl.BlockSpec((1,) + shape, lambda b_, i: (b_, 0, 0))
    return pl.pallas_call(
        functools.partial(_dsa_attn_body, ksel=ksel),
        grid=(B, Lp // TQ),
        in_specs=[qcol(W), qcol(W), qcol(LANES), full((Lp, W)), full((W, Lp)), full((Lp, LANES))],
        out_specs=pl.BlockSpec((1, TQ, W), lambda b_, i: (b_, i, 0)),
        out_shape=jax.ShapeDtypeStruct((B, Lp, W), F32),
        scratch_shapes=[
            pltpu.VMEM((Lp, TQ), I16),
            pltpu.VMEM((Lp, TQ), I16),
            pltpu.VMEM((Lp, TQ), F32),
            pltpu.VMEM((W, TQ), F32),
            pltpu.VMEM((8, TQ), I32),
            pltpu.VMEM((N_HEADS, KC, TQ), F32),
            pltpu.VMEM((N_HEADS, KC, TQ), BF16),
        ],
        compiler_params=_params("parallel", "arbitrary"),
        name="dsa_attn",
    )(qt, qit, wt, k, vt, kx)


def _outproj_router_body(x_ref, yr_ref, ya_ref, eg_ref, eb_ref, wo_ref, g1_ref, b1_ref, wrh_ref, wrl_ref, br_ref,
                         h_ref, grp_ref):
    h0 = _layer_norm(x_ref[0], eg_ref[...], eb_ref[...])
    mix = (_mm(yr_ref[0].astype(BF16), wo_ref[0:RWKV_WIDTH, :])
           + _mm(ya_ref[0].astype(BF16), wo_ref[RWKV_WIDTH:, :]))
    h1 = _layer_norm(DN_ALPHA * h0 + mix, g1_ref[...], b1_ref[...])
    h_ref[0, :, 0:D_MODEL] = h1
    logits = _dot3(_split(h1), (wrh_ref[...], wrl_ref[...]), _NN) + br_ref[...]
    lane = lax.broadcasted_iota(I32, (1, LANES), 1)
    lanef = lane.astype(F32)
    low = -3e38
    lgm = jnp.where(lane < N_GROUPS, logits, low)
    gmax = jnp.max(lgm, axis=1, keepdims=True)
    gsel = jnp.min(jnp.where(lgm == gmax, lanef, 1e9), axis=1, keepdims=True)
    gsum = jnp.sum(jnp.where(lane < N_GROUPS, jnp.exp(lgm - gmax), 0.0), axis=1, keepdims=True)
    group_of_lane = ((lane - EXPERT_LANE0) // EXPERTS_PER_GROUP).astype(F32)
    lem = jnp.where(group_of_lane == gsel, logits, low)
    m1 = jnp.max(lem, axis=1, keepdims=True)
    i1 = jnp.min(jnp.where(lem == m1, lanef, 1e9), axis=1, keepdims=True)
    lem2 = jnp.where(lanef == i1, low, lem)
    m2 = jnp.max(lem2, axis=1, keepdims=True)
    i2 = jnp.min(jnp.where(lem2 == m2, lanef, 1e9), axis=1, keepdims=True)
    e2 = jnp.exp(m2 - m1)
    w1 = 1.0 / (1.0 + e2)
    w2 = e2 / (1.0 + e2)
    gates = jnp.where(lanef == i1, w1, jnp.where(lanef == i2, w2, 0.0)) / gsum
    h_ref[0, :, D_MODEL:] = gates
    grp_ref[0] = jnp.broadcast_to(gsel, (ROWS_A, LANES))


def _outproj_router(x, y_r, y_a, eg, eb, wo, g1, b1, wrh, wrl, br):
    B, S, D = x.shape
    skip = PAD_ROWS // ROWS_A
    xrow = pl.BlockSpec((1, ROWS_A, D), lambda b_, i: (b_, i, 0))
    yrow = pl.BlockSpec((1, ROWS_A, RWKV_WIDTH), lambda b_, i: (b_, i + skip, 0))
    vec = _const_spec((1, D))
    return pl.pallas_call(
        _outproj_router_body,
        grid=(B, S // ROWS_A),
        in_specs=[xrow, yrow, yrow, vec, vec, _const_spec((D, D)), vec, vec,
                  _const_spec((D, LANES)), _const_spec((D, LANES)), _const_spec((1, LANES))],
        out_specs=[pl.BlockSpec((1, ROWS_A, D + LANES), lambda b_, i: (b_, i, 0)),
                   pl.BlockSpec((1, ROWS_A, LANES), lambda b_, i: (b_, i, 0))],
        out_shape=[jax.ShapeDtypeStruct((B, S, D + LANES), F32), jax.ShapeDtypeStruct((B, S, LANES), F32)],
        compiler_params=_params("parallel", "arbitrary"),
        name="outproj_router",
    )(x, y_r, y_a, eg, eb, wo, g1, b1, wrh, wrl, br)


def _moe_body(tgrp_ref, tcnt_ref, idx_ref, idx_next_ref, idx_prev_ref, hx_ref, wg_ref, wu_ref, wd_ref, g2_ref, b2_ref,
              out_ref, xg_ref, acc_ref, hb_ref, ob_ref, gsem, ssem):
    i = pl.program_id(0)
    e = pl.program_id(1)
    n_tiles = pl.num_programs(0)
    n = tcnt_ref[i]
    slot = i % 2
    D = D_MODEL

    def row_gather(rows_ref, r, to_slot):
        return pltpu.make_async_copy(hx_ref.at[pl.ds(rows_ref[0, 0, r], 1)], xg_ref.at[to_slot, pl.ds(r, 1)],
                                     gsem.at[to_slot])

    def row_scatter(rows_ref, r, from_slot):
        return pltpu.make_async_copy(ob_ref.at[from_slot, pl.ds(r, 1)], out_ref.at[pl.ds(rows_ref[0, 0, r], 1)],
                                     ssem.at[from_slot])

    def for_rows(count, fn):
        lax.fori_loop(0, count, lambda r, c: (fn(r), c)[1], 0)

    @pl.when(jnp.logical_and(i == 0, e == 0))
    def _():
        for_rows(n, lambda r: row_gather(idx_ref, r, slot).start())

    has_next = i + 1 < n_tiles
    n_next = tcnt_ref[jnp.minimum(i + 1, n_tiles - 1)]
    spread = jnp.logical_and(has_next, n_next == TMG)
    slice_rows = TMG // EXPERTS_PER_GROUP

    @pl.when(jnp.logical_and(jnp.logical_and(e == 1, has_next), n_next < TMG))
    def _():
        for_rows(n_next, lambda r: row_gather(idx_next_ref, r, 1 - slot).start())

    @pl.when(jnp.logical_and(e == 0, n == TMG))
    def _():
        pltpu.make_async_copy(hx_ref.at[pl.ds(0, TMG)], xg_ref.at[slot], gsem.at[slot]).wait()

    @pl.when(jnp.logical_and(e == 0, n < TMG))
    def _():
        for_rows(n, lambda r: row_gather(idx_ref, r, slot).wait())

    @pl.when(jnp.logical_and(e == 0, n > 0))
    def _():
        acc_ref[...] = jnp.zeros_like(acc_ref)
        hb_ref[...] = xg_ref[slot, :, 0:D].astype(BF16)

    def expert_step():
        t = hb_ref[...]
        lane = lax.broadcasted_iota(I32, (1, LANES), 1)
        gate_lane = EXPERT_LANE0 + tgrp_ref[i] * EXPERTS_PER_GROUP + e
        gcol = jnp.sum(jnp.where(lane == gate_lane, xg_ref[slot, :, D:], 0.0), axis=1, keepdims=True)
        a = _mm(t, wg_ref[0])
        hid = a * _sigmoid(a) * _mm(t, wu_ref[0]) * gcol
        acc_ref[...] += _mm(hid.astype(BF16), wd_ref[0])

    n_prev = tcnt_ref[jnp.maximum(i - 1, 0)]
    drain_prev = jnp.logical_and(jnp.logical_and(i > 0, n_prev == TMG), n > 0)
    deferred = jnp.logical_and(jnp.logical_and(n == TMG, has_next), n_next > 0)

    def next_gather_slice():
        for j in range(slice_rows):
            row_gather(idx_next_ref, e * slice_rows + j, 1 - slot).start()

    def prev_scatter_slice():
        for j in range(slice_rows):
            row_scatter(idx_prev_ref, e * slice_rows + j, 1 - slot).start(priority=j % 2)

    for do_gather in (False, True):
        for do_scatter in (False, True):
            cond = jnp.logical_and(n > 0, jnp.logical_and(spread == do_gather, drain_prev == do_scatter))

            @pl.when(cond)
            def _(do_gather=do_gather, do_scatter=do_scatter):
                if do_gather:
                    next_gather_slice()
                if do_scatter:
                    prev_scatter_slice()
                expert_step()

    last = e == EXPERTS_PER_GROUP - 1

    @pl.when(jnp.logical_and(last, drain_prev))
    def _():
        pltpu.make_async_copy(ob_ref.at[1 - slot], out_ref.at[pl.ds(0, TMG)], ssem.at[1 - slot]).wait()

    @pl.when(jnp.logical_and(last, n > 0))
    def _():
        ob_ref[slot] = _layer_norm(DN_ALPHA * xg_ref[slot, :, 0:D] + acc_ref[...], g2_ref[...], b2_ref[...])

    @pl.when(jnp.logical_and(last, jnp.logical_and(n == TMG, jnp.logical_not(deferred))))
    def _():
        lax.fori_loop(0, TMG, lambda r, c: (row_scatter(idx_ref, r, slot).start(), c)[1], 0, unroll=8)
        pltpu.make_async_copy(ob_ref.at[slot], out_ref.at[pl.ds(0, TMG)], ssem.at[slot]).wait()

    @pl.when(jnp.logical_and(last, n < TMG))
    def _():
        for_rows(n, lambda r: row_scatter(idx_ref, r, slot).start())
        for_rows(n, lambda r: row_scatter(idx_ref, r, slot).wait())


def _moe(hx, tile_group, tile_count, tile_rows, wg, wu, wd, g2, b2):
    T, DX = hx.shape
    D = D_MODEL
    n_tiles = tile_rows.shape[0]
    wspec = lambda shape: pl.BlockSpec(
        (1,) + shape, lambda i, e, tg, tc: (tg[i] * EXPERTS_PER_GROUP + e, 0, 0))
    vec = pl.BlockSpec((1, D), lambda i, e, tg, tc: (0, 0))
    rows_of = lambda step: pl.BlockSpec(
        (1, 1, TMG), lambda i, e, tg, tc: (jnp.clip(i + step, 0, n_tiles - 1), 0, 0), memory_space=pltpu.SMEM)
    return pl.pallas_call(
        _moe_body,
        grid_spec=pltpu.PrefetchScalarGridSpec(
            num_scalar_prefetch=2,
            grid=(n_tiles, EXPERTS_PER_GROUP),
            in_specs=[rows_of(0), rows_of(1), rows_of(-1), pl.BlockSpec(memory_space=pl.ANY),
                      wspec((D, D_EXPERT)), wspec((D, D_EXPERT)), wspec((D_EXPERT, D)), vec, vec],
            out_specs=pl.BlockSpec(memory_space=pl.ANY),
            scratch_shapes=[pltpu.VMEM((2, TMG, DX), F32), pltpu.VMEM((TMG, D), F32), pltpu.VMEM((TMG, D), BF16),
                            pltpu.VMEM((2, TMG, D), F32), pltpu.SemaphoreType.DMA((2,)),
                            pltpu.SemaphoreType.DMA((2,))],
        ),
        out_shape=jax.ShapeDtypeStruct((T, D), F32),
        compiler_params=_params("arbitrary", "arbitrary"),
        name="moe",
    )(tile_group, tile_count, tile_rows, tile_rows, tile_rows, hx, wg, wu, wd, g2, b2)


def _group_tiles(grp, n_tiles):
    T = grp.shape[0]
    onehot = (grp[:, None] == jnp.arange(N_GROUPS)[None, :]).astype(I32)
    rank = jnp.cumsum(onehot, axis=0) - onehot
    count = jnp.sum(onehot, axis=0)
    tiles_per_group = (count + TMG - 1) // TMG
    first_tile = jnp.cumsum(tiles_per_group) - tiles_per_group
    pos = first_tile[grp] * TMG + jnp.sum(rank * onehot, axis=1)
    tile_rows = jnp.zeros((n_tiles * TMG,), I32).at[pos].set(jnp.arange(T, dtype=I32)).reshape(n_tiles, 1, TMG)
    tile = jnp.arange(n_tiles)
    tile_group = jnp.clip(jnp.sum((tile[:, None] >= first_tile[None, :]).astype(I32), axis=1) - 1, 0, N_GROUPS - 1)
    in_group = tile - first_tile[tile_group]
    tile_count = jnp.clip(count[tile_group] - in_group * TMG, 0, TMG)
    tile_count = jnp.where(in_group < tiles_per_group[tile_group], tile_count, 0)
    return tile_group.astype(I32), tile_count.astype(I32), tile_rows


def _rope_tables(Lp):
    pos = jnp.maximum(jnp.arange(Lp, dtype=I32) - OFF, 0).astype(F32)
    j = jnp.arange(LANES) % HEAD_DIM

    def table(half, rot_dim):
        inv = 1.0 / (ROPE_THETA ** (jnp.arange(half, dtype=F32) / half))
        ang = pos[:, None] * inv[None, :]
        cos, sin = jnp.cos(ang)[:, j % half], jnp.sin(ang)[:, j % half]
        rotated = (j < rot_dim)[None, :]
        sign = jnp.where(j < half, -1.0, 1.0)[None, :]
        return jnp.where(rotated, cos, 1.0), jnp.where(rotated, sin * sign, 0.0)

    cf, sf = table(HEAD_DIM // 2, HEAD_DIM)
    cp, sp = table(IDX_ROPE_DIM // 2, IDX_ROPE_DIM)
    return cf, sf, cp, sp


def _block_ones(n, block):
    idx = jnp.arange(n) // block
    return (idx[:, None] == idx[None, :]).astype(F32)


def kernel(x, meta_tokens, ln_emb_g, ln_emb_b, w_in, rw_mu, rw_w0, rw_w2, rw_a0, rw_a2, rw_g2, rw_kk, rw_ka,
           rw_rk, rw_lnx_g, rw_lnx_b, att_qnorm_g, att_wuq, idx_wq, idx_knorm_g, idx_knorm_b, w_out, ln1_g,
           ln1_b, rt_grp_w, rt_grp_b, rt_exp_w, rt_exp_b, ex_w_gate, ex_w_up, ex_w_down, ln2_g, ln2_b):
    B, S, D = x.shape
    assert w_in.shape[0] == 1 and D == D_MODEL and (B * S) % TMG == 0
    Lp = S + PAD_ROWS
    ksel = min(INDEX_TOPK, S // 4)
    row = lambda t: t.reshape(1, -1)
    W = RWKV_WIDTH

    meta_pad = jnp.zeros((PAD_ROWS, D), F32).at[OFF:].set(meta_tokens)
    w_in_p = jnp.pad(w_in[0], ((0, 0), (0, ATT_COLS_PAD - ATT_COLS))).astype(BF16)
    u_r, u_a = _ln_inproj(x, meta_pad, row(ln_emb_g), row(ln_emb_b), w_in_p)

    w2p = jnp.concatenate([rw_w2[0], jnp.zeros((ICLR_RANK, W), F32)], 0).astype(BF16)
    a2p = jnp.concatenate([jnp.zeros((DECAY_RANK, W), F32), rw_a2[0]], 0).astype(BF16)
    head_of_lane = jnp.arange(W) // HEAD_DIM
    hs = (head_of_lane[:, None] == jnp.arange(LANES)[None, :]).astype(BF16)
    tri = (jnp.arange(ROWS_R)[:, None] >= jnp.arange(ROWS_R)[None, :]).astype(F32)
    tri = (tri * _block_ones(ROWS_R, CHUNK)).astype(BF16)
    rp, kp, bp, ap, v, pc, g, bonus = _rwkv_prep(
        u_r, row(rw_mu[0]), row(rw_w0[0]), w2p, row(rw_a0[0]), a2p, rw_g2[0].astype(BF16), row(rw_kk[0]),
        row(rw_ka[0]), row(rw_rk[0]), hs, hs.T, tri)
    y_r = _rwkv_scan(rp, kp, bp, ap, v, pc, g, bonus, row(rw_lnx_g[0]), row(rw_lnx_b[0]))

    pad_lanes = lambda t: jnp.pad(t, (0, LANES - t.shape[0])).reshape(1, LANES)
    cf, sf, cp, sp = _rope_tables(Lp)
    qt, k, vt, qit, kx, wt = _dsa_prep(
        u_a, row(att_qnorm_g[0]), att_wuq[0].astype(BF16), idx_wq[0].astype(BF16),
        pad_lanes(idx_knorm_g[0]), pad_lanes(idx_knorm_b[0]), cf, sf, cp, sp)
    y_a = _dsa_attn(qt, k, vt, qit, kx, wt, ksel)

    wr = jnp.zeros((D, LANES), F32).at[:, :N_GROUPS].set(rt_grp_w[0])
    wr = wr.at[:, EXPERT_LANE0:EXPERT_LANE0 + N_EXPERTS].set(rt_exp_w[0])
    br = jnp.zeros((1, LANES), F32).at[0, :N_GROUPS].set(rt_grp_b[0])
    br = br.at[0, EXPERT_LANE0:EXPERT_LANE0 + N_EXPERTS].set(rt_exp_b[0])
    wrh = wr.astype(BF16)
    wrl = (wr - wrh.astype(F32)).astype(BF16)
    h1, grp = _outproj_router(x, y_r, y_a, row(ln_emb_g), row(ln_emb_b), w_out[0].astype(BF16),
                                row(ln1_g[0]), row(ln1_b[0]), wrh, wrl, br)
    T = B * S
    n_tiles = T // TMG + N_GROUPS
    tile_group, tile_count, tile_rows = _group_tiles(grp.reshape(T, LANES)[:, 0].astype(I32), n_tiles)
    out = _moe(h1.reshape(T, D + LANES), tile_group, tile_count, tile_rows, ex_w_gate[0].astype(BF16),
               ex_w_up[0].astype(BF16), ex_w_down[0].astype(BF16), row(ln2_g[0]), row(ln2_b[0]))
    return out.reshape(B, S, D)
```
